```python
import math
import jax, jax.numpy as jnp
from jax import lax
import numpy as np

D_MODEL = 2048
BATCH = 4
SEQ = 2048
DEPTH = 1
DEC_BATCH = 128
DEC_SEQ = 8
PAST_LEN = 16384
PAGE_SIZE = 128

RW_HEAD = 64
RW_W = D_MODEL
RW_HEADS = RW_W // RW_HEAD
DECAY_LORA = 96
AICL_LORA = 96
GATE_LORA = 256
DECAY_SCALE = math.exp(-0.5)
GN_EPS = 64e-5
CONV_W = D_MODEL
CONV_K = 3
MEM_TOKENS = 256
MEM_HEADS = 4
MEM_W = D_MODEL
MEM_HEAD = MEM_W // MEM_HEADS
N_EXPERTS = 32
TOP_K = 4
D_FF = D_MODEL
SWIGLU_LIMIT = 7.0
SWIGLU_ALPHA = 1.702
MOE_BLOCK = 128
LN_EPS = 1e-5
DEEPNORM_ALPHA = (2.0 * DEPTH) ** 0.25
DEEPNORM_BETA = (8.0 * DEPTH) ** -0.25
RWKV_COLS = 3 * RW_W + DECAY_LORA + AICL_LORA + GATE_LORA
CONV_COLS = 3 * CONV_W
MEMQ_COLS = MEM_W
GATE_COLS = 3 * D_MODEL
RWKV_END = RWKV_COLS
CONV_END = RWKV_END + CONV_COLS
MEMQ_END = CONV_END + MEMQ_COLS
IN_COLS = MEMQ_END + GATE_COLS

kernel_name = 'rwkv7_shortconv_memxattn_gated_moe_decoder_step'


def layer_norm(x, g, b):
    xf = x.astype(jnp.float32)
    mu = xf.mean(-1, keepdims=True)
    var = jnp.square(xf - mu).mean(-1, keepdims=True)
    return ((xf - mu) * lax.rsqrt(var + LN_EPS) * g + b).astype(x.dtype)


def rwkv7_time_mix(z, s0, w0, w_w_up, a0, w_a_up, w_g_up, k_k, k_a, r_k, gn_g, gn_b):
    b, t, _ = z.shape
    f32 = jnp.float32
    r, k, v, wd, ad, gd = jnp.split(z, [RW_W, 2 * RW_W, 3 * RW_W, 3 * RW_W + DECAY_LORA,
                                        3 * RW_W + DECAY_LORA + AICL_LORA], axis=-1)
    decay = jnp.exp(-DECAY_SCALE * jax.nn.sigmoid((w0 + jnp.tanh(wd) @ w_w_up).astype(f32)))
    a = jax.nn.sigmoid(a0 + ad @ w_a_up)
    g = jax.nn.sigmoid(gd) @ w_g_up
    kk = k * k_k
    k = k * (1.0 + (a - 1.0) * k_a)
    heads = lambda u: u.astype(f32).reshape(b, t, RW_HEADS, RW_HEAD)
    r, k, v, decay, a, kk = (heads(u) for u in (r, k, v, decay, a, kk))
    kk = kk / jnp.maximum(jnp.sqrt(jnp.sum(kk * kk, axis=-1, keepdims=True)), 1e-12)

    def step(S, inp):
        r_t, k_t, v_t, w_t, kk_t, a_t = inp
        sa = jnp.einsum('bhvk,bhk->bhv', S, -kk_t)
        S = (S * w_t[:, :, None, :] + sa[..., None] * (kk_t * a_t)[:, :, None, :]
             + v_t[..., None] * k_t[:, :, None, :])
        return S, jnp.einsum('bhvk,bhk->bhv', S, r_t)

    xs = tuple(jnp.swapaxes(u, 0, 1) for u in (r, k, v, decay, kk, a))
    s_final, ys = lax.scan(step, s0.astype(f32), xs)
    y = jnp.swapaxes(ys, 0, 1)
    mu = y.mean(-1, keepdims=True)
    var = jnp.square(y - mu).mean(-1, keepdims=True)
    y = ((y - mu) * lax.rsqrt(var + GN_EPS)).reshape(b, t, RW_W) * gn_g + gn_b
    bonus = jnp.sum(r * k * r_k, axis=-1, keepdims=True) * v
    y = (y + bonus.reshape(b, t, RW_W)) * g
    return y.astype(z.dtype), s_final.astype(s0.dtype)


def short_conv(p_conv, conv_prev, conv_w):
    b_gate, c_gate, xin = jnp.split(p_conv, 3, axis=-1)
    u = c_gate * xin
    t = u.shape[1]
    u_ext = jnp.concatenate([conv_prev.astype(u.dtype), u], axis=1)
    y = conv_w[0] * u_ext[:, 0:t]
    for j in range(1, CONV_K):
        y = y + conv_w[j] * u_ext[:, j:j + t]
    return b_gate * y, u_ext[:, t:]


def memory_attention(q, mem_k, mem_v):
    b, t, _ = q.shape
    q = q.reshape(b, t, MEM_HEADS, MEM_HEAD)
    s = jnp.einsum('bthd,bmhd->bhtm', q, mem_k).astype(jnp.float32) * (MEM_HEAD ** -0.5)
    p = jax.nn.softmax(s, axis=-1).astype(mem_v.dtype)
    return jnp.einsum('bhtm,bmhd->bthd', p, mem_v).reshape(b, t, MEM_W)


def mixer_block(x, x_prev, rw_state, conv_prev, mem_k, mem_v, mix_p):
    (w_in, w_w_up, w0, w_a_up, a0, w_g_up, tshift_mu, k_k, k_a, r_k, gn_g, gn_b,
     conv_w, w_proj_a, w_proj_b, w_proj_m, w_o) = mix_p
    b, t, d = x.shape
    p = x @ w_in
    p_rw, p_conv, q_mem, gate_logits = jnp.split(p, [RWKV_END, CONV_END, MEMQ_END], axis=-1)
    p_rw_prev = (x_prev @ w_in[:, :RWKV_COLS])[:, None, :]
    shifted = jnp.concatenate([p_rw_prev, p_rw[:, :-1]], axis=1)
    z = p_rw + (shifted - p_rw) * tshift_mu
    y_a, rw_new = rwkv7_time_mix(z, rw_state, w0, w_w_up, a0, w_a_up, w_g_up, k_k, k_a, r_k, gn_g, gn_b)
    y_b, conv_new = short_conv(p_conv, conv_prev, conv_w)
    y_m = memory_attention(q_mem, mem_k, mem_v)
    gates = jax.nn.sigmoid(gate_logits).reshape(b, t, 3, d)
    merged = (gates[:, :, 0] * (y_a @ w_proj_a) + gates[:, :, 1] * (y_b @ w_proj_b)
              + gates[:, :, 2] * (y_m @ w_proj_m))
    return merged @ w_o, x[:, -1], rw_new, conv_new


def routed_moe(x, router_w, router_b, w_gate_up, b_gate_up, w_down, b_down):
    n_tok, d = x.shape
    n_assign = n_tok * TOP_K
    n_blocks = -(-n_assign // MOE_BLOCK) + N_EXPERTS
    logits = (x @ router_w + router_b).astype(jnp.float32)
    top_logit, top_idx = lax.top_k(logits, TOP_K)
    gate = jax.nn.softmax(top_logit, axis=-1)
    flat_e = top_idx.reshape(-1)
    flat_tok = jnp.arange(n_assign, dtype=jnp.int32) // TOP_K
    order = jnp.argsort(flat_e)
    e_sorted = flat_e[order]
    counts = jnp.bincount(flat_e, length=N_EXPERTS)
    padded = (counts + MOE_BLOCK - 1) // MOE_BLOCK * MOE_BLOCK
    start = jnp.cumsum(counts) - counts
    padded_end = jnp.cumsum(padded)
    padded_start = padded_end - padded
    dest = padded_start[e_sorted] + jnp.arange(n_assign, dtype=jnp.int32) - start[e_sorted]
    slot_tok = jnp.full((n_blocks * MOE_BLOCK,), n_tok, jnp.int32).at[dest].set(flat_tok[order])
    slot_gate = jnp.zeros((n_blocks * MOE_BLOCK,), jnp.float32).at[dest].set(gate.reshape(-1)[order])
    block_expert = jnp.minimum(jnp.searchsorted(padded_end, jnp.arange(n_blocks, dtype=jnp.int32) * MOE_BLOCK,
                                                side='right'), N_EXPERTS - 1)
    x_pad = jnp.concatenate([x, jnp.zeros((1, d), x.dtype)], axis=0)

    def expert_block(args):
        tok, e, g = args
        h = x_pad[tok] @ w_gate_up[e] + b_gate_up[e]
        h_gate, h_up = jnp.split(h, 2, axis=-1)
        h_gate = jnp.minimum(h_gate, SWIGLU_LIMIT)
        h_up = jnp.clip(h_up, -SWIGLU_LIMIT, SWIGLU_LIMIT)
        act = h_gate * jax.nn.sigmoid(SWIGLU_ALPHA * h_gate) * (h_up + 1.0)
        return (act @ w_down[e] + b_down[e]) * g[:, None].astype(x.dtype)

    y_blocks = lax.map(expert_block, (slot_tok.reshape(n_blocks, MOE_BLOCK), block_expert,
                                      slot_gate.reshape(n_blocks, MOE_BLOCK)))
    out = jnp.zeros((n_tok + 1, d), x.dtype).at[slot_tok].add(y_blocks.reshape(-1, d))
    return out[:n_tok]


def decoder_layer(x, x_prev, rw_state, conv_prev, mem_k, mem_v, mix_p, ffn_p):
    ln1_g, ln1_b, router_w, router_b, w_gate_up, b_gate_up, w_down, b_down, ln2_g, ln2_b = ffn_p
    h, x_last, rw_new, conv_new = mixer_block(x, x_prev, rw_state, conv_prev, mem_k, mem_v, mix_p)
    x = layer_norm(DEEPNORM_ALPHA * x + h, ln1_g, ln1_b)
    b, t, d = x.shape
    f = routed_moe(x.reshape(b * t, d), router_w, router_b, w_gate_up, b_gate_up, w_down, b_down).reshape(b, t, d)
    x = layer_norm(DEEPNORM_ALPHA * x + f, ln2_g, ln2_b)
    return x, x_last, rw_new, conv_new


def setup_inputs(seed: int = 0) -> dict:
    key = jax.random.key(seed)
    k = jax.random.split(key, 40)
    f32 = jnp.float32
    L = DEPTH

    def nrm(i, shape, scale):
        return jax.random.normal(k[i], shape, f32) * scale

    inp = {}
    inp['x_prompt'] = nrm(0, (BATCH, SEQ, D_MODEL), 1.0)
    inp['x_sample'] = nrm(1, (DEC_BATCH, DEC_SEQ, D_MODEL), 1.0)
    inp['mem_prompt'] = nrm(2, (BATCH, MEM_TOKENS, D_MODEL), 1.0)
    inp['cache_mem_k'] = nrm(3, (L, DEC_BATCH, MEM_TOKENS, MEM_HEADS, MEM_HEAD), 1.0)
    inp['cache_mem_v'] = nrm(4, (L, DEC_BATCH, MEM_TOKENS, MEM_HEADS, MEM_HEAD), 1.0)
    inp['state_rwkv'] = nrm(5, (L, DEC_BATCH, RW_HEADS, RW_HEAD, RW_HEAD), 0.3)
    inp['state_shift'] = nrm(6, (L, DEC_BATCH, D_MODEL), 1.0)
    inp['state_conv'] = nrm(7, (L, DEC_BATCH, CONV_K - 1, CONV_W), 1.0)
    inp['w_in'] = nrm(8, (L, D_MODEL, IN_COLS), D_MODEL ** -0.5)
    inp['w_w_up'] = nrm(9, (L, DECAY_LORA, RW_W), 0.1)
    inp['w0'] = -1.0 + nrm(10, (L, RW_W), 1.0)
    inp['w_a_up'] = nrm(11, (L, AICL_LORA, RW_W), 0.1)
    inp['a0'] = nrm(12, (L, RW_W), 0.3)
    inp['w_g_up'] = nrm(13, (L, GATE_LORA, RW_W), GATE_LORA ** -0.5)
    inp['tshift_mu'] = jax.random.uniform(k[14], (L, RWKV_COLS), f32)
    inp['k_k'] = 0.85 + nrm(15, (L, RW_W), 0.05)
    inp['k_a'] = 1.0 + nrm(16, (L, RW_W), 0.05)
    inp['r_k'] = nrm(17, (L, RW_HEADS, RW_HEAD), 0.1)
    inp['gn_g'] = 1.0 + nrm(18, (L, RW_W), 0.05)
    inp['gn_b'] = nrm(19, (L, RW_W), 0.02)
    inp['conv_w'] = nrm(20, (L, CONV_K, CONV_W), CONV_K ** -0.5)
    inp['w_mem_k'] = nrm(21, (L, D_MODEL, MEM_W), D_MODEL ** -0.5)
    inp['w_mem_v'] = nrm(22, (L, D_MODEL, MEM_W), D_MODEL ** -0.5)
    inp['w_proj_a'] = nrm(23, (L, RW_W, D_MODEL), RW_W ** -0.5)
    inp['w_proj_b'] = nrm(24, (L, CONV_W, D_MODEL), CONV_W ** -0.5)
    inp['w_proj_m'] = nrm(25, (L, MEM_W, D_MODEL), MEM_W ** -0.5)
    inp['w_o'] = nrm(26, (L, D_MODEL, D_MODEL), DEEPNORM_BETA * D_MODEL ** -0.5)
    inp['ln1_g'] = 1.0 + nrm(27, (L, D_MODEL), 0.05)
    inp['ln1_b'] = nrm(28, (L, D_MODEL), 0.02)
    inp['router_w'] = nrm(29, (L, D_MODEL, N_EXPERTS), D_MODEL ** -0.5)
    inp['router_b'] = nrm(30, (L, N_EXPERTS), 0.01)
    inp['w_gate_up'] = nrm(31, (L, N_EXPERTS, D_MODEL, 2 * D_FF), D_MODEL ** -0.5)
    inp['b_gate_up'] = nrm(32, (L, N_EXPERTS, 2 * D_FF), 0.01)
    inp['w_down'] = nrm(33, (L, N_EXPERTS, D_FF, D_MODEL), DEEPNORM_BETA * D_FF ** -0.5)
    inp['b_down'] = nrm(34, (L, N_EXPERTS, D_MODEL), 0.01)
    inp['ln2_g'] = 1.0 + nrm(35, (L, D_MODEL), 0.05)
    inp['ln2_b'] = nrm(36, (L, D_MODEL), 0.02)
    return inp


def reference(x_prompt, x_sample, mem_prompt, cache_mem_k, cache_mem_v, state_rwkv, state_shift, state_conv,
              w_in, w_w_up, w0, w_a_up, a0, w_g_up, tshift_mu, k_k, k_a, r_k, gn_g, gn_b, conv_w,
              w_mem_k, w_mem_v, w_proj_a, w_proj_b, w_proj_m, w_o, ln1_g, ln1_b,
              router_w, router_b, w_gate_up, b_gate_up, w_down, b_down, ln2_g, ln2_b):
    bp = x_prompt.shape[0]
    n_mem = mem_prompt.shape[1]
    dt = x_prompt.dtype
    y_prompt, y_sample = x_prompt, x_sample
    mk_p, mv_p, rw_p, sh_p, cv_p, rw_s, sh_s, cv_s = [], [], [], [], [], [], [], []
    for l in range(DEPTH):
        mix_p = (w_in[l], w_w_up[l], w0[l], w_a_up[l], a0[l], w_g_up[l], tshift_mu[l], k_k[l], k_a[l],
                 r_k[l], gn_g[l], gn_b[l], conv_w[l], w_proj_a[l], w_proj_b[l], w_proj_m[l], w_o[l])
        ffn_p = (ln1_g[l], ln1_b[l], router_w[l], router_b[l], w_gate_up[l], b_gate_up[l],
                 w_down[l], b_down[l], ln2_g[l], ln2_b[l])
        mem_k = (mem_prompt @ w_mem_k[l]).reshape(bp, n_mem, MEM_HEADS, MEM_HEAD)
        mem_v = (mem_prompt @ w_mem_v[l]).reshape(bp, n_mem, MEM_HEADS, MEM_HEAD)
        y_prompt, sh, rw, cv = decoder_layer(
            y_prompt, jnp.zeros((bp, D_MODEL), dt), jnp.zeros((bp, RW_HEADS, RW_HEAD, RW_HEAD), dt),
            jnp.zeros((bp, CONV_K - 1, CONV_W), dt), mem_k, mem_v, mix_p, ffn_p)
        mk_p.append(mem_k); mv_p.append(mem_v); rw_p.append(rw); sh_p.append(sh); cv_p.append(cv)
        y_sample, sh2, rw2, cv2 = decoder_layer(
            y_sample, state_shift[l], state_rwkv[l], state_conv[l], cache_mem_k[l], cache_mem_v[l], mix_p, ffn_p)
        rw_s.append(rw2); sh_s.append(sh2); cv_s.append(cv2)
    return (y_prompt, y_sample, jnp.stack(mk_p), jnp.stack(mv_p), jnp.stack(rw_p), jnp.stack(sh_p),
            jnp.stack(cv_p), jnp.stack(rw_s), jnp.stack(sh_s), jnp.stack(cv_s))
```

```python
import functools
import math

import jax
import jax.numpy as jnp
from jax import lax
from jax.experimental import pallas as pl
from jax.experimental.pallas import tpu as pltpu

F32 = jnp.float32
BF16 = jnp.bfloat16
HIGHEST = lax.Precision.HIGHEST

LANES = 128
RW_HEAD = 64
HEADS_PER_BLOCK = LANES // RW_HEAD
DECAY_LORA = 96
AICL_LORA = 96
GATE_LORA = 256
LORA_PAD = 512
DECAY_SCALE = math.exp(-0.5)
GN_EPS = 64e-5
LN_EPS = 1e-5
N_EXPERTS = 32
TOP_K = 4
SWIGLU_LIMIT = 7.0
SWIGLU_ALPHA = 1.702
VMEM_LIMIT = 56 * 1024 * 1024


def _dot(a, b, prec=None):
    return jnp.dot(a, b, preferred_element_type=F32, precision=prec)


def _dot_nt(a, b, prec=None):
    return lax.dot_general(a, b, (((1,), (1,)), ((), ())), preferred_element_type=F32, precision=prec)


def _dot_tn(a, b, prec=None):
    return lax.dot_general(a, b, (((0,), (0,)), ((), ())), preferred_element_type=F32, precision=prec)


def _rwkv_kernel(pr_ref, pk_ref, pv_ref, pl_ref, qr_ref, qk_ref, qv_ref, ql_ref,
                 mur_ref, muk_ref, muv_ref, mul_ref,
                 w0_ref, a0_ref, kk_ref, ka_ref, rk_ref, gng_ref, gnb_ref,
                 ww_ref, wa_ref, wg_ref, s0_ref,
                 y_ref, sout_ref,
                 s_sc, cr_sc, ck_sc, cv_sc, cl_sc, *, chunk, n_chunks, prec):
    c = pl.program_id(2)
    C = chunk
    R = HEADS_PER_BLOCK * C

    @pl.when(c == 0)
    def _init():
        cr_sc[0:1, :] = qr_ref[0]
        ck_sc[0:1, :] = qk_ref[0]
        cv_sc[0:1, :] = qv_ref[0]
        cl_sc[0:1, :] = ql_ref[0]
        x = s0_ref[0].reshape(HEADS_PER_BLOCK * RW_HEAD, RW_HEAD)
        dup = (lax.broadcasted_iota(jnp.int32, (RW_HEAD, LANES), 0)
               == lax.broadcasted_iota(jnp.int32, (RW_HEAD, LANES), 1) % RW_HEAD).astype(F32)
        xx = _dot(x, dup, HIGHEST)
        rb = lax.broadcasted_iota(jnp.int32, (LANES, LANES), 0) // RW_HEAD
        cb = lax.broadcasted_iota(jnp.int32, (LANES, LANES), 1) // RW_HEAD
        s_sc[...] = jnp.where(rb == cb, xx, 0.0)

    row = lax.broadcasted_iota(jnp.int32, (C, LANES), 0)

    def shifted_lerp(p_ref, carry_sc, mu_ref, width):
        p = p_ref[...]
        r0 = lax.broadcasted_iota(jnp.int32, (C, width), 0)
        prev = jnp.where(r0 == 0, carry_sc[0:1, :], pltpu.roll(p, 1, axis=0))
        z = p + (prev - p) * mu_ref[...]
        carry_sc[0:1, :] = p[C - 1:C, :]
        return z

    zr = shifted_lerp(pr_ref, cr_sc, mur_ref, LANES)
    zk = shifted_lerp(pk_ref, ck_sc, muk_ref, LANES)
    zv = shifted_lerp(pv_ref, cv_sc, muv_ref, LANES)
    zl = shifted_lerp(pl_ref, cl_sc, mul_ref, LORA_PAD)

    lb_r = lax.broadcasted_iota(jnp.int32, (LANES, LANES), 0) // RW_HEAD
    lb_c = lax.broadcasted_iota(jnp.int32, (LANES, LANES), 1) // RW_HEAD
    head_ones = (lb_r == lb_c).astype(F32)

    def head_sum(x):
        return _dot(x, head_ones, HIGHEST)

    lw = -DECAY_SCALE * jax.nn.sigmoid(w0_ref[...] + _dot(jnp.tanh(zl).astype(BF16), ww_ref[...]))
    a = jax.nn.sigmoid(a0_ref[...] + _dot(zl.astype(BF16), wa_ref[...]))
    g = _dot(jax.nn.sigmoid(zl).astype(BF16), wg_ref[...])
    kk = zk * kk_ref[...]
    kk = kk / jnp.maximum(jnp.sqrt(head_sum(kk * kk)), 1e-12)
    kmod = zk * (1.0 + (a - 1.0) * ka_ref[...])
    beta = kk * a

    ti = lax.broadcasted_iota(jnp.int32, (C, C), 0)
    tj = lax.broadcasted_iota(jnp.int32, (C, C), 1)
    cw = _dot((tj <= ti).astype(F32), lw, HIGHEST)
    cw_last = cw[C - 1:C, :]
    e_neg = jnp.exp(-cw)
    e_tail = jnp.exp(cw_last - cw)

    head0 = lax.broadcasted_iota(jnp.int32, (C, LANES), 1) < RW_HEAD

    def stack(x):
        return jnp.concatenate([jnp.where(head0, x, 0.0), jnp.where(head0, 0.0, x)], axis=0)

    kk_s = stack(kk * jnp.exp(cw - lw))
    b_s = stack(beta * e_neg)
    k_s = stack(kmod * e_neg)
    r_s = stack(zr * jnp.exp(cw))
    v_s = stack(zv)
    bw_s = stack(beta * e_tail)
    kw_s = stack(kmod * e_tail)

    si = lax.broadcasted_iota(jnp.int32, (R, R), 0)
    sj = lax.broadcasted_iota(jnp.int32, (R, R), 1)
    same = (si // C) == (sj // C)
    strict = same & (si > sj)
    incl = same & (si >= sj)
    a_ab = jnp.where(strict, _dot_nt(kk_s, b_s, prec), 0.0)
    a_ak = jnp.where(strict, _dot_nt(kk_s, k_s, prec), 0.0)
    a_rb = jnp.where(incl, _dot_nt(r_s, b_s, prec), 0.0)
    a_rk = jnp.where(incl, _dot_nt(r_s, k_s, prec), 0.0)

    x = -a_ab
    minv = (si == sj).astype(F32) + x
    for _ in range(int(math.log2(C)) - 1):
        x = _dot(x, x, prec)
        minv = minv + _dot(minv, x, prec)

    s_prev = s_sc[...]
    p_s = -_dot(minv, kk_s, prec)
    q_s = -_dot(minv, _dot(a_ak, v_s, prec), prec)
    u_s = _dot_nt(p_s, s_prev, prec) + q_s
    y_s = _dot_nt(r_s, s_prev, prec) + _dot(a_rb, u_s, prec) + _dot(a_rk, v_s, prec)
    s_new = s_prev * jnp.exp(cw_last) + _dot_tn(u_s, bw_s, prec) + _dot_tn(v_s, kw_s, prec)
    s_sc[...] = s_new

    y = y_s[0:C, :] + y_s[C:R, :]
    mean = head_sum(y) * (1.0 / RW_HEAD)
    d = y - mean
    var = head_sum(d * d) * (1.0 / RW_HEAD)
    yn = d * lax.rsqrt(var + GN_EPS) * gng_ref[...] + gnb_ref[...]
    bonus = head_sum(zr * kmod * rk_ref[...]) * zv
    y_ref[...] = ((yn + bonus) * g).astype(y_ref.dtype)

    @pl.when(c == n_chunks - 1)
    def _fin():
        fold = (lax.broadcasted_iota(jnp.int32, (LANES, RW_HEAD), 0) % RW_HEAD
                == lax.broadcasted_iota(jnp.int32, (LANES, RW_HEAD), 1)).astype(F32)
        out = _dot(s_new, fold, HIGHEST)
        sout_ref[0] = out.reshape(HEADS_PER_BLOCK, RW_HEAD, RW_HEAD)


def rwkv_time_mix(p_rw, prev_rw, s0, mu, w0, a0, k_k, k_a, r_k, gn_g, gn_b, ww, wa, wg,
                  *, batch, seq, chunk, row_block0, prec):
    width = w0.shape[-1]
    n_blk = width // LANES
    n_chunks = seq // chunk
    lora_blk = 3 * width // LORA_PAD
    heads = width // RW_HEAD

    def tok(col0):
        return pl.BlockSpec((chunk, LANES), lambda b, h, c: (row_block0 + b * n_chunks + c, col0 + h))

    def prev(col0):
        return pl.BlockSpec((1, 1, LANES), lambda b, h, c: (b, 0, col0 + h))

    def vec(col0):
        return pl.BlockSpec((1, LANES), lambda b, h, c: (0, col0 + h))

    lora_w = pl.BlockSpec((LORA_PAD, LANES), lambda b, h, c: (0, h))
    state = pl.BlockSpec((1, HEADS_PER_BLOCK, RW_HEAD, RW_HEAD), lambda b, h, c: (b, h, 0, 0))
    in_specs = [
        tok(0), tok(n_blk), tok(2 * n_blk),
        pl.BlockSpec((chunk, LORA_PAD), lambda b, h, c: (row_block0 + b * n_chunks + c, lora_blk)),
        prev(0), prev(n_blk), prev(2 * n_blk),
        pl.BlockSpec((1, 1, LORA_PAD), lambda b, h, c: (b, 0, lora_blk)),
        vec(0), vec(n_blk), vec(2 * n_blk),
        pl.BlockSpec((1, LORA_PAD), lambda b, h, c: (0, lora_blk)),
        vec(0), vec(0), vec(0), vec(0), vec(0), vec(0), vec(0),
        lora_w, lora_w, lora_w, state,
    ]
    out_specs = [
        pl.BlockSpec((chunk, LANES), lambda b, h, c: (b * n_chunks + c, h)),
        state,
    ]
    kern = functools.partial(_rwkv_kernel, chunk=chunk, n_chunks=n_chunks, prec=prec)
    return pl.pallas_call(
        kern,
        grid=(batch, n_blk, n_chunks),
        in_specs=in_specs,
        out_specs=out_specs,
        out_shape=[jax.ShapeDtypeStruct((batch * seq, width), F32),
                   jax.ShapeDtypeStruct((batch, heads, RW_HEAD, RW_HEAD), F32)],
        scratch_shapes=[pltpu.VMEM((LANES, LANES), F32), pltpu.VMEM((8, LANES), F32),
                        pltpu.VMEM((8, LANES), F32), pltpu.VMEM((8, LANES), F32),
                        pltpu.VMEM((8, LORA_PAD), F32)],
        compiler_params=pltpu.CompilerParams(
            dimension_semantics=("parallel", "parallel", "arbitrary"), vmem_limit_bytes=VMEM_LIMIT),
        name=f"rwkv_c{chunk}",
    )(p_rw, p_rw, p_rw, p_rw, prev_rw, prev_rw, prev_rw, prev_rw, mu, mu, mu, mu,
      w0, a0, k_k, k_a, r_k, gn_g, gn_b, ww, wa, wg, s0)


def _params(*sem):
    return pltpu.CompilerParams(dimension_semantics=sem, vmem_limit_bytes=VMEM_LIMIT)


def _mm_kernel(x_ref, w_ref, o_ref):
    o_ref[...] = _dot(x_ref[...].astype(BF16), w_ref[...]).astype(o_ref.dtype)


def matmul(x, w, out_dtype, tm, tn, name):
    m, kd = x.shape
    n = w.shape[1]
    return pl.pallas_call(
        _mm_kernel,
        grid=(n // tn, m // tm),
        in_specs=[pl.BlockSpec((tm, kd), lambda j, i: (i, 0)),
                  pl.BlockSpec((kd, tn), lambda j, i: (0, j))],
        out_specs=pl.BlockSpec((tm, tn), lambda j, i: (i, j)),
        out_shape=jax.ShapeDtypeStruct((m, n), out_dtype),
        compiler_params=_params("parallel", "parallel"),
        name=name,
    )(x, w)


def _conv_seq_kernel(bg_ref, cg_ref, xi_ref, prev_ref, cw_ref, y_ref, new_ref, carry_sc, *, rows, n_t):
    i = pl.program_id(2)

    @pl.when(i == 0)
    def _init():
        carry_sc[0:2, :] = prev_ref[0]

    u = cg_ref[...].astype(F32) * xi_ref[...].astype(F32)
    r = lax.broadcasted_iota(jnp.int32, u.shape, 0)
    c0 = carry_sc[0:1, :]
    c1 = carry_sc[1:2, :]
    u1 = jnp.where(r == 0, c1, pltpu.roll(u, 1, axis=0))
    u2 = jnp.where(r == 0, c0, jnp.where(r == 1, c1, pltpu.roll(u, 2, axis=0)))
    y = cw_ref[0:1, :] * u2 + cw_ref[1:2, :] * u1 + cw_ref[2:3, :] * u
    y_ref[...] = (bg_ref[...].astype(F32) * y).astype(y_ref.dtype)
    carry_sc[0:2, :] = u[rows - 2:rows, :]

    @pl.when(i == n_t - 1)
    def _fin():
        new_ref[0] = u[rows - 2:rows, :]


def conv_sequences(p2, conv_prev, conv_w, *, batch, seq, width, rows, tn):
    nb = width // tn
    n_t = seq // rows

    def tok(sec):
        return pl.BlockSpec((rows, tn), lambda b, j, i: (b * n_t + i, sec * nb + j))

    return pl.pallas_call(
        functools.partial(_conv_seq_kernel, rows=rows, n_t=n_t),
        grid=(batch, nb, n_t),
        in_specs=[tok(0), tok(1), tok(2),
                  pl.BlockSpec((1, 2, tn), lambda b, j, i: (b, 0, j)),
                  pl.BlockSpec((3, tn), lambda b, j, i: (0, j))],
        out_specs=[pl.BlockSpec((rows, tn), lambda b, j, i: (b * n_t + i, j)),
                   pl.BlockSpec((1, 2, tn), lambda b, j, i: (b, 0, j))],
        out_shape=[jax.ShapeDtypeStruct((batch * seq, width), BF16),
                   jax.ShapeDtypeStruct((batch, 2, width), F32)],
        scratch_shapes=[pltpu.VMEM((8, tn), F32)],
        compiler_params=_params("parallel", "parallel", "arbitrary"),
        name="conv_seq",
    )(p2, p2, p2, conv_prev, conv_w)


def _conv_step_kernel(bg_ref, cg_ref, xi_ref, p0_ref, p1_ref, cw_ref, y_ref, n0_ref, n1_ref, u1_sc, u2_sc, *, seq):
    t = pl.program_id(1)

    @pl.when(t == 0)
    def _init():
        u2_sc[...] = p0_ref[...]
        u1_sc[...] = p1_ref[...]

    u = cg_ref[...].astype(F32) * xi_ref[...].astype(F32)
    u1 = u1_sc[...]
    u2 = u2_sc[...]
    y = cw_ref[0:1, :] * u2 + cw_ref[1:2, :] * u1 + cw_ref[2:3, :] * u
    y_ref[...] = (bg_ref[...].astype(F32) * y).astype(y_ref.dtype)
    u2_sc[...] = u1
    u1_sc[...] = u

    @pl.when(t == seq - 1)
    def _fin():
        n0_ref[...] = u1
        n1_ref[...] = u


def conv_steps(p2_wide, conv_prev_wide, conv_w, *, row_block, batch, seq, p2_cols, width, tn):
    nb = width // tn
    cb = p2_cols // tn

    def tok(sec):
        return pl.BlockSpec((batch, tn), lambda j, t: (row_block, t * cb + sec * nb + j))

    return pl.pallas_call(
        functools.partial(_conv_step_kernel, seq=seq),
        grid=(nb, seq),
        in_specs=[tok(0), tok(1), tok(2),
                  pl.BlockSpec((batch, tn), lambda j, t: (0, j)),
                  pl.BlockSpec((batch, tn), lambda j, t: (0, nb + j)),
                  pl.BlockSpec((3, tn), lambda j, t: (0, j))],
        out_specs=[pl.BlockSpec((batch, tn), lambda j, t: (0, t * nb + j)),
                   pl.BlockSpec((batch, tn), lambda j, t: (0, j)),
                   pl.BlockSpec((batch, tn), lambda j, t: (0, j))],
        out_shape=[jax.ShapeDtypeStruct((batch, seq * width), BF16),
                   jax.ShapeDtypeStruct((batch, width), F32),
                   jax.ShapeDtypeStruct((batch, width), F32)],
        scratch_shapes=[pltpu.VMEM((batch, tn), F32), pltpu.VMEM((batch, tn), F32)],
        compiler_params=_params("parallel", "arbitrary"),
        name="conv_step",
    )(p2_wide, p2_wide, p2_wide, conv_prev_wide, conv_prev_wide, conv_w)


def _softmax_rows(s):
    m = jnp.max(s, axis=-1, keepdims=True)
    e = jnp.exp(s - m)
    return e / jnp.sum(e, axis=-1, keepdims=True)


def _attn_seq_kernel(q_ref, k_ref, v_ref, o_ref, *, scale):
    s = _dot_nt(q_ref[...], k_ref[...].astype(BF16)) * scale
    p = _softmax_rows(s)
    o_ref[...] = _dot(p.astype(BF16), v_ref[...].astype(BF16)).astype(o_ref.dtype)


def attention_sequences(p2, q_col0, mem_k, mem_v, *, batch, seq, n_mem, heads, head_dim, tq):
    n_t = seq // tq
    kv = pl.BlockSpec((n_mem, head_dim), lambda b, h, i: (b, h))
    return pl.pallas_call(
        functools.partial(_attn_seq_kernel, scale=head_dim ** -0.5),
        grid=(batch, heads, n_t),
        in_specs=[pl.BlockSpec((tq, head_dim), lambda b, h, i: (b * n_t + i, q_col0 + h)), kv, kv],
        out_specs=pl.BlockSpec((tq, head_dim), lambda b, h, i: (b * n_t + i, h)),
        out_shape=jax.ShapeDtypeStruct((batch * seq, heads * head_dim), BF16),
        compiler_params=_params("parallel", "parallel", "arbitrary"),
        name="attn_seq",
    )(p2, mem_k, mem_v)


def _attn_step_kernel(q_ref, k_ref, v_ref, o_ref, *, scale, bb, seq):
    q = q_ref[...].astype(F32)
    outs = []
    for b in range(bb):
        qb = q[b * seq:(b + 1) * seq, :].astype(BF16)
        s = _dot_nt(qb, k_ref[b].astype(BF16)) * scale
        p = _softmax_rows(s)
        outs.append(_dot(p.astype(BF16), v_ref[b].astype(BF16)))
    o_ref[...] = jnp.concatenate(outs, axis=0).astype(o_ref.dtype)


def attention_steps(p2, q_col0, row0, cache_k, cache_v, *, batch, seq, n_mem, heads, head_dim, bb):
    kv = pl.BlockSpec((bb, n_mem, head_dim), lambda g, h: (g, 0, h))
    return pl.pallas_call(
        functools.partial(_attn_step_kernel, scale=head_dim ** -0.5, bb=bb, seq=seq),
        grid=(batch // bb, heads),
        in_specs=[pl.BlockSpec((bb * seq, head_dim), lambda g, h: (row0 + g, q_col0 + h)), kv, kv],
        out_specs=pl.BlockSpec((bb * seq, head_dim), lambda g, h: (g, h)),
        out_shape=jax.ShapeDtypeStruct((batch * seq, heads * head_dim), BF16),
        compiler_params=_params("parallel", "parallel"),
        name="attn_step",
    )(p2, cache_k, cache_v)


def _merge_kernel(ya_ref, yb_ref, ym_ref, g0_ref, g1_ref, g2_ref, wa_ref, wb_ref, wm_ref, o_ref):
    acc = jax.nn.sigmoid(g0_ref[...].astype(F32)) * _dot(ya_ref[...].astype(BF16), wa_ref[...])
    acc += jax.nn.sigmoid(g1_ref[...].astype(F32)) * _dot(yb_ref[...].astype(BF16), wb_ref[...])
    acc += jax.nn.sigmoid(g2_ref[...].astype(F32)) * _dot(ym_ref[...].astype(BF16), wm_ref[...])
    o_ref[...] = acc.astype(o_ref.dtype)


def merge_branches(ya, yb, ym, p2, gate_col0, wa, wb, wm, *, tm, tn):
    m, d = ya.shape
    nb = d // tn
    act = pl.BlockSpec((tm, d), lambda j, i: (i, 0))
    wsp = pl.BlockSpec((d, tn), lambda j, i: (0, j))

    def gate(k):
        return pl.BlockSpec((tm, tn), lambda j, i: (i, gate_col0 + k * nb + j))

    return pl.pallas_call(
        _merge_kernel,
        grid=(nb, m // tm),
        in_specs=[act, act, act, gate(0), gate(1), gate(2), wsp, wsp, wsp],
        out_specs=pl.BlockSpec((tm, tn), lambda j, i: (i, j)),
        out_shape=jax.ShapeDtypeStruct((m, d), BF16),
        compiler_params=_params("parallel", "parallel"),
        name="merge",
    )(ya, yb, ym, p2, p2, p2, wa, wb, wm)


def _layer_norm(y, g, b):
    mu = jnp.mean(y, axis=-1, keepdims=True)
    d = y - mu
    var = jnp.mean(d * d, axis=-1, keepdims=True)
    return d * lax.rsqrt(var + LN_EPS) * g + b


def _proj_ln_kernel(m_ref, x_ref, wo_ref, g_ref, b_ref, rw_ref, rb_ref, x1_ref, x1b_ref, lg_ref, *, alpha):
    h = _dot(m_ref[...], wo_ref[...])
    x1 = _layer_norm(alpha * x_ref[...] + h, g_ref[...], b_ref[...])
    x1_ref[...] = x1
    x1b_ref[...] = x1.astype(BF16)
    lg_ref[...] = _dot(x1, rw_ref[...], HIGHEST) + rb_ref[...]


def project_norm_route(merged, x, wo, ln_g, ln_b, router_w, router_b, *, alpha, tm):
    m, d = x.shape
    n_r = router_w.shape[1]
    row = pl.BlockSpec((tm, d), lambda i: (i, 0))
    vec = pl.BlockSpec((1, d), lambda i: (0, 0))
    return pl.pallas_call(
        functools.partial(_proj_ln_kernel, alpha=alpha),
        grid=(m // tm,),
        in_specs=[row, row, pl.BlockSpec((d, d), lambda i: (0, 0)), vec, vec,
                  pl.BlockSpec((d, n_r), lambda i: (0, 0)), pl.BlockSpec((1, n_r), lambda i: (0, 0))],
        out_specs=[row, row, pl.BlockSpec((tm, n_r), lambda i: (i, 0))],
        out_shape=[jax.ShapeDtypeStruct((m, d), F32), jax.ShapeDtypeStruct((m, d), BF16),
                   jax.ShapeDtypeStruct((m, n_r), F32)],
        compiler_params=_params("parallel"),
        name="proj_ln_route",
    )(merged, x, wo, ln_g, ln_b, router_w, router_b)


MOE_GROUP = 1536
MOE_ROW_TILE = 512
MOE_F_TILE = 256


def _moe_kernel(ie_ref, ib_ref, nr_ref, x_ref, gate_ref, wg_ref, wu_ref, wd_ref, bg_ref, bu_ref, bd_ref,
                o_ref, acc_sc, *, n_f):
    i = pl.program_id(0)
    f = pl.program_id(1)
    rows = nr_ref[i]
    rt = MOE_ROW_TILE

    @pl.when(rows > 0)
    def _work():
        wg = wg_ref[0].astype(BF16)
        wu = wu_ref[0].astype(BF16)
        wd = wd_ref[0].astype(BF16)
        n_tiles = (rows + rt - 1) // rt

        def tile(r, carry):
            sl = pl.ds(pl.multiple_of(r * rt, rt), rt)
            x = x_ref[sl, :]
            hg = jnp.minimum(_dot(x, wg) + bg_ref[0], SWIGLU_LIMIT)
            hu = jnp.clip(_dot(x, wu) + bu_ref[0], -SWIGLU_LIMIT, SWIGLU_LIMIT)
            act = hg * jax.nn.sigmoid(SWIGLU_ALPHA * hg) * (hu + 1.0)
            part = _dot(act.astype(BF16), wd)

            @pl.when(f == 0)
            def _first():
                acc_sc[sl, :] = part

            @pl.when(f > 0)
            def _rest():
                acc_sc[sl, :] += part

            return carry

        lax.fori_loop(0, n_tiles, tile, 0)

        @pl.when(f == n_f - 1)
        def _fin():
            def emit(r, carry):
                sl = pl.ds(pl.multiple_of(r * rt, rt), rt)
                o_ref[sl, :] = ((acc_sc[sl, :] + bd_ref[0]) * gate_ref[sl, :]).astype(o_ref.dtype)
                return carry

            def blank(r, carry):
                sl = pl.ds(pl.multiple_of(r * rt, rt), rt)
                o_ref[sl, :] = jnp.zeros((rt, o_ref.shape[1]), o_ref.dtype)
                return carry

            lax.fori_loop(0, n_tiles, emit, 0)
            lax.fori_loop(n_tiles, MOE_GROUP // rt, blank, 0)


def moe_experts(x_sorted, slot_gate, item_expert, item_block, item_rows, w_gate_up, b_gate_up, w_down, b_down):
    n_items = item_expert.shape[0]
    d = x_sorted.shape[1]
    d_ff = w_down.shape[1]
    tf = MOE_F_TILE
    n_f = d_ff // tf
    n_exp = w_down.shape[0]

    def f_eff(i, f, nr):
        return jnp.where(nr[i] > 0, f, n_f - 1)

    single = pl.Buffered(1)
    grid_spec = pltpu.PrefetchScalarGridSpec(
        num_scalar_prefetch=3,
        grid=(n_items, n_f),
        in_specs=[
            pl.BlockSpec((MOE_GROUP, d), lambda i, f, ie, ib, nr: (ib[i], 0), pipeline_mode=single),
            pl.BlockSpec((MOE_GROUP, 1), lambda i, f, ie, ib, nr: (ib[i], 0), pipeline_mode=single),
            pl.BlockSpec((1, d, tf), lambda i, f, ie, ib, nr: (ie[i], 0, f_eff(i, f, nr))),
            pl.BlockSpec((1, d, tf), lambda i, f, ie, ib, nr: (ie[i], 0, n_f + f_eff(i, f, nr))),
            pl.BlockSpec((1, tf, d), lambda i, f, ie, ib, nr: (ie[i], f_eff(i, f, nr), 0)),
            pl.BlockSpec((1, 1, tf), lambda i, f, ie, ib, nr: (ie[i], 0, f_eff(i, f, nr))),
            pl.BlockSpec((1, 1, tf), lambda i, f, ie, ib, nr: (ie[i], 0, n_f + f_eff(i, f, nr))),
            pl.BlockSpec((1, 1, d), lambda i, f, ie, ib, nr: (ie[i], 0, 0)),
        ],
        out_specs=pl.BlockSpec((MOE_GROUP, d), lambda i, f, ie, ib, nr: (ib[i], 0), pipeline_mode=single),
        scratch_shapes=[pltpu.VMEM((MOE_GROUP, d), F32)],
    )
    return pl.pallas_call(
        functools.partial(_moe_kernel, n_f=n_f),
        grid_spec=grid_spec,
        out_shape=jax.ShapeDtypeStruct(x_sorted.shape, BF16),
        compiler_params=_params("arbitrary", "arbitrary"),
        name="moe_experts",
    )(item_expert, item_block, item_rows, x_sorted, slot_gate, w_gate_up, w_gate_up, w_down,
      b_gate_up.reshape(n_exp, 1, 2 * d_ff), b_gate_up.reshape(n_exp, 1, 2 * d_ff), b_down.reshape(n_exp, 1, d))


def _res_ln_kernel(x_ref, f_ref, g_ref, b_ref, o_ref, *, alpha):
    o_ref[...] = _layer_norm(alpha * x_ref[...] + f_ref[...].astype(F32), g_ref[...], b_ref[...])


def residual_norm(x, f, ln_g, ln_b, *, alpha, tm):
    m, d = x.shape
    row = pl.BlockSpec((tm, d), lambda i: (i, 0))
    vec = pl.BlockSpec((1, d), lambda i: (0, 0))
    return pl.pallas_call(
        functools.partial(_res_ln_kernel, alpha=alpha),
        grid=(m // tm,),
        in_specs=[row, row, vec, vec],
        out_specs=row,
        out_shape=jax.ShapeDtypeStruct((m, d), F32),
        compiler_params=_params("parallel"),
        name="res_ln",
    )(x, f, ln_g, ln_b)


def _route(logits, n_items):
    n_tok = logits.shape[0]
    n_assign = n_tok * TOP_K
    top_logit, top_idx = lax.top_k(logits, TOP_K)
    gate = jax.nn.softmax(top_logit, axis=-1)
    flat_e = top_idx.reshape(-1)
    order = jnp.argsort(flat_e)
    e_sorted = flat_e[order]
    counts = jnp.bincount(flat_e, length=N_EXPERTS)
    padded = (counts + MOE_GROUP - 1) // MOE_GROUP * MOE_GROUP
    start = jnp.cumsum(counts) - counts
    padded_end = jnp.cumsum(padded)
    padded_start = padded_end - padded
    dest = (padded_start[e_sorted] + jnp.arange(n_assign, dtype=jnp.int32) - start[e_sorted]).astype(jnp.int32)
    n_rows = n_items * MOE_GROUP
    flat_tok = jnp.arange(n_assign, dtype=jnp.int32) // TOP_K
    slot_tok = jnp.zeros((n_rows,), jnp.int32).at[dest].set(flat_tok[order])
    slot_gate = jnp.zeros((n_rows,), F32).at[dest].set(gate.reshape(-1)[order])
    pos = jnp.zeros((n_assign,), jnp.int32).at[order].set(dest).reshape(n_tok, TOP_K)
    item_row0 = jnp.arange(n_items, dtype=jnp.int32) * MOE_GROUP
    n_real = (padded_end[-1] // MOE_GROUP).astype(jnp.int32)
    item = jnp.minimum(jnp.arange(n_items, dtype=jnp.int32), jnp.maximum(n_real - 1, 0))
    item_expert = jnp.minimum(jnp.searchsorted(padded_end, item * MOE_GROUP, side='right'),
                              N_EXPERTS - 1).astype(jnp.int32)
    filled = jnp.clip(padded_start[item_expert] + counts[item_expert] - item_row0, 0, MOE_GROUP)
    item_rows = jnp.where(jnp.arange(n_items) < n_real, filled, 0).astype(jnp.int32)
    return slot_tok, slot_gate, pos, item_expert, item.astype(jnp.int32), item_rows


def _pad_rw_cols(a, rw_cols, pad):
    lead = a.shape[:-1]
    return jnp.concatenate([a[..., :rw_cols], jnp.zeros(lead + (pad,), a.dtype), a[..., rw_cols:]], axis=-1)


def _layer(xp, xs, mem_prompt, cache_k, cache_v, st_rwkv, st_shift, st_conv, wts, depth):
    (w_in, w_w_up, w0, w_a_up, a0, w_g_up, tshift_mu, k_k, k_a, r_k, gn_g, gn_b, conv_w,
     w_mem_k, w_mem_v, w_proj_a, w_proj_b, w_proj_m, w_o, ln1_g, ln1_b,
     router_w, router_b, w_gate_up, b_gate_up, w_down, b_down, ln2_g, ln2_b) = wts
    bp, sp, d = xp.shape
    bs, ss, _ = xs.shape
    n_mem = mem_prompt.shape[1]
    mem_heads, mem_head = cache_k.shape[-2:]
    n_p, n_s = bp * sp, bs * ss
    n_tok = n_p + n_s
    alpha = (2.0 * depth) ** 0.25
    rw_cols = 3 * d + DECAY_LORA + AICL_LORA + GATE_LORA
    lora_cols = rw_cols - 3 * d
    pad = LORA_PAD - lora_cols
    rw_pad = rw_cols + pad

    x_all = jnp.concatenate([xp.reshape(n_p, d), xs.reshape(n_s, d)], axis=0)
    x_bf = x_all.astype(BF16)
    w_cat = _pad_rw_cols(w_in, rw_cols, pad).astype(BF16)
    mu_pad = _pad_rw_cols(tshift_mu, rw_cols, pad)[None, :rw_pad]
    w_rw = w_cat[:, :rw_pad]
    w_rest = w_cat[:, rw_pad:]
    rest_cols = w_rest.shape[1]

    p_rw = matmul(x_bf, w_rw, F32, 1024, 512, "in_proj_rw")
    p2 = matmul(x_bf, w_rest, BF16, 1024, 512, "in_proj_rest")
    prev_s = matmul(st_shift.astype(BF16), w_rw, F32, bs, 512, "prev_proj").reshape(bs, 1, rw_pad)
    prev_p = jnp.zeros((bp, 1, rw_pad), F32)

    def lora_rows(w, row0):
        return jnp.zeros((LORA_PAD, d), F32).at[row0:row0 + w.shape[0]].set(w).astype(BF16)

    ww = lora_rows(w_w_up, 0)
    wa = lora_rows(w_a_up, DECAY_LORA)
    wg = lora_rows(w_g_up, DECAY_LORA + AICL_LORA)
    r2 = lambda v: v.reshape(1, d)
    rw_vecs = (mu_pad, r2(w0), r2(a0), r2(k_k), r2(k_a), r2(r_k), r2(gn_g), r2(gn_b), ww, wa, wg)
    heads = d // RW_HEAD
    ya_p, rw_p = rwkv_time_mix(p_rw, prev_p, jnp.zeros((bp, heads, RW_HEAD, RW_HEAD), F32), *rw_vecs,
                               batch=bp, seq=sp, chunk=64, row_block0=0, prec=HIGHEST)
    ya_s, rw_s = rwkv_time_mix(p_rw, prev_s, st_rwkv, *rw_vecs,
                               batch=bs, seq=ss, chunk=ss, row_block0=n_p // ss, prec=HIGHEST)
    ya = jnp.concatenate([ya_p, ya_s], axis=0)

    yb_p, cv_p = conv_sequences(p2, jnp.zeros((bp, 2, d), F32), conv_w, batch=bp, seq=sp, width=d, rows=256, tn=512)
    yb_s, cv_s0, cv_s1 = conv_steps(p2.reshape(n_tok // ss, ss * rest_cols), st_conv.reshape(bs, 2 * d), conv_w,
                                    row_block=n_p // ss // bs, batch=bs, seq=ss, p2_cols=rest_cols, width=d, tn=512)
    yb = jnp.concatenate([yb_p, yb_s.reshape(n_s, d)], axis=0)
    cv_s = jnp.stack([cv_s0, cv_s1], axis=1)

    mem_in = mem_prompt.reshape(bp * n_mem, d).astype(BF16)
    mk = matmul(mem_in, w_mem_k.astype(BF16), F32, bp * n_mem, 512, "mem_k")
    mv = matmul(mem_in, w_mem_v.astype(BF16), F32, bp * n_mem, 512, "mem_v")
    q_col0 = 3 * d // mem_head
    ym_p = attention_sequences(p2, q_col0, mk, mv, batch=bp, seq=sp, n_mem=n_mem, heads=mem_heads,
                               head_dim=mem_head, tq=512)
    bb = 8
    ym_s = attention_steps(p2, q_col0, n_p // (bb * ss), cache_k.reshape(bs, n_mem, d), cache_v.reshape(bs, n_mem, d),
                           batch=bs, seq=ss, n_mem=n_mem, heads=mem_heads, head_dim=mem_head, bb=bb)
    ym = jnp.concatenate([ym_p, ym_s], axis=0)

    merged = merge_branches(ya, yb, ym, p2, 4 * d // 512, w_proj_a.astype(BF16), w_proj_b.astype(BF16),
                            w_proj_m.astype(BF16), tm=512, tn=512)
    rw_pad_r = jnp.zeros((d, LANES), F32).at[:, :N_EXPERTS].set(router_w)
    rb_pad_r = jnp.zeros((1, LANES), F32).at[0, :N_EXPERTS].set(router_b)
    x1, x1_bf, logits = project_norm_route(merged, x_all, w_o.astype(BF16), r2(ln1_g), r2(ln1_b),
                                           rw_pad_r, rb_pad_r, alpha=alpha, tm=256)

    n_items = n_tok * TOP_K // MOE_GROUP + N_EXPERTS
    slot_tok, slot_gate, pos, item_expert, item_block, item_rows = _route(logits[:, :N_EXPERTS], n_items)
    x_sorted = jnp.take(x1_bf, slot_tok, axis=0)
    y_sorted = moe_experts(x_sorted, slot_gate[:, None], item_expert, item_block, item_rows,
                           w_gate_up, b_gate_up, w_down, b_down)
    f = jnp.take(y_sorted, pos.reshape(-1), axis=0).astype(F32).reshape(n_tok, TOP_K, d).sum(axis=1)
    y = residual_norm(x1, f, r2(ln2_g), r2(ln2_b), alpha=alpha, tm=512)

    y_p = y[:n_p].reshape(bp, sp, d)
    y_s = y[n_p:].reshape(bs, ss, d)
    mk5 = mk.reshape(bp, n_mem, mem_heads, mem_head)
    mv5 = mv.reshape(bp, n_mem, mem_heads, mem_head)
    return y_p, y_s, mk5, mv5, rw_p, xp[:, -1], cv_p, rw_s, xs[:, -1], cv_s


def kernel(x_prompt, x_sample, mem_prompt, cache_mem_k, cache_mem_v, state_rwkv, state_shift, state_conv, w_in, w_w_up, w0, w_a_up, a0, w_g_up, tshift_mu, k_k, k_a, r_k, gn_g, gn_b, conv_w, w_mem_k, w_mem_v, w_proj_a, w_proj_b, w_proj_m, w_o, ln1_g, ln1_b, router_w, router_b, w_gate_up, b_gate_up, w_down, b_down, ln2_g, ln2_b):
    weights = (w_in, w_w_up, w0, w_a_up, a0, w_g_up, tshift_mu, k_k, k_a, r_k, gn_g, gn_b, conv_w,
               w_mem_k, w_mem_v, w_proj_a, w_proj_b, w_proj_m, w_o, ln1_g, ln1_b,
               router_w, router_b, w_gate_up, b_gate_up, w_down, b_down, ln2_g, ln2_b)
    depth = w_in.shape[0]
    yp, ys = x_prompt, x_sample
    outs = [[] for _ in range(8)]
    for l in range(depth):
        res = _layer(yp, ys, mem_prompt, cache_mem_k[l], cache_mem_v[l], state_rwkv[l], state_shift[l],
                     state_conv[l], tuple(w[l] for w in weights), depth)
        yp, ys = res[0], res[1]
        for acc, r in zip(outs, res[2:]):
            acc.append(r)
    return (yp, ys) + tuple(jnp.stack(o) for o in outs)
```

```python
import functools
import math

import jax
import jax.numpy as jnp
from jax import lax
from jax.experimental import pallas as pl
from jax.experimental.pallas import tpu as pltpu

F32 = jnp.float32
BF16 = jnp.bfloat16

LANES = 128
ROW_SLABS = 16
RW_HEAD = 64
HEADS_PER_BLOCK = LANES // RW_HEAD
RW_ROWS = 64
DECAY_LORA = 96
AICL_LORA = 96
GATE_LORA = 256
LORA_PAD = 512
DECAY_SCALE = math.exp(-0.5)
GN_EPS = 64e-5
LN_EPS = 1e-5
N_EXPERTS = 32
TOP_K = 4
SWIGLU_LIMIT = 7.0
SWIGLU_ALPHA = 1.702
VMEM_LIMIT = 56 * 1024 * 1024


def _dot(a, b):
    return jnp.dot(a, b, preferred_element_type=F32)


def _dot_nt(a, b):
    return lax.dot_general(a, b, (((1,), (1,)), ((), ())), preferred_element_type=F32)


def _dot_tn(a, b):
    return lax.dot_general(a, b, (((0,), (0,)), ((), ())), preferred_element_type=F32)


def _b16(x):
    return x.astype(BF16)


def _params(*sem):
    return pltpu.CompilerParams(dimension_semantics=sem, vmem_limit_bytes=VMEM_LIMIT)


def _split3(x):
    hi = _b16(x)
    r1 = x - hi.astype(F32)
    mid = _b16(r1)
    lo = _b16(r1 - mid.astype(F32))
    return hi, mid, lo


def _rwkv_kernel(pr_ref, pk_ref, pv_ref, pl_ref, qr_ref, qk_ref, qv_ref, ql_ref,
                 mur_ref, muk_ref, muv_ref, mul_ref,
                 w0_ref, a0_ref, kk_ref, ka_ref, rk_ref, gng_ref, gnb_ref,
                 ww_ref, wa_ref, wg_ref, s0_ref,
                 y_ref, sout_ref, *scratch, n_seq, n_chunks, pairs, has_state):
    C = RW_ROWS
    L = C // n_seq
    R = HEADS_PER_BLOCK * C
    W = pairs * LANES
    c = pl.program_id(2)
    carried = n_chunks > 1
    if carried:
        s_sc, cr_sc, ck_sc, cv_sc, cl_sc = scratch

        @pl.when(c == 0)
        def _init_state():
            for p in range(pairs):
                if has_state:
                    s_sc[p] = s0_ref[0, 2 * p:2 * p + 2].reshape(LANES, RW_HEAD)
                else:
                    s_sc[p] = jnp.zeros((LANES, RW_HEAD), F32)

    def prev_rows(q_ref, carry_sc, width):
        if carried:
            @pl.when(c == 0)
            def _():
                carry_sc[0:1, :] = q_ref[0]
            return jnp.broadcast_to(carry_sc[0:1, :], (C, width))
        q = q_ref[...]
        return jnp.broadcast_to(q, (n_seq, L, width)).reshape(C, width)

    def shifted_lerp(p_ref, q_ref, carry_sc, mu_ref, width):
        p = p_ref[...]
        pos = lax.broadcasted_iota(jnp.int32, (C, width), 0) % L
        prev = jnp.where(pos == 0, prev_rows(q_ref, carry_sc, width), pltpu.roll(p, 1, axis=0))
        if carried:
            carry_sc[0:1, :] = p[C - 1:C, :]
        return p + (prev - p) * mu_ref[...]

    zr = shifted_lerp(pr_ref, qr_ref, cr_sc if carried else None, mur_ref, W)
    zk = shifted_lerp(pk_ref, qk_ref, ck_sc if carried else None, muk_ref, W)
    zv = shifted_lerp(pv_ref, qv_ref, cv_sc if carried else None, muv_ref, W)
    zl = shifted_lerp(pl_ref, ql_ref, cl_sc if carried else None, mul_ref, LORA_PAD)

    lw = -DECAY_SCALE * jax.nn.sigmoid(w0_ref[...] + _dot(_b16(jnp.tanh(zl)), ww_ref[...]))
    a_all = jax.nn.sigmoid(a0_ref[...] + _dot(_b16(zl), wa_ref[...]))
    g_all = _dot(_b16(jax.nn.sigmoid(zl)), wg_ref[...])

    ti = lax.broadcasted_iota(jnp.int32, (2 * C, C), 0)
    tj = lax.broadcasted_iota(jnp.int32, (2 * C, C), 1)
    same_seq = (ti % C) // L == tj // L
    cum_lhs = _b16((same_seq & ((ti >= C) | (tj <= ti))).astype(F32))
    hi, mid, lo = _split3(lw)
    cum = _dot(cum_lhs, jnp.concatenate([hi, mid, lo], axis=1))
    cum = cum[:, 0:W] + cum[:, W:2 * W] + cum[:, 2 * W:3 * W]
    cw_all = cum[0:C]
    tot_all = cum[C:2 * C]

    lb_r = lax.broadcasted_iota(jnp.int32, (2 * LANES, LANES), 0) % LANES // RW_HEAD
    lb_c = lax.broadcasted_iota(jnp.int32, (2 * LANES, LANES), 1) // RW_HEAD
    head_ones2 = _b16((lb_r == lb_c).astype(F32))

    def head_sum(x):
        xh = _b16(x)
        xl = _b16(x - xh.astype(F32))
        return _dot(jnp.concatenate([xh, xl], axis=1), head_ones2)

    head0 = lax.broadcasted_iota(jnp.int32, (n_seq, L, LANES), 2) < RW_HEAD

    def stack(x):
        x3 = x.reshape(n_seq, L, LANES)
        return jnp.concatenate([jnp.where(head0, x3, 0.0), jnp.where(head0, 0.0, x3)], axis=1).reshape(R, LANES)

    def unstack(x):
        x3 = x.reshape(n_seq, 2 * L, LANES)
        return (x3[:, 0:L, :] + x3[:, L:2 * L, :]).reshape(C, LANES)

    si = lax.broadcasted_iota(jnp.int32, (R, R), 0)
    sj = lax.broadcasted_iota(jnp.int32, (R, R), 1)
    same = (si // L) == (sj // L)
    strict = same & (si > sj)
    incl = same & (si >= sj)
    eye = (si == sj).astype(F32)
    row_head = (lax.broadcasted_iota(jnp.int32, (R, LANES), 0) // L) % HEADS_PER_BLOCK
    lane_head = lax.broadcasted_iota(jnp.int32, (R, LANES), 1) // RW_HEAD
    head_match = row_head == lane_head
    state_row_head0 = lax.broadcasted_iota(jnp.int32, (LANES, LANES), 0) < RW_HEAD
    n_lvl = int(math.log2(L)) - 1
    S2 = 2 * L

    for p in range(pairs):
        ls = slice(p * LANES, (p + 1) * LANES)
        r_, k_, v_, a, g = zr[:, ls], zk[:, ls], zv[:, ls], a_all[:, ls], g_all[:, ls]
        cw, tot, lwp = cw_all[:, ls], tot_all[:, ls], lw[:, ls]
        kk = k_ * kk_ref[:, ls]
        kk = kk / jnp.maximum(jnp.sqrt(head_sum(kk * kk)), 1e-12)
        kmod = k_ * (1.0 + (a - 1.0) * ka_ref[:, ls])
        beta = kk * a
        e_neg = jnp.exp(-cw)
        e_tail = jnp.exp(tot - cw)

        kk_s = stack(kk * jnp.exp(cw - lwp))
        b_s = stack(beta * e_neg)
        k_s = stack(kmod * e_neg)
        r_s = stack(r_ * jnp.exp(cw))
        v_s = stack(v_)
        bw_s = stack(beta * e_tail)
        kw_s = stack(kmod * e_tail)

        amat = _dot_nt(_b16(jnp.concatenate([kk_s, r_s], axis=0)), _b16(jnp.concatenate([b_s, k_s], axis=0)))
        a_ab = jnp.where(strict, amat[0:R, 0:R], 0.0)
        a_ak = jnp.where(strict, amat[0:R, R:2 * R], 0.0)
        a_rb = jnp.where(incl, amat[R:2 * R, 0:R], 0.0)
        a_rk = jnp.where(incl, amat[R:2 * R, R:2 * R], 0.0)

        x = -a_ab
        minv = eye + x
        xb = _b16(x)
        cur = _dot(xb, xb)
        for lvl in range(n_lvl):
            cb = _b16(cur)
            if lvl < n_lvl - 1:
                both = _dot(_b16(jnp.concatenate([minv, cur], axis=0)), cb)
                minv = minv + both[0:R]
                cur = both[R:2 * R]
            else:
                minv = minv + _dot(_b16(minv), cb)

        v_b = _b16(v_s)
        akv = _dot(_b16(a_ak), v_b)
        pq = _dot(_b16(minv), _b16(jnp.concatenate([kk_s, akv], axis=1)))
        p_s = -pq[:, 0:LANES]
        q_s = -pq[:, LANES:2 * LANES]

        pr_cat = jnp.concatenate([p_s, r_s], axis=0)
        pr_fold = _b16(pr_cat + pltpu.roll(pr_cat, RW_HEAD, axis=1))[:, 0:RW_HEAD]
        u_parts, rs_parts, states = [], [], []
        for b in range(n_seq):
            if carried:
                s_prev = s_sc[p]
            else:
                s_prev = s0_ref[b, 2 * p:2 * p + 2].reshape(LANES, RW_HEAD)
            states.append(s_prev)
            rows = slice(b * S2, (b + 1) * S2)
            rows2 = slice(R + b * S2, R + (b + 1) * S2)
            lhs = jnp.concatenate([pr_fold[rows], pr_fold[rows2]], axis=0)
            out = _dot_nt(lhs, _b16(s_prev))
            hm = head_match[rows]
            u_parts.append(jnp.where(hm, out[0:S2], 0.0) + q_s[rows])
            rs_parts.append(jnp.where(hm, out[S2:2 * S2], 0.0))
        u_s = jnp.concatenate(u_parts, axis=0) if n_seq > 1 else u_parts[0]
        rs_s = jnp.concatenate(rs_parts, axis=0) if n_seq > 1 else rs_parts[0]
        u_b = _b16(u_s)

        y_s = rs_s + _dot(_b16(jnp.concatenate([a_rb, a_rk], axis=1)), jnp.concatenate([u_b, v_b], axis=0))
        bw_b, kw_b = _b16(bw_s), _b16(kw_s)
        for b in range(n_seq):
            rows = slice(b * S2, (b + 1) * S2)
            z = _dot_tn(jnp.concatenate([u_b[rows], v_b[rows]], axis=0),
                        jnp.concatenate([bw_b[rows], kw_b[rows]], axis=0))
            z = (z + pltpu.roll(z, RW_HEAD, axis=1))[:, 0:RW_HEAD]
            wt = jnp.broadcast_to(jnp.exp(tot[b * L:b * L + 1, :]), (LANES, LANES))
            wnat = jnp.where(state_row_head0, wt, pltpu.roll(wt, RW_HEAD, axis=1))[:, 0:RW_HEAD]
            s_new = states[b] * wnat + z
            if carried:
                s_sc[p] = s_new
            sout_ref[b, 2 * p:2 * p + 2] = s_new.reshape(HEADS_PER_BLOCK, RW_HEAD, RW_HEAD)

        y = unstack(y_s)
        stats = head_sum(jnp.concatenate([y, r_ * kmod * rk_ref[:, ls]], axis=0))
        mean = stats[0:C] * (1.0 / RW_HEAD)
        d = y - mean
        var = head_sum(d * d) * (1.0 / RW_HEAD)
        yn = d * lax.rsqrt(var + GN_EPS) * gng_ref[:, ls] + gnb_ref[:, ls]
        y_ref[:, ls] = ((yn + stats[C:2 * C] * v_) * g).astype(y_ref.dtype)


def rwkv_time_mix(p_rw, prev_rw, s0, mu, w0, a0, k_k, k_a, r_k, gn_g, gn_b, ww, wa, wg,
                  *, batch, seq, row_block0, pairs, has_state):
    width = w0.shape[-1]
    wp = pairs * LANES
    n_blk = width // wp
    heads = width // RW_HEAD
    hp = HEADS_PER_BLOCK * pairs
    lora_blk = 3 * width // LORA_PAD
    if seq >= RW_ROWS:
        n_seq, n_chunks, n_outer = 1, seq // RW_ROWS, batch
    else:
        n_seq, n_chunks, n_outer = RW_ROWS // seq, 1, batch * seq // RW_ROWS

    def tok(col0):
        return pl.BlockSpec((RW_ROWS, wp), lambda b, h, c: (row_block0 + b * n_chunks + c, col0 + h))

    def prev(col0):
        return pl.BlockSpec((n_seq, 1, wp), lambda b, h, c: (b, 0, col0 + h))

    def vec(col0):
        return pl.BlockSpec((1, wp), lambda b, h, c: (0, col0 + h))

    lora_w = pl.BlockSpec((LORA_PAD, wp), lambda b, h, c: (0, h))
    state = pl.BlockSpec((n_seq, hp, RW_HEAD, RW_HEAD), lambda b, h, c: (b, h, 0, 0))
    in_specs = [
        tok(0), tok(n_blk), tok(2 * n_blk),
        pl.BlockSpec((RW_ROWS, LORA_PAD), lambda b, h, c: (row_block0 + b * n_chunks + c, lora_blk)),
        prev(0), prev(n_blk), prev(2 * n_blk),
        pl.BlockSpec((n_seq, 1, LORA_PAD), lambda b, h, c: (b, 0, lora_blk)),
        vec(0), vec(n_blk), vec(2 * n_blk),
        pl.BlockSpec((1, LORA_PAD), lambda b, h, c: (0, lora_blk)),
        vec(0), vec(0), vec(0), vec(0), vec(0), vec(0), vec(0),
        lora_w, lora_w, lora_w, state,
    ]
    out_specs = [pl.BlockSpec((RW_ROWS, wp), lambda b, h, c: (b * n_chunks + c, h)), state]
    scratch = []
    if n_chunks > 1:
        scratch = [pltpu.VMEM((pairs, LANES, RW_HEAD), F32), pltpu.VMEM((8, wp), F32), pltpu.VMEM((8, wp), F32),
                   pltpu.VMEM((8, wp), F32), pltpu.VMEM((8, LORA_PAD), F32)]
    kern = functools.partial(_rwkv_kernel, n_seq=n_seq, n_chunks=n_chunks, pairs=pairs, has_state=has_state)
    return pl.pallas_call(
        kern,
        grid=(n_outer, n_blk, n_chunks),
        in_specs=in_specs,
        out_specs=out_specs,
        out_shape=[jax.ShapeDtypeStruct((batch * seq, width), F32),
                   jax.ShapeDtypeStruct((batch, heads, RW_HEAD, RW_HEAD), F32)],
        scratch_shapes=scratch,
        compiler_params=_params("parallel", "parallel", "arbitrary"),
        name=f"rwkv_l{min(seq, RW_ROWS)}",
    )(p_rw, p_rw, p_rw, p_rw, prev_rw, prev_rw, prev_rw, prev_rw, mu, mu, mu, mu,
      w0, a0, k_k, k_a, r_k, gn_g, gn_b, ww, wa, wg, s0)


def _mm_kernel(x_ref, w_ref, o_ref):
    o_ref[...] = _dot(_b16(x_ref[...]), w_ref[...]).astype(o_ref.dtype)


def matmul(x, w, out_dtype, tm, tn, name):
    m, kd = x.shape
    n = w.shape[1]
    return pl.pallas_call(
        _mm_kernel,
        grid=(n // tn, m // tm),
        in_specs=[pl.BlockSpec((tm, kd), lambda j, i: (i, 0)),
                  pl.BlockSpec((kd, tn), lambda j, i: (0, j))],
        out_specs=pl.BlockSpec((tm, tn), lambda j, i: (i, j)),
        out_shape=jax.ShapeDtypeStruct((m, n), out_dtype),
        compiler_params=_params("parallel", "parallel"),
        name=name,
    )(x, w)


def _conv_taps(u, u1, u2, bg_ref, cw_ref, y_ref):
    y = cw_ref[0:1, :] * u2 + cw_ref[1:2, :] * u1 + cw_ref[2:3, :] * u
    y_ref[...] = (bg_ref[...].astype(F32) * y).astype(y_ref.dtype)


def _conv_seq_kernel(bg_ref, cg_ref, xi_ref, cw_ref, y_ref, new_ref, carry_sc, *, rows, n_t):
    i = pl.program_id(2)

    @pl.when(i == 0)
    def _init():
        carry_sc[...] = jnp.zeros(carry_sc.shape, F32)

    u = cg_ref[...].astype(F32) * xi_ref[...].astype(F32)
    r = lax.broadcasted_iota(jnp.int32, u.shape, 0)
    c0 = carry_sc[0:1, :]
    c1 = carry_sc[1:2, :]
    u1 = jnp.where(r == 0, c1, pltpu.roll(u, 1, axis=0))
    u2 = jnp.where(r == 0, c0, jnp.where(r == 1, c1, pltpu.roll(u, 2, axis=0)))
    _conv_taps(u, u1, u2, bg_ref, cw_ref, y_ref)
    carry_sc[0:2, :] = u[rows - 2:rows, :]

    @pl.when(i == n_t - 1)
    def _fin():
        new_ref[0] = u[rows - 2:rows, :]


def conv_sequences(p2, conv_w, *, batch, seq, width, rows, tn):
    nb = width // tn
    n_t = seq // rows

    def tok(sec):
        return pl.BlockSpec((rows, tn), lambda b, j, i: (b * n_t + i, sec * nb + j))

    return pl.pallas_call(
        functools.partial(_conv_seq_kernel, rows=rows, n_t=n_t),
        grid=(batch, nb, n_t),
        in_specs=[tok(0), tok(1), tok(2), pl.BlockSpec((3, tn), lambda b, j, i: (0, j))],
        out_specs=[pl.BlockSpec((rows, tn), lambda b, j, i: (b * n_t + i, j)),
                   pl.BlockSpec((1, 2, tn), lambda b, j, i: (b, 0, j))],
        out_shape=[jax.ShapeDtypeStruct((batch * seq, width), BF16),
                   jax.ShapeDtypeStruct((batch, 2, width), F32)],
        scratch_shapes=[pltpu.VMEM((8, tn), F32)],
        compiler_params=_params("parallel", "parallel", "arbitrary"),
        name="conv_seq",
    )(p2, p2, p2, conv_w)


def _conv_short_kernel(bg_ref, cg_ref, xi_ref, p0_ref, p1_ref, cw_ref, y_ref, n0_ref, n1_ref, *, seq):
    u = cg_ref[...].astype(F32) * xi_ref[...].astype(F32)
    rows, tn = u.shape
    n_seq = rows // seq

    def per_seq(p_ref):
        return jnp.broadcast_to(p_ref[...].reshape(n_seq, 1, tn), (n_seq, seq, tn)).reshape(rows, tn)

    h0, h1 = per_seq(p0_ref), per_seq(p1_ref)
    pos = lax.broadcasted_iota(jnp.int32, u.shape, 0) % seq
    u1 = jnp.where(pos == 0, h1, pltpu.roll(u, 1, axis=0))
    u2 = jnp.where(pos == 0, h0, jnp.where(pos == 1, h1, pltpu.roll(u, 2, axis=0)))
    _conv_taps(u, u1, u2, bg_ref, cw_ref, y_ref)
    u3 = u.reshape(n_seq, seq, tn)
    n0_ref[...] = u3[:, seq - 2, :]
    n1_ref[...] = u3[:, seq - 1, :]


def conv_short(p2, hist0, hist1, conv_w, *, row_block0, batch, seq, width, rows, tn):
    nb = width // tn
    n_seq = rows // seq
    n_r = batch * seq // rows

    def tok(sec):
        return pl.BlockSpec((rows, tn), lambda i, j: (row_block0 + i, sec * nb + j))

    per_seq = pl.BlockSpec((n_seq, tn), lambda i, j: (i, j))
    return pl.pallas_call(
        functools.partial(_conv_short_kernel, seq=seq),
        grid=(n_r, nb),
        in_specs=[tok(0), tok(1), tok(2), per_seq, per_seq, pl.BlockSpec((3, tn), lambda i, j: (0, j))],
        out_specs=[pl.BlockSpec((rows, tn), lambda i, j: (i, j)), per_seq, per_seq],
        out_shape=[jax.ShapeDtypeStruct((batch * seq, width), BF16),
                   jax.ShapeDtypeStruct((batch, width), F32), jax.ShapeDtypeStruct((batch, width), F32)],
        compiler_params=_params("parallel", "parallel"),
        name="conv_short",
    )(p2, p2, p2, hist0, hist1, conv_w)


def _softmax_rows(s):
    m = jnp.max(s, axis=-1, keepdims=True)
    e = jnp.exp(s - m)
    return e / jnp.sum(e, axis=-1, keepdims=True)


def _attn_seq_kernel(q_ref, k_ref, v_ref, o_ref, *, scale):
    s = _dot_nt(q_ref[...], _b16(k_ref[...])) * scale
    p = _softmax_rows(s)
    o_ref[...] = _dot(_b16(p), _b16(v_ref[...])).astype(o_ref.dtype)


def attention_sequences(p2, q_col0, mem_k, mem_v, *, batch, seq, n_mem, heads, head_dim, tq):
    n_t = seq // tq
    kv = pl.BlockSpec((n_mem, head_dim), lambda b, h, i: (b, h))
    return pl.pallas_call(
        functools.partial(_attn_seq_kernel, scale=head_dim ** -0.5),
        grid=(batch, heads, n_t),
        in_specs=[pl.BlockSpec((tq, head_dim), lambda b, h, i: (b * n_t + i, q_col0 + h)), kv, kv],
        out_specs=pl.BlockSpec((tq, head_dim), lambda b, h, i: (b * n_t + i, h)),
        out_shape=jax.ShapeDtypeStruct((batch * seq, heads * head_dim), BF16),
        compiler_params=_params("parallel", "parallel", "arbitrary"),
        name="attn_seq",
    )(p2, mem_k, mem_v)


def _attn_cache_kernel(q_ref, k_ref, v_ref, o_ref, *, scale, bb, seq, heads, head_dim):
    tiles = head_dim // LANES
    slabs = tiles * heads
    n_mem = k_ref.shape[1] // slabs
    q = q_ref[...].astype(F32)

    def head_slab(ref, b, h):
        return jnp.concatenate([ref[b, pl.ds(t * heads + h, n_mem, stride=slabs), :] for t in range(tiles)], axis=1)

    rows = []
    for b in range(bb):
        cols = []
        for h in range(heads):
            qb = _b16(q[b * seq:(b + 1) * seq, h * head_dim:(h + 1) * head_dim])
            s = _dot_nt(qb, _b16(head_slab(k_ref, b, h))) * scale
            p = _softmax_rows(s)
            cols.append(_dot(_b16(p), _b16(head_slab(v_ref, b, h))))
        rows.append(jnp.concatenate(cols, axis=1))
    o_ref[...] = jnp.concatenate(rows, axis=0).astype(o_ref.dtype)


def attention_cache(p2, q_blk0, row0, cache_k, cache_v, *, batch, seq, heads, head_dim, bb):
    d = heads * head_dim
    kv = pl.BlockSpec((bb,) + cache_k.shape[1:], lambda g: (g, 0, 0))
    return pl.pallas_call(
        functools.partial(_attn_cache_kernel, scale=head_dim ** -0.5, bb=bb, seq=seq, heads=heads, head_dim=head_dim),
        grid=(batch // bb,),
        in_specs=[pl.BlockSpec((bb * seq, d), lambda g: (row0 + g, q_blk0)), kv, kv],
        out_specs=pl.BlockSpec((bb * seq, d), lambda g: (g, 0)),
        out_shape=jax.ShapeDtypeStruct((batch * seq, d), BF16),
        compiler_params=_params("parallel"),
        name="attn_cache",
    )(p2, cache_k, cache_v)


def _merge_kernel(ya_ref, yb_ref, ym_ref, g0_ref, g1_ref, g2_ref, wa_ref, wb_ref, wm_ref, o_ref):
    acc = jax.nn.sigmoid(g0_ref[...].astype(F32)) * _dot(_b16(ya_ref[...]), wa_ref[...])
    acc += jax.nn.sigmoid(g1_ref[...].astype(F32)) * _dot(_b16(yb_ref[...]), wb_ref[...])
    acc += jax.nn.sigmoid(g2_ref[...].astype(F32)) * _dot(_b16(ym_ref[...]), wm_ref[...])
    o_ref[...] = acc.astype(o_ref.dtype)


def merge_branches(ya, yb, ym, p2, gate_col0, wa, wb, wm, *, tm, tn):
    m, d = ya.shape
    nb = d // tn
    act = pl.BlockSpec((tm, d), lambda j, i: (i, 0))
    wsp = pl.BlockSpec((d, tn), lambda j, i: (0, j))

    def gate(k):
        return pl.BlockSpec((tm, tn), lambda j, i: (i, gate_col0 + k * nb + j))

    return pl.pallas_call(
        _merge_kernel,
        grid=(nb, m // tm),
        in_specs=[act, act, act, gate(0), gate(1), gate(2), wsp, wsp, wsp],
        out_specs=pl.BlockSpec((tm, tn), lambda j, i: (i, j)),
        out_shape=jax.ShapeDtypeStruct((m, d), BF16),
        compiler_params=_params("parallel", "parallel"),
        name="merge",
    )(ya, yb, ym, p2, p2, p2, wa, wb, wm)


def _layer_norm(y, g, b):
    mu = jnp.mean(y, axis=-1, keepdims=True)
    d = y - mu
    var = jnp.mean(d * d, axis=-1, keepdims=True)
    return d * lax.rsqrt(var + LN_EPS) * g + b


def _to_slabs(ref, row0, x):
    rows = x.shape[0]
    for j in range(x.shape[1] // LANES):
        ref[pl.ds(row0 * ROW_SLABS + j, rows, stride=ROW_SLABS), :] = x[:, j * LANES:(j + 1) * LANES]


def _from_slabs(ref, row0, rows, offset=0, stride=ROW_SLABS):
    return jnp.concatenate(
        [ref[pl.ds(row0 * stride + offset + j, rows, stride=stride), :] for j in range(ROW_SLABS)], axis=1)


def _proj_ln_kernel(m_ref, x_ref, wo_ref, g_ref, b_ref, rw_ref, rb_ref, x1_ref, lg_ref, *, alpha):
    h = _dot(m_ref[...], wo_ref[...])
    x1 = _layer_norm(alpha * x_ref[...] + h, g_ref[...], b_ref[...])
    _to_slabs(x1_ref, 0, x1)
    hi, mid, lo = _split3(x1)
    w_hi, w_mid, w_lo = _split3(rw_ref[...])
    lg = _dot(hi, w_hi) + (_dot(hi, w_mid) + _dot(mid, w_hi)) + (_dot(hi, w_lo) + _dot(mid, w_mid) + _dot(lo, w_hi))
    lg_ref[...] = lg + rb_ref[...]


def project_norm_route(merged, x, wo, ln_g, ln_b, router_w, router_b, *, alpha, tm):
    m, d = x.shape
    n_r = router_w.shape[1]
    row = pl.BlockSpec((tm, d), lambda i: (i, 0))
    vec = pl.BlockSpec((1, d), lambda i: (0, 0))
    return pl.pallas_call(
        functools.partial(_proj_ln_kernel, alpha=alpha),
        grid=(m // tm,),
        in_specs=[row, row, pl.BlockSpec((d, d), lambda i: (0, 0)), vec, vec,
                  pl.BlockSpec((d, n_r), lambda i: (0, 0)), pl.BlockSpec((1, n_r), lambda i: (0, 0))],
        out_specs=[pl.BlockSpec((tm * ROW_SLABS, LANES), lambda i: (i, 0)),
                   pl.BlockSpec((tm, n_r), lambda i: (i, 0))],
        out_shape=[jax.ShapeDtypeStruct((m * ROW_SLABS, LANES), F32), jax.ShapeDtypeStruct((m, n_r), F32)],
        compiler_params=_params("parallel"),
        name="proj_ln_route",
    )(merged, x, wo, ln_g, ln_b, router_w, router_b)


MOE_GROUP = 1536
MOE_ROW_TILE = 256
MOE_F_TILE = 256
MOE_N_TILE = 256
COMBINE_TOKENS = 128


def _row_copy(src_hbm, src_row, dst_vmem, dst_row, sem):
    return pltpu.make_async_copy(
        src_hbm.at[pl.ds(pl.multiple_of(src_row * ROW_SLABS, ROW_SLABS), ROW_SLABS)],
        dst_vmem.at[pl.ds(pl.multiple_of(dst_row * ROW_SLABS, ROW_SLABS), ROW_SLABS)], sem)


def _gather_rows(idx_ref, n_rows, src_hbm, buf, sem):
    def issue(r, carry):
        _row_copy(src_hbm, idx_ref[0, 0, r], buf, r, sem).start()
        return carry

    lax.fori_loop(0, n_rows, issue, 0)
    pltpu.make_async_copy(src_hbm.at[pl.ds(0, n_rows * ROW_SLABS)], buf, sem).wait()


def _dispatch_kernel(tv_ref, tok_ref, x_hbm, o_ref, buf, sem):
    t = pl.program_id(0)

    @pl.when(tv_ref[t] > 0)
    def _():
        _gather_rows(tok_ref, MOE_ROW_TILE, x_hbm, buf, sem)
        o_ref[...] = _b16(_from_slabs(buf, 0, MOE_ROW_TILE))

    @pl.when(tv_ref[t] == 0)
    def _():
        o_ref[...] = jnp.zeros(o_ref.shape, o_ref.dtype)


def moe_dispatch(x_slabs, slot_tok, tile_valid, d):
    n_tiles = tile_valid.shape[0]
    grid_spec = pltpu.PrefetchScalarGridSpec(
        num_scalar_prefetch=1,
        grid=(n_tiles,),
        in_specs=[pl.BlockSpec((1, 1, MOE_ROW_TILE), lambda t, tv: (t, 0, 0), memory_space=pltpu.SMEM),
                  pl.BlockSpec(memory_space=pl.ANY)],
        out_specs=pl.BlockSpec((MOE_ROW_TILE, d), lambda t, tv: (t, 0)),
        scratch_shapes=[pltpu.VMEM((MOE_ROW_TILE * ROW_SLABS, LANES), F32), pltpu.SemaphoreType.DMA],
    )
    return pl.pallas_call(
        _dispatch_kernel,
        grid_spec=grid_spec,
        out_shape=jax.ShapeDtypeStruct((n_tiles * MOE_ROW_TILE, d), BF16),
        compiler_params=_params("arbitrary"),
        name="moe_dispatch",
    )(tile_valid, slot_tok.reshape(n_tiles, 1, MOE_ROW_TILE), x_slabs)


def _moe_kernel(ie_ref, ib_ref, nr_ref, x_ref, gate_ref, wg_ref, wu_ref, wd_ref, bg_ref, bu_ref, bd_ref,
                o_ref, h_sc, *, n_f):
    i = pl.program_id(0)
    s = pl.program_id(1)
    rows = nr_ref[i]
    rt = MOE_ROW_TILE
    n_tiles = (rows + rt - 1) // rt

    @pl.when((rows > 0) & (s < n_f))
    def _hidden():
        wg = _b16(wg_ref[0])
        wu = _b16(wu_ref[0])

        def tile(r, carry):
            sl = pl.ds(pl.multiple_of(r * rt, rt), rt)
            x = x_ref[sl, :]
            hg = jnp.minimum(_dot(x, wg) + bg_ref[0], SWIGLU_LIMIT)
            hu = jnp.clip(_dot(x, wu) + bu_ref[0], -SWIGLU_LIMIT, SWIGLU_LIMIT)
            h_sc[s, sl, :] = _b16(hg * jax.nn.sigmoid(SWIGLU_ALPHA * hg) * (hu + 1.0))
            return carry

        lax.fori_loop(0, n_tiles, tile, 0)

    @pl.when((rows > 0) & (s >= n_f))
    def _down():
        n = s - n_f
        wd = _b16(wd_ref[0])
        tf = h_sc.shape[2]

        def tile(r, carry):
            sl = pl.ds(pl.multiple_of(r * rt, rt), rt)
            acc = _dot(h_sc[0, sl, :], wd[0:tf, :])
            for f in range(1, n_f):
                acc += _dot(h_sc[f, sl, :], wd[f * tf:(f + 1) * tf, :])
            val = (acc + bd_ref[0]) * gate_ref[sl, :]
            for jj in range(MOE_N_TILE // LANES):
                start = r * (rt * ROW_SLABS) + n * (MOE_N_TILE // LANES) + jj
                o_ref[pl.ds(start, rt, stride=ROW_SLABS), :] = val[:, jj * LANES:(jj + 1) * LANES]
            return carry

        lax.fori_loop(0, n_tiles, tile, 0)

    @pl.when(s == 0)
    def _blank():
        def blank(r, carry):
            sl = pl.ds(pl.multiple_of(r * (rt * ROW_SLABS), rt * ROW_SLABS), rt * ROW_SLABS)
            o_ref[sl, :] = jnp.zeros((rt * ROW_SLABS, LANES), F32)
            return carry

        lax.fori_loop(n_tiles, MOE_GROUP // rt, blank, 0)


def moe_experts(x_sorted, slot_gate, item_expert, item_block, item_rows, w_gate_up, b_gate_up, w_down, b_down):
    n_items = item_expert.shape[0]
    d = x_sorted.shape[1]
    n_exp, d_ff, d_out = w_down.shape
    n_f = d_ff // MOE_F_TILE
    n_n = d_out // MOE_N_TILE

    def f_idx(i, s, nr):
        return jnp.minimum(jnp.where(nr[i] > 0, s, n_f - 1), n_f - 1)

    def n_idx(i, s, nr):
        return jnp.where(nr[i] > 0, jnp.maximum(s - n_f, 0), n_n - 1)

    single = pl.Buffered(1)
    grid_spec = pltpu.PrefetchScalarGridSpec(
        num_scalar_prefetch=3,
        grid=(n_items, n_f + n_n),
        in_specs=[
            pl.BlockSpec((MOE_GROUP, d), lambda i, s, ie, ib, nr: (ib[i], 0), pipeline_mode=single),
            pl.BlockSpec((MOE_GROUP, 1), lambda i, s, ie, ib, nr: (ib[i], 0), pipeline_mode=single),
            pl.BlockSpec((1, d, MOE_F_TILE), lambda i, s, ie, ib, nr: (ie[i], 0, f_idx(i, s, nr))),
            pl.BlockSpec((1, d, MOE_F_TILE), lambda i, s, ie, ib, nr: (ie[i], 0, n_f + f_idx(i, s, nr))),
            pl.BlockSpec((1, d_ff, MOE_N_TILE), lambda i, s, ie, ib, nr: (ie[i], 0, n_idx(i, s, nr))),
            pl.BlockSpec((1, 1, MOE_F_TILE), lambda i, s, ie, ib, nr: (ie[i], 0, f_idx(i, s, nr))),
            pl.BlockSpec((1, 1, MOE_F_TILE), lambda i, s, ie, ib, nr: (ie[i], 0, n_f + f_idx(i, s, nr))),
            pl.BlockSpec((1, 1, MOE_N_TILE), lambda i, s, ie, ib, nr: (ie[i], 0, n_idx(i, s, nr))),
        ],
        out_specs=pl.BlockSpec((MOE_GROUP * ROW_SLABS, LANES), lambda i, s, ie, ib, nr: (i, 0),
                               pipeline_mode=single),
        scratch_shapes=[pltpu.VMEM((n_f, MOE_GROUP, MOE_F_TILE), BF16)],
    )
    return pl.pallas_call(
        functools.partial(_moe_kernel, n_f=n_f),
        grid_spec=grid_spec,
        out_shape=jax.ShapeDtypeStruct((n_items * MOE_GROUP * ROW_SLABS, LANES), F32),
        compiler_params=_params("arbitrary", "arbitrary"),
        name="moe_experts",
    )(item_expert, item_block, item_rows, x_sorted, slot_gate, w_gate_up, w_gate_up, w_down,
      b_gate_up.reshape(n_exp, 1, 2 * d_ff), b_gate_up.reshape(n_exp, 1, 2 * d_ff), b_down.reshape(n_exp, 1, d_out))


def _combine_kernel(pos_ref, y_hbm, x1_ref, g_ref, b_ref, o_ref, buf, sem, *, alpha):
    tm = COMBINE_TOKENS
    _gather_rows(pos_ref, tm * TOP_K, y_hbm, buf, sem)
    f = _from_slabs(buf, 0, tm, 0, TOP_K * ROW_SLABS)
    for k in range(1, TOP_K):
        f += _from_slabs(buf, 0, tm, k * ROW_SLABS, TOP_K * ROW_SLABS)
    o_ref[...] = _layer_norm(alpha * _from_slabs(x1_ref, 0, tm) + f, g_ref[...], b_ref[...])


def moe_combine_norm(y_slabs, pos, x1_slabs, ln_g, ln_b, *, alpha):
    n_tok = pos.shape[0]
    tm = COMBINE_TOKENS
    d = ln_g.shape[1]
    n_t = n_tok // tm
    vec = pl.BlockSpec((1, d), lambda t: (0, 0))
    return pl.pallas_call(
        functools.partial(_combine_kernel, alpha=alpha),
        grid=(n_t,),
        in_specs=[pl.BlockSpec((1, 1, tm * TOP_K), lambda t: (t, 0, 0), memory_space=pltpu.SMEM),
                  pl.BlockSpec(memory_space=pl.ANY),
                  pl.BlockSpec((tm * ROW_SLABS, LANES), lambda t: (t, 0)), vec, vec],
        out_specs=pl.BlockSpec((tm, d), lambda t: (t, 0)),
        out_shape=jax.ShapeDtypeStruct((n_tok, d), F32),
        scratch_shapes=[pltpu.VMEM((tm * TOP_K * ROW_SLABS, LANES), F32), pltpu.SemaphoreType.DMA],
        compiler_params=_params("arbitrary"),
        name="moe_combine",
    )(pos.reshape(n_t, 1, tm * TOP_K), y_slabs, x1_slabs, ln_g, ln_b)


def _route(logits, n_items):
    n_tok = logits.shape[0]
    n_assign = n_tok * TOP_K
    tiles_per_item = MOE_GROUP // MOE_ROW_TILE
    top_logit, top_idx = lax.top_k(logits, TOP_K)
    gate = jax.nn.softmax(top_logit, axis=-1)
    flat_e = top_idx.reshape(-1)
    order = jnp.argsort(flat_e)
    e_sorted = flat_e[order]
    counts = jnp.bincount(flat_e, length=N_EXPERTS)
    padded = (counts + MOE_GROUP - 1) // MOE_GROUP * MOE_GROUP
    start = jnp.cumsum(counts) - counts
    padded_end = jnp.cumsum(padded)
    padded_start = padded_end - padded
    dest = (padded_start[e_sorted] + jnp.arange(n_assign, dtype=jnp.int32) - start[e_sorted]).astype(jnp.int32)
    n_rows = n_items * MOE_GROUP
    flat_tok = jnp.arange(n_assign, dtype=jnp.int32) // TOP_K
    slot_tok = jnp.zeros((n_rows,), jnp.int32).at[dest].set(flat_tok[order])
    slot_gate = jnp.zeros((n_rows,), F32).at[dest].set(gate.reshape(-1)[order])
    pos = jnp.zeros((n_assign,), jnp.int32).at[order].set(dest).reshape(n_tok, TOP_K)
    n_real = (padded_end[-1] // MOE_GROUP).astype(jnp.int32)
    item = jnp.minimum(jnp.arange(n_items, dtype=jnp.int32), jnp.maximum(n_real - 1, 0))
    item_expert = jnp.minimum(jnp.searchsorted(padded_end, item * MOE_GROUP, side='right'),
                              N_EXPERTS - 1).astype(jnp.int32)
    filled = jnp.clip(padded_start[item_expert] + counts[item_expert] - item * MOE_GROUP, 0, MOE_GROUP)
    item_rows = jnp.where(jnp.arange(n_items) < n_real, filled, 0).astype(jnp.int32)
    tile_row0 = (jnp.arange(tiles_per_item, dtype=jnp.int32) * MOE_ROW_TILE)[None, :]
    tile_valid = jnp.clip(item_rows[:, None] - tile_row0, 0, MOE_ROW_TILE).reshape(-1).astype(jnp.int32)
    return slot_tok, slot_gate, pos, item_expert, item.astype(jnp.int32), item_rows, tile_valid


def _pad_rw_cols(a, rw_cols, pad):
    lead = a.shape[:-1]
    return jnp.concatenate([a[..., :rw_cols], jnp.zeros(lead + (pad,), a.dtype), a[..., rw_cols:]], axis=-1)


def _cache_rows(c, heads, head_dim):
    b, n_mem = c.shape[:2]
    tiles = head_dim // LANES
    c = c.reshape(b, n_mem, heads, tiles, LANES).transpose(0, 1, 3, 2, 4)
    return c.reshape(b, n_mem * tiles * heads, LANES)


def _layer(xp, xs, mem_prompt, cache_k, cache_v, st_rwkv, st_shift, st_conv, wts, depth):
    (w_in, w_w_up, w0, w_a_up, a0, w_g_up, tshift_mu, k_k, k_a, r_k, gn_g, gn_b, conv_w,
     w_mem_k, w_mem_v, w_proj_a, w_proj_b, w_proj_m, w_o, ln1_g, ln1_b,
     router_w, router_b, w_gate_up, b_gate_up, w_down, b_down, ln2_g, ln2_b) = wts
    bp, sp, d = xp.shape
    bs, ss, _ = xs.shape
    n_mem = mem_prompt.shape[1]
    mem_heads, mem_head = cache_k.shape[-2:]
    n_p, n_s = bp * sp, bs * ss
    n_tok = n_p + n_s
    alpha = (2.0 * depth) ** 0.25
    rw_cols = 3 * d + DECAY_LORA + AICL_LORA + GATE_LORA
    pad = LORA_PAD - (rw_cols - 3 * d)
    rw_pad = rw_cols + pad

    x_all = jnp.concatenate([xp.reshape(n_p, d), xs.reshape(n_s, d)], axis=0)
    x_bf = _b16(x_all)
    w_cat = _b16(_pad_rw_cols(w_in, rw_cols, pad))
    mu_pad = _pad_rw_cols(tshift_mu, rw_cols, pad)[None, :rw_pad]
    w_rw = w_cat[:, :rw_pad]
    w_rest = w_cat[:, rw_pad:]

    p_rw = matmul(x_bf, w_rw, F32, 1024, 512, "in_proj_rw")
    p2 = matmul(x_bf, w_rest, BF16, 1024, 512, "in_proj_rest")
    prev_s = matmul(_b16(st_shift), w_rw, F32, bs, 512, "prev_proj").reshape(bs, 1, rw_pad)
    prev_p = jnp.zeros((bp, 1, rw_pad), F32)

    def lora_rows(w, row0):
        return _b16(jnp.zeros((LORA_PAD, d), F32).at[row0:row0 + w.shape[0]].set(w))

    ww = lora_rows(w_w_up, 0)
    wa = lora_rows(w_a_up, DECAY_LORA)
    wg = lora_rows(w_g_up, DECAY_LORA + AICL_LORA)
    r2 = lambda v: v.reshape(1, d)
    rw_vecs = (mu_pad, r2(w0), r2(a0), r2(k_k), r2(k_a), r2(r_k), r2(gn_g), r2(gn_b), ww, wa, wg)
    heads = d // RW_HEAD
    ya_p, rw_p = rwkv_time_mix(p_rw, prev_p, jnp.zeros((bp, heads, RW_HEAD, RW_HEAD), F32), *rw_vecs,
                               batch=bp, seq=sp, row_block0=0, pairs=4, has_state=False)
    ya_s, rw_s = rwkv_time_mix(p_rw, prev_s, st_rwkv, *rw_vecs,
                               batch=bs, seq=ss, row_block0=n_p // RW_ROWS, pairs=2, has_state=True)
    ya = jnp.concatenate([ya_p, ya_s], axis=0)

    yb_p, cv_p = conv_sequences(p2, conv_w, batch=bp, seq=sp, width=d, rows=256, tn=512)
    yb_s, cv_s0, cv_s1 = conv_short(p2, st_conv[:, 0, :], st_conv[:, 1, :], conv_w, row_block0=n_p // 128,
                                    batch=bs, seq=ss, width=d, rows=128, tn=512)
    yb = jnp.concatenate([yb_p, yb_s], axis=0)
    cv_s = jnp.stack([cv_s0, cv_s1], axis=1)

    mem_in = _b16(mem_prompt.reshape(bp * n_mem, d))
    mk = matmul(mem_in, _b16(w_mem_k), F32, bp * n_mem, 512, "mem_k")
    mv = matmul(mem_in, _b16(w_mem_v), F32, bp * n_mem, 512, "mem_v")
    q_col0 = 3 * d // mem_head
    ym_p = attention_sequences(p2, q_col0, mk, mv, batch=bp, seq=sp, n_mem=n_mem, heads=mem_heads,
                               head_dim=mem_head, tq=512)
    bb = 2
    ym_s = attention_cache(p2, 3 * d // d, n_p // (bb * ss), _cache_rows(cache_k, mem_heads, mem_head),
                           _cache_rows(cache_v, mem_heads, mem_head),
                           batch=bs, seq=ss, heads=mem_heads, head_dim=mem_head, bb=bb)
    ym = jnp.concatenate([ym_p, ym_s], axis=0)

    merged = merge_branches(ya, yb, ym, p2, 4 * d // 512, _b16(w_proj_a), _b16(w_proj_b),
                            _b16(w_proj_m), tm=512, tn=512)
    rw_pad_r = jnp.zeros((d, LANES), F32).at[:, :N_EXPERTS].set(router_w)
    rb_pad_r = jnp.zeros((1, LANES), F32).at[0, :N_EXPERTS].set(router_b)
    x1_slabs, logits = project_norm_route(merged, x_all, _b16(w_o), r2(ln1_g), r2(ln1_b),
                                          rw_pad_r, rb_pad_r, alpha=alpha, tm=256)

    n_items = n_tok * TOP_K // MOE_GROUP + N_EXPERTS
    slot_tok, slot_gate, pos, item_expert, item_block, item_rows, tile_valid = _route(logits[:, :N_EXPERTS], n_items)
    x_sorted = moe_dispatch(x1_slabs, slot_tok, tile_valid, d)
    y_slabs = moe_experts(x_sorted, slot_gate[:, None], item_expert, item_block, item_rows,
                          w_gate_up, b_gate_up, w_down, b_down)
    y = moe_combine_norm(y_slabs, pos, x1_slabs, r2(ln2_g), r2(ln2_b), alpha=alpha)

    y_p = y[:n_p].reshape(bp, sp, d)
    y_s = y[n_p:].reshape(bs, ss, d)
    mk5 = mk.reshape(bp, n_mem, mem_heads, mem_head)
    mv5 = mv.reshape(bp, n_mem, mem_heads, mem_head)
    return y_p, y_s, mk5, mv5, rw_p, xp[:, -1], cv_p, rw_s, xs[:, -1], cv_s


def kernel(x_prompt, x_sample, mem_prompt, cache_mem_k, cache_mem_v, state_rwkv, state_shift, state_conv, w_in, w_w_up, w0, w_a_up, a0, w_g_up, tshift_mu, k_k, k_a, r_k, gn_g, gn_b, conv_w, w_mem_k, w_mem_v, w_proj_a, w_proj_b, w_proj_m, w_o, ln1_g, ln1_b, router_w, router_b, w_gate_up, b_gate_up, w_down, b_down, ln2_g, ln2_b):
    weights = (w_in, w_w_up, w0, w_a_up, a0, w_g_up, tshift_mu, k_k, k_a, r_k, gn_g, gn_b, conv_w,
               w_mem_k, w_mem_v, w_proj_a, w_proj_b, w_proj_m, w_o, ln1_g, ln1_b,
               router_w, router_b, w_gate_up, b_gate_up, w_down, b_down, ln2_g, ln2_b)
    depth = w_in.shape[0]
    yp, ys = x_prompt, x_sample
    outs = [[] for _ in range(8)]
    for l in range(depth):
        res = _layer(yp, ys, mem_prompt, cache_mem_k[l], cache_mem_v[l], state_rwkv[l], state_shift[l],
                     state_conv[l], tuple(w[l] for w in weights), depth)
        yp, ys = res[0], res[1]
        for acc, r in zip(outs, res[2:]):
            acc.append(r)
    return (yp, ys) + tuple(jnp.stack(o) for o in outs)
```

```python
import functools
import math

import jax
import jax.numpy as jnp
from jax import lax
from jax.experimental import pallas as pl
from jax.experimental.pallas import tpu as pltpu

F32 = jnp.float32
BF16 = jnp.bfloat16

LANES = 128
ROW_SLABS = 16
RW_HEAD = 64
HEADS_PER_BLOCK = LANES // RW_HEAD
RW_ROWS = 64
DECAY_LORA = 96
AICL_LORA = 96
GATE_LORA = 256
LORA_PAD = 512
DECAY_SCALE = math.exp(-0.5)
GN_EPS = 64e-5
LN_EPS = 1e-5
N_EXPERTS = 32
TOP_K = 4
SWIGLU_LIMIT = 7.0
SWIGLU_ALPHA = 1.702
VMEM_LIMIT = 56 * 1024 * 1024


def _dot(a, b):
    return jnp.dot(a, b, preferred_element_type=F32)


def _dot_nt(a, b):
    return lax.dot_general(a, b, (((1,), (1,)), ((), ())), preferred_element_type=F32)


def _dot_tn(a, b):
    return lax.dot_general(a, b, (((0,), (0,)), ((), ())), preferred_element_type=F32)


def _b16(x):
    return x.astype(BF16)


def _params(*sem):
    return pltpu.CompilerParams(dimension_semantics=sem, vmem_limit_bytes=VMEM_LIMIT)


def _split3(x):
    hi = _b16(x)
    r1 = x - hi.astype(F32)
    mid = _b16(r1)
    lo = _b16(r1 - mid.astype(F32))
    return hi, mid, lo


def _rwkv_kernel(pr_ref, pk_ref, pv_ref, pl_ref, qr_ref, qk_ref, qv_ref, ql_ref,
                 mur_ref, muk_ref, muv_ref, mul_ref,
                 w0_ref, a0_ref, kk_ref, ka_ref, rk_ref, gng_ref, gnb_ref,
                 ww_ref, wa_ref, wg_ref, s0_ref,
                 y_ref, sout_ref, *scratch, n_seq, n_chunks, pairs, has_state):
    C = RW_ROWS
    L = C // n_seq
    R = HEADS_PER_BLOCK * C
    W = pairs * LANES
    c = pl.program_id(2)
    carried = n_chunks > 1
    if carried:
        s_sc, cr_sc, ck_sc, cv_sc, cl_sc = scratch

        @pl.when(c == 0)
        def _init_state():
            for p in range(pairs):
                if has_state:
                    s_sc[p] = s0_ref[0, 2 * p:2 * p + 2].reshape(LANES, RW_HEAD)
                else:
                    s_sc[p] = jnp.zeros((LANES, RW_HEAD), F32)

    def prev_rows(q_ref, carry_sc, width):
        if carried:
            @pl.when(c == 0)
            def _():
                carry_sc[0:1, :] = q_ref[0]
            return jnp.broadcast_to(carry_sc[0:1, :], (C, width))
        q = q_ref[...]
        return jnp.broadcast_to(q, (n_seq, L, width)).reshape(C, width)

    def shifted_lerp(p_ref, q_ref, carry_sc, mu_ref, width):
        p = p_ref[...]
        pos = lax.broadcasted_iota(jnp.int32, (C, width), 0) % L
        prev = jnp.where(pos == 0, prev_rows(q_ref, carry_sc, width), pltpu.roll(p, 1, axis=0))
        if carried:
            carry_sc[0:1, :] = p[C - 1:C, :]
        return p + (prev - p) * mu_ref[...]

    zr = shifted_lerp(pr_ref, qr_ref, cr_sc if carried else None, mur_ref, W)
    zk = shifted_lerp(pk_ref, qk_ref, ck_sc if carried else None, muk_ref, W)
    zv = shifted_lerp(pv_ref, qv_ref, cv_sc if carried else None, muv_ref, W)
    zl = shifted_lerp(pl_ref, ql_ref, cl_sc if carried else None, mul_ref, LORA_PAD)

    lw = -DECAY_SCALE * jax.nn.sigmoid(w0_ref[...] + _dot(_b16(jnp.tanh(zl)), ww_ref[...]))
    a_all = jax.nn.sigmoid(a0_ref[...] + _dot(_b16(zl), wa_ref[...]))
    g_all = _dot(_b16(jax.nn.sigmoid(zl)), wg_ref[...])

    ti = lax.broadcasted_iota(jnp.int32, (2 * C, C), 0)
    tj = lax.broadcasted_iota(jnp.int32, (2 * C, C), 1)
    same_seq = (ti % C) // L == tj // L
    cum_lhs = _b16((same_seq & ((ti >= C) | (tj <= ti))).astype(F32))
    hi, mid, lo = _split3(lw)
    cum = _dot(cum_lhs, jnp.concatenate([hi, mid, lo], axis=1))
    cum = cum[:, 0:W] + cum[:, W:2 * W] + cum[:, 2 * W:3 * W]
    cw_all = cum[0:C]
    tot_all = cum[C:2 * C]

    lb_r = lax.broadcasted_iota(jnp.int32, (2 * LANES, LANES), 0) % LANES // RW_HEAD
    lb_c = lax.broadcasted_iota(jnp.int32, (2 * LANES, LANES), 1) // RW_HEAD
    head_ones2 = _b16((lb_r == lb_c).astype(F32))

    def head_sum(x):
        xh = _b16(x)
        xl = _b16(x - xh.astype(F32))
        return _dot(jnp.concatenate([xh, xl], axis=1), head_ones2)

    head0 = lax.broadcasted_iota(jnp.int32, (n_seq, L, LANES), 2) < RW_HEAD

    def stack(x):
        x3 = x.reshape(n_seq, L, LANES)
        return jnp.concatenate([jnp.where(head0, x3, 0.0), jnp.where(head0, 0.0, x3)], axis=1).reshape(R, LANES)

    def unstack(x):
        x3 = x.reshape(n_seq, 2 * L, LANES)
        return (x3[:, 0:L, :] + x3[:, L:2 * L, :]).reshape(C, LANES)

    si = lax.broadcasted_iota(jnp.int32, (R, R), 0)
    sj = lax.broadcasted_iota(jnp.int32, (R, R), 1)
    same = (si // L) == (sj // L)
    strict = same & (si > sj)
    incl = same & (si >= sj)
    eye = (si == sj).astype(F32)
    row_head = (lax.broadcasted_iota(jnp.int32, (R, LANES), 0) // L) % HEADS_PER_BLOCK
    lane_head = lax.broadcasted_iota(jnp.int32, (R, LANES), 1) // RW_HEAD
    head_match = row_head == lane_head
    state_row_head0 = lax.broadcasted_iota(jnp.int32, (LANES, LANES), 0) < RW_HEAD
    n_lvl = int(math.log2(L)) - 1
    S2 = 2 * L

    P = range(pairs)
    lsl = [slice(p * LANES, (p + 1) * LANES) for p in P]

    def rows_cat(xs):
        return jnp.concatenate(xs, axis=0) if len(xs) > 1 else xs[0]

    kk_raw = [zk[:, ls] * kk_ref[:, ls] for ls in lsl]
    kk_n2 = head_sum(rows_cat([x * x for x in kk_raw]))
    kk_l = [kk_raw[p] / jnp.maximum(jnp.sqrt(kk_n2[p * C:(p + 1) * C]), 1e-12) for p in P]
    kmod_l = [zk[:, ls] * (1.0 + (a_all[:, ls] - 1.0) * ka_ref[:, ls]) for ls in lsl]
    beta_l = [kk_l[p] * a_all[:, lsl[p]] for p in P]
    e_neg_l = [jnp.exp(-cw_all[:, ls]) for ls in lsl]
    e_tail_l = [jnp.exp(tot_all[:, ls] - cw_all[:, ls]) for ls in lsl]
    kk_s_l = [stack(kk_l[p] * jnp.exp(cw_all[:, lsl[p]] - lw[:, lsl[p]])) for p in P]
    r_s_l = [stack(zr[:, ls] * jnp.exp(cw_all[:, ls])) for ls in lsl]
    v_b_l = [_b16(stack(zv[:, ls])) for ls in lsl]
    bw_b_l = [_b16(stack(beta_l[p] * e_tail_l[p])) for p in P]
    kw_b_l = [_b16(stack(kmod_l[p] * e_tail_l[p])) for p in P]
    a_lhs = [_b16(jnp.concatenate([kk_s_l[p], r_s_l[p]], axis=0)) for p in P]
    a_rhs = [_b16(jnp.concatenate([stack(beta_l[p] * e_neg_l[p]), stack(kmod_l[p] * e_neg_l[p])], axis=0)) for p in P]

    amat_l = [_dot_nt(a_lhs[p], a_rhs[p]) for p in P]
    a_ak_b = [_b16(jnp.where(strict, m[0:R, R:2 * R], 0.0)) for m in amat_l]
    a_r_b = [_b16(jnp.concatenate([jnp.where(incl, m[R:2 * R, 0:R], 0.0),
                                    jnp.where(incl, m[R:2 * R, R:2 * R], 0.0)], axis=1)) for m in amat_l]

    x_l = [jnp.where(strict, -m[0:R, 0:R], 0.0) for m in amat_l]
    minv_l = [eye + x for x in x_l]
    xb_l = [_b16(x) for x in x_l]
    cur_l = [_dot(xb, xb) for xb in xb_l]
    akv_l = [_dot(a_ak_b[p], v_b_l[p]) for p in P]
    for lvl in range(n_lvl):
        cb_l = [_b16(cur) for cur in cur_l]
        if lvl < n_lvl - 1:
            both_l = [_dot(_b16(jnp.concatenate([minv_l[p], cur_l[p]], axis=0)), cb_l[p]) for p in P]
            minv_l = [minv_l[p] + both_l[p][0:R] for p in P]
            cur_l = [both[R:2 * R] for both in both_l]
        else:
            minv_l = [minv_l[p] + _dot(_b16(minv_l[p]), cb_l[p]) for p in P]

    pq_l = [_dot(_b16(minv_l[p]), _b16(jnp.concatenate([kk_s_l[p], akv_l[p]], axis=1))) for p in P]

    def fold_lanes(x):
        return _b16(x + pltpu.roll(x, RW_HEAD, axis=1))[:, 0:RW_HEAD]

    p_fold = [fold_lanes(-pq[:, 0:LANES]) for pq in pq_l]
    r_fold = [fold_lanes(r_s) for r_s in r_s_l]
    seqs = [(p, b) for p in P for b in range(n_seq)]

    def state_in(p, b):
        if carried:
            return s_sc[p]
        return s0_ref[b, 2 * p:2 * p + 2].reshape(LANES, RW_HEAD)

    states = {pb: state_in(*pb) for pb in seqs}
    sdot = {(p, b): _dot_nt(jnp.concatenate([p_fold[p][b * S2:(b + 1) * S2], r_fold[p][b * S2:(b + 1) * S2]], axis=0),
                            _b16(states[(p, b)])) for (p, b) in seqs}
    u_b_l, rs_l = [], []
    for p in P:
        us = [jnp.where(head_match[b * S2:(b + 1) * S2], sdot[(p, b)][0:S2], 0.0) for b in range(n_seq)]
        rs = [jnp.where(head_match[b * S2:(b + 1) * S2], sdot[(p, b)][S2:2 * S2], 0.0) for b in range(n_seq)]
        u_b_l.append(_b16(rows_cat(us) - pq_l[p][:, LANES:2 * LANES]))
        rs_l.append(rows_cat(rs))

    y_s_l = [rs_l[p] + _dot(a_r_b[p], jnp.concatenate([u_b_l[p], v_b_l[p]], axis=0)) for p in P]
    z_l = {(p, b): _dot_tn(jnp.concatenate([u_b_l[p][b * S2:(b + 1) * S2], v_b_l[p][b * S2:(b + 1) * S2]], axis=0),
                           jnp.concatenate([bw_b_l[p][b * S2:(b + 1) * S2], kw_b_l[p][b * S2:(b + 1) * S2]], axis=0))
           for (p, b) in seqs}
    for (p, b) in seqs:
        z = z_l[(p, b)]
        z = (z + pltpu.roll(z, RW_HEAD, axis=1))[:, 0:RW_HEAD]
        wt = jnp.broadcast_to(jnp.exp(tot_all[b * L:b * L + 1, lsl[p]]), (LANES, LANES))
        wnat = jnp.where(state_row_head0, wt, pltpu.roll(wt, RW_HEAD, axis=1))[:, 0:RW_HEAD]
        s_new = states[(p, b)] * wnat + z
        if carried:
            s_sc[p] = s_new
        sout_ref[b, 2 * p:2 * p + 2] = s_new.reshape(HEADS_PER_BLOCK, RW_HEAD, RW_HEAD)

    y_l = [unstack(y_s) for y_s in y_s_l]
    stats = head_sum(rows_cat(y_l + [zr[:, ls] * kmod_l[p] * rk_ref[:, ls] for p, ls in enumerate(lsl)]))
    d_l = [y_l[p] - stats[p * C:(p + 1) * C] * (1.0 / RW_HEAD) for p in P]
    var = head_sum(rows_cat([d * d for d in d_l])) * (1.0 / RW_HEAD)
    for p, ls in enumerate(lsl):
        yn = d_l[p] * lax.rsqrt(var[p * C:(p + 1) * C] + GN_EPS) * gng_ref[:, ls] + gnb_ref[:, ls]
        bonus = stats[(pairs + p) * C:(pairs + p + 1) * C] * zv[:, ls]
        y_ref[:, ls] = ((yn + bonus) * g_all[:, ls]).astype(y_ref.dtype)


def rwkv_time_mix(p_rw, prev_rw, s0, mu, w0, a0, k_k, k_a, r_k, gn_g, gn_b, ww, wa, wg,
                  *, batch, seq, row_block0, pairs, has_state):
    width = w0.shape[-1]
    wp = pairs * LANES
    n_blk = width // wp
    heads = width // RW_HEAD
    hp = HEADS_PER_BLOCK * pairs
    lora_blk = 3 * width // LORA_PAD
    if seq >= RW_ROWS:
        n_seq, n_chunks, n_outer = 1, seq // RW_ROWS, batch
    else:
        n_seq, n_chunks, n_outer = RW_ROWS // seq, 1, batch * seq // RW_ROWS

    def tok(col0):
        return pl.BlockSpec((RW_ROWS, wp), lambda b, h, c: (row_block0 + b * n_chunks + c, col0 + h))

    def prev(col0):
        return pl.BlockSpec((n_seq, 1, wp), lambda b, h, c: (b, 0, col0 + h))

    def vec(col0):
        return pl.BlockSpec((1, wp), lambda b, h, c: (0, col0 + h))

    lora_w = pl.BlockSpec((LORA_PAD, wp), lambda b, h, c: (0, h))
    state = pl.BlockSpec((n_seq, hp, RW_HEAD, RW_HEAD), lambda b, h, c: (b, h, 0, 0))
    in_specs = [
        tok(0), tok(n_blk), tok(2 * n_blk),
        pl.BlockSpec((RW_ROWS, LORA_PAD), lambda b, h, c: (row_block0 + b * n_chunks + c, lora_blk)),
        prev(0), prev(n_blk), prev(2 * n_blk),
        pl.BlockSpec((n_seq, 1, LORA_PAD), lambda b, h, c: (b, 0, lora_blk)),
        vec(0), vec(n_blk), vec(2 * n_blk),
        pl.BlockSpec((1, LORA_PAD), lambda b, h, c: (0, lora_blk)),
        vec(0), vec(0), vec(0), vec(0), vec(0), vec(0), vec(0),
        lora_w, lora_w, lora_w, state,
    ]
    out_specs = [pl.BlockSpec((RW_ROWS, wp), lambda b, h, c: (b * n_chunks + c, h)), state]
    scratch = []
    if n_chunks > 1:
        scratch = [pltpu.VMEM((pairs, LANES, RW_HEAD), F32), pltpu.VMEM((8, wp), F32), pltpu.VMEM((8, wp), F32),
                   pltpu.VMEM((8, wp), F32), pltpu.VMEM((8, LORA_PAD), F32)]
    kern = functools.partial(_rwkv_kernel, n_seq=n_seq, n_chunks=n_chunks, pairs=pairs, has_state=has_state)
    return pl.pallas_call(
        kern,
        grid=(n_outer, n_blk, n_chunks),
        in_specs=in_specs,
        out_specs=out_specs,
        out_shape=[jax.ShapeDtypeStruct((batch * seq, width), F32),
                   jax.ShapeDtypeStruct((batch, heads, RW_HEAD, RW_HEAD), F32)],
        scratch_shapes=scratch,
        compiler_params=_params("parallel", "parallel", "arbitrary"),
        name=f"rwkv_l{min(seq, RW_ROWS)}",
    )(p_rw, p_rw, p_rw, p_rw, prev_rw, prev_rw, prev_rw, prev_rw, mu, mu, mu, mu,
      w0, a0, k_k, k_a, r_k, gn_g, gn_b, ww, wa, wg, s0)


def _mm_kernel(x_ref, w_ref, o_ref):
    o_ref[...] = _dot(_b16(x_ref[...]), w_ref[...]).astype(o_ref.dtype)


def matmul(x, w, out_dtype, tm, tn, name):
    m, kd = x.shape
    n = w.shape[1]
    return pl.pallas_call(
        _mm_kernel,
        grid=(n // tn, m // tm),
        in_specs=[pl.BlockSpec((tm, kd), lambda j, i: (i, 0)),
                  pl.BlockSpec((kd, tn), lambda j, i: (0, j))],
        out_specs=pl.BlockSpec((tm, tn), lambda j, i: (i, j)),
        out_shape=jax.ShapeDtypeStruct((m, n), out_dtype),
        compiler_params=_params("parallel", "parallel"),
        name=name,
    )(x, w)


def _conv_taps(u, u1, u2, bg_ref, cw_ref, y_ref):
    y = cw_ref[0:1, :] * u2 + cw_ref[1:2, :] * u1 + cw_ref[2:3, :] * u
    y_ref[...] = (bg_ref[...].astype(F32) * y).astype(y_ref.dtype)


def _conv_seq_kernel(bg_ref, cg_ref, xi_ref, cw_ref, y_ref, new_ref, carry_sc, *, rows, n_t):
    i = pl.program_id(2)

    @pl.when(i == 0)
    def _init():
        carry_sc[...] = jnp.zeros(carry_sc.shape, F32)

    u = cg_ref[...].astype(F32) * xi_ref[...].astype(F32)
    r = lax.broadcasted_iota(jnp.int32, u.shape, 0)
    c0 = carry_sc[0:1, :]
    c1 = carry_sc[1:2, :]
    u1 = jnp.where(r == 0, c1, pltpu.roll(u, 1, axis=0))
    u2 = jnp.where(r == 0, c0, jnp.where(r == 1, c1, pltpu.roll(u, 2, axis=0)))
    _conv_taps(u, u1, u2, bg_ref, cw_ref, y_ref)
    carry_sc[0:2, :] = u[rows - 2:rows, :]

    @pl.when(i == n_t - 1)
    def _fin():
        new_ref[0] = u[rows - 2:rows, :]


def conv_sequences(p2, conv_w, *, batch, seq, width, rows, tn):
    nb = width // tn
    n_t = seq // rows

    def tok(sec):
        return pl.BlockSpec((rows, tn), lambda b, j, i: (b * n_t + i, sec * nb + j))

    return pl.pallas_call(
        functools.partial(_conv_seq_kernel, rows=rows, n_t=n_t),
        grid=(batch, nb, n_t),
        in_specs=[tok(0), tok(1), tok(2), pl.BlockSpec((3, tn), lambda b, j, i: (0, j))],
        out_specs=[pl.BlockSpec((rows, tn), lambda b, j, i: (b * n_t + i, j)),
                   pl.BlockSpec((1, 2, tn), lambda b, j, i: (b, 0, j))],
        out_shape=[jax.ShapeDtypeStruct((batch * seq, width), BF16),
                   jax.ShapeDtypeStruct((batch, 2, width), F32)],
        scratch_shapes=[pltpu.VMEM((8, tn), F32)],
        compiler_params=_params("parallel", "parallel", "arbitrary"),
        name="conv_seq",
    )(p2, p2, p2, conv_w)


def _conv_short_kernel(bg_ref, cg_ref, xi_ref, p0_ref, p1_ref, cw_ref, y_ref, n0_ref, n1_ref, *, seq):
    u = cg_ref[...].astype(F32) * xi_ref[...].astype(F32)
    rows, tn = u.shape
    n_seq = rows // seq

    def per_seq(p_ref):
        return jnp.broadcast_to(p_ref[...].reshape(n_seq, 1, tn), (n_seq, seq, tn)).reshape(rows, tn)

    h0, h1 = per_seq(p0_ref), per_seq(p1_ref)
    pos = lax.broadcasted_iota(jnp.int32, u.shape, 0) % seq
    u1 = jnp.where(pos == 0, h1, pltpu.roll(u, 1, axis=0))
    u2 = jnp.where(pos == 0, h0, jnp.where(pos == 1, h1, pltpu.roll(u, 2, axis=0)))
    _conv_taps(u, u1, u2, bg_ref, cw_ref, y_ref)
    u3 = u.reshape(n_seq, seq, tn)
    n0_ref[...] = u3[:, seq - 2, :]
    n1_ref[...] = u3[:, seq - 1, :]


def conv_short(p2, hist0, hist1, conv_w, *, row_block0, batch, seq, width, rows, tn):
    nb = width // tn
    n_seq = rows // seq
    n_r = batch * seq // rows

    def tok(sec):
        return pl.BlockSpec((rows, tn), lambda i, j: (row_block0 + i, sec * nb + j))

    per_seq = pl.BlockSpec((n_seq, tn), lambda i, j: (i, j))
    return pl.pallas_call(
        functools.partial(_conv_short_kernel, seq=seq),
        grid=(n_r, nb),
        in_specs=[tok(0), tok(1), tok(2), per_seq, per_seq, pl.BlockSpec((3, tn), lambda i, j: (0, j))],
        out_specs=[pl.BlockSpec((rows, tn), lambda i, j: (i, j)), per_seq, per_seq],
        out_shape=[jax.ShapeDtypeStruct((batch * seq, width), BF16),
                   jax.ShapeDtypeStruct((batch, width), F32), jax.ShapeDtypeStruct((batch, width), F32)],
        compiler_params=_params("parallel", "parallel"),
        name="conv_short",
    )(p2, p2, p2, hist0, hist1, conv_w)


def _softmax_rows(s):
    m = jnp.max(s, axis=-1, keepdims=True)
    e = jnp.exp(s - m)
    return e / jnp.sum(e, axis=-1, keepdims=True)


def _attn_seq_kernel(q_ref, k_ref, v_ref, o_ref, *, scale):
    s = _dot_nt(q_ref[...], _b16(k_ref[...])) * scale
    p = _softmax_rows(s)
    o_ref[...] = _dot(_b16(p), _b16(v_ref[...])).astype(o_ref.dtype)


def attention_sequences(p2, q_col0, mem_k, mem_v, *, batch, seq, n_mem, heads, head_dim, tq):
    n_t = seq // tq
    kv = pl.BlockSpec((n_mem, head_dim), lambda b, h, i: (b, h))
    return pl.pallas_call(
        functools.partial(_attn_seq_kernel, scale=head_dim ** -0.5),
        grid=(batch, heads, n_t),
        in_specs=[pl.BlockSpec((tq, head_dim), lambda b, h, i: (b * n_t + i, q_col0 + h)), kv, kv],
        out_specs=pl.BlockSpec((tq, head_dim), lambda b, h, i: (b * n_t + i, h)),
        out_shape=jax.ShapeDtypeStruct((batch * seq, heads * head_dim), BF16),
        compiler_params=_params("parallel", "parallel", "arbitrary"),
        name="attn_seq",
    )(p2, mem_k, mem_v)


def _attn_cache_kernel(q_ref, k_ref, v_ref, o_ref, *, scale, bb, seq, heads, head_dim):
    tiles = head_dim // LANES
    slabs = tiles * heads
    n_mem = k_ref.shape[1] // slabs
    q = q_ref[...].astype(F32)

    def head_slab(ref, b, h):
        return jnp.concatenate([ref[b, pl.ds(t * heads + h, n_mem, stride=slabs), :] for t in range(tiles)], axis=1)

    rows = []
    for b in range(bb):
        cols = []
        for h in range(heads):
            qb = _b16(q[b * seq:(b + 1) * seq, h * head_dim:(h + 1) * head_dim])
            s = _dot_nt(qb, _b16(head_slab(k_ref, b, h))) * scale
            p = _softmax_rows(s)
            cols.append(_dot(_b16(p), _b16(head_slab(v_ref, b, h))))
        rows.append(jnp.concatenate(cols, axis=1))
    o_ref[...] = jnp.concatenate(rows, axis=0).astype(o_ref.dtype)


def attention_cache(p2, q_blk0, row0, cache_k, cache_v, *, batch, seq, heads, head_dim, bb):
    d = heads * head_dim
    kv = pl.BlockSpec((bb,) + cache_k.shape[1:], lambda g: (g, 0, 0))
    return pl.pallas_call(
        functools.partial(_attn_cache_kernel, scale=head_dim ** -0.5, bb=bb, seq=seq, heads=heads, head_dim=head_dim),
        grid=(batch // bb,),
        in_specs=[pl.BlockSpec((bb * seq, d), lambda g: (row0 + g, q_blk0)), kv, kv],
        out_specs=pl.BlockSpec((bb * seq, d), lambda g: (g, 0)),
        out_shape=jax.ShapeDtypeStruct((batch * seq, d), BF16),
        compiler_params=_params("parallel"),
        name="attn_cache",
    )(p2, cache_k, cache_v)


def _merge_kernel(ya_ref, yb_ref, ym_ref, g0_ref, g1_ref, g2_ref, wa_ref, wb_ref, wm_ref, o_ref):
    acc = jax.nn.sigmoid(g0_ref[...].astype(F32)) * _dot(_b16(ya_ref[...]), wa_ref[...])
    acc += jax.nn.sigmoid(g1_ref[...].astype(F32)) * _dot(_b16(yb_ref[...]), wb_ref[...])
    acc += jax.nn.sigmoid(g2_ref[...].astype(F32)) * _dot(_b16(ym_ref[...]), wm_ref[...])
    o_ref[...] = acc.astype(o_ref.dtype)


def merge_branches(ya, yb, ym, p2, gate_col0, wa, wb, wm, *, tm, tn):
    m, d = ya.shape
    nb = d // tn
    act = pl.BlockSpec((tm, d), lambda j, i: (i, 0))
    wsp = pl.BlockSpec((d, tn), lambda j, i: (0, j))

    def gate(k):
        return pl.BlockSpec((tm, tn), lambda j, i: (i, gate_col0 + k * nb + j))

    return pl.pallas_call(
        _merge_kernel,
        grid=(nb, m // tm),
        in_specs=[act, act, act, gate(0), gate(1), gate(2), wsp, wsp, wsp],
        out_specs=pl.BlockSpec((tm, tn), lambda j, i: (i, j)),
        out_shape=jax.ShapeDtypeStruct((m, d), BF16),
        compiler_params=_params("parallel", "parallel"),
        name="merge",
    )(ya, yb, ym, p2, p2, p2, wa, wb, wm)


def _layer_norm(y, g, b):
    mu = jnp.mean(y, axis=-1, keepdims=True)
    d = y - mu
    var = jnp.mean(d * d, axis=-1, keepdims=True)
    return d * lax.rsqrt(var + LN_EPS) * g + b


def _to_slabs(ref, row0, x):
    rows = x.shape[0]
    for j in range(x.shape[1] // LANES):
        ref[pl.ds(row0 * ROW_SLABS + j, rows, stride=ROW_SLABS), :] = x[:, j * LANES:(j + 1) * LANES]


def _from_slabs(ref, row0, rows, offset=0, stride=ROW_SLABS):
    return jnp.concatenate(
        [ref[pl.ds(row0 * stride + offset + j, rows, stride=stride), :] for j in range(ROW_SLABS)], axis=1)


def _proj_ln_kernel(m_ref, x_ref, wo_ref, g_ref, b_ref, rw_ref, rb_ref, x1_ref, lg_ref, *, alpha):
    h = _dot(m_ref[...], wo_ref[...])
    x1 = _layer_norm(alpha * x_ref[...] + h, g_ref[...], b_ref[...])
    _to_slabs(x1_ref, 0, x1)
    hi, mid, lo = _split3(x1)
    w_hi, w_mid, w_lo = _split3(rw_ref[...])
    lg = _dot(hi, w_hi) + (_dot(hi, w_mid) + _dot(mid, w_hi)) + (_dot(hi, w_lo) + _dot(mid, w_mid) + _dot(lo, w_hi))
    lg_ref[...] = lg + rb_ref[...]


def project_norm_route(merged, x, wo, ln_g, ln_b, router_w, router_b, *, alpha, tm):
    m, d = x.shape
    n_r = router_w.shape[1]
    row = pl.BlockSpec((tm, d), lambda i: (i, 0))
    vec = pl.BlockSpec((1, d), lambda i: (0, 0))
    return pl.pallas_call(
        functools.partial(_proj_ln_kernel, alpha=alpha),
        grid=(m // tm,),
        in_specs=[row, row, pl.BlockSpec((d, d), lambda i: (0, 0)), vec, vec,
                  pl.BlockSpec((d, n_r), lambda i: (0, 0)), pl.BlockSpec((1, n_r), lambda i: (0, 0))],
        out_specs=[pl.BlockSpec((tm * ROW_SLABS, LANES), lambda i: (i, 0)),
                   pl.BlockSpec((tm, n_r), lambda i: (i, 0))],
        out_shape=[jax.ShapeDtypeStruct((m * ROW_SLABS, LANES), F32), jax.ShapeDtypeStruct((m, n_r), F32)],
        compiler_params=_params("parallel"),
        name="proj_ln_route",
    )(merged, x, wo, ln_g, ln_b, router_w, router_b)


MOE_GROUP = 1536
MOE_ROW_TILE = 256
MOE_F_TILE = 256
MOE_N_TILE = 256
COMBINE_TOKENS = 128


def _row_copy(src_hbm, src_row, dst_vmem, dst_row, sem):
    return pltpu.make_async_copy(
        src_hbm.at[pl.ds(pl.multiple_of(src_row * ROW_SLABS, ROW_SLABS), ROW_SLABS)],
        dst_vmem.at[pl.ds(pl.multiple_of(dst_row * ROW_SLABS, ROW_SLABS), ROW_SLABS)], sem)


def _start_row_gather(idx_ref, n_rows, src_hbm, buf, sem):
    def issue(r, carry):
        _row_copy(src_hbm, idx_ref[0, 0, r], buf, r, sem).start()
        return carry

    lax.fori_loop(0, n_rows, issue, 0, unroll=8)


def _wait_row_gather(n_rows, src_hbm, buf, sem):
    pltpu.make_async_copy(src_hbm.at[pl.ds(0, n_rows * ROW_SLABS)], buf, sem).wait()


def _dispatch_kernel(tv_ref, tok_ref, nxt_ref, x_hbm, o_ref, buf, sem):
    t = pl.program_id(0)
    n = pl.num_programs(0)
    slot = t % 2
    rt = MOE_ROW_TILE

    @pl.when((t == 0) & (tv_ref[0] > 0))
    def _first():
        _start_row_gather(tok_ref, rt, x_hbm, buf.at[0], sem.at[0])

    @pl.when((t + 1 < n) & (tv_ref[jnp.minimum(t + 1, n - 1)] > 0))
    def _next():
        _start_row_gather(nxt_ref, rt, x_hbm, buf.at[1 - slot], sem.at[1 - slot])

    @pl.when(tv_ref[t] > 0)
    def _():
        _wait_row_gather(rt, x_hbm, buf.at[slot], sem.at[slot])
        o_ref[...] = _b16(_from_slabs(buf.at[slot], 0, rt))

    @pl.when(tv_ref[t] == 0)
    def _():
        o_ref[...] = jnp.zeros(o_ref.shape, o_ref.dtype)


def moe_dispatch(x_slabs, slot_tok, tile_valid, d):
    n_tiles = tile_valid.shape[0]
    rt = MOE_ROW_TILE
    grid_spec = pltpu.PrefetchScalarGridSpec(
        num_scalar_prefetch=1,
        grid=(n_tiles,),
        in_specs=[pl.BlockSpec((1, 1, rt), lambda t, tv: (t, 0, 0), memory_space=pltpu.SMEM),
                  pl.BlockSpec((1, 1, rt), lambda t, tv: (jnp.minimum(t + 1, n_tiles - 1), 0, 0),
                               memory_space=pltpu.SMEM),
                  pl.BlockSpec(memory_space=pl.ANY)],
        out_specs=pl.BlockSpec((rt, d), lambda t, tv: (t, 0)),
        scratch_shapes=[pltpu.VMEM((2, rt * ROW_SLABS, LANES), F32), pltpu.SemaphoreType.DMA((2,))],
    )
    idx = slot_tok.reshape(n_tiles, 1, rt)
    return pl.pallas_call(
        _dispatch_kernel,
        grid_spec=grid_spec,
        out_shape=jax.ShapeDtypeStruct((n_tiles * rt, d), BF16),
        compiler_params=_params("arbitrary"),
        name="moe_dispatch",
    )(tile_valid, idx, idx, x_slabs)


def _moe_kernel(ie_ref, ib_ref, nr_ref, x_ref, gate_ref, wg_ref, wu_ref, wd_ref, bg_ref, bu_ref, bd_ref,
                o_ref, h_sc, *, n_f):
    i = pl.program_id(0)
    s = pl.program_id(1)
    rows = nr_ref[i]
    rt = MOE_ROW_TILE
    n_tiles = (rows + rt - 1) // rt

    @pl.when((rows > 0) & (s < n_f))
    def _hidden():
        wg = _b16(wg_ref[0])
        wu = _b16(wu_ref[0])

        def tile(r, carry):
            sl = pl.ds(pl.multiple_of(r * rt, rt), rt)
            x = x_ref[sl, :]
            hg = jnp.minimum(_dot(x, wg) + bg_ref[0], SWIGLU_LIMIT)
            hu = jnp.clip(_dot(x, wu) + bu_ref[0], -SWIGLU_LIMIT, SWIGLU_LIMIT)
            h_sc[s, sl, :] = _b16(hg * jax.nn.sigmoid(SWIGLU_ALPHA * hg) * (hu + 1.0))
            return carry

        lax.fori_loop(0, n_tiles, tile, 0)

    @pl.when((rows > 0) & (s >= n_f))
    def _down():
        n = s - n_f
        wd = _b16(wd_ref[0])
        tf = h_sc.shape[2]

        def tile(r, carry):
            sl = pl.ds(pl.multiple_of(r * rt, rt), rt)
            acc = _dot(h_sc[0, sl, :], wd[0:tf, :])
            for f in range(1, n_f):
                acc += _dot(h_sc[f, sl, :], wd[f * tf:(f + 1) * tf, :])
            val = (acc + bd_ref[0]) * gate_ref[sl, :]
            for jj in range(MOE_N_TILE // LANES):
                start = r * (rt * ROW_SLABS) + n * (MOE_N_TILE // LANES) + jj
                o_ref[pl.ds(start, rt, stride=ROW_SLABS), :] = val[:, jj * LANES:(jj + 1) * LANES]
            return carry

        lax.fori_loop(0, n_tiles, tile, 0)

    @pl.when(s == 0)
    def _blank():
        def blank(r, carry):
            sl = pl.ds(pl.multiple_of(r * (rt * ROW_SLABS), rt * ROW_SLABS), rt * ROW_SLABS)
            o_ref[sl, :] = jnp.zeros((rt * ROW_SLABS, LANES), F32)
            return carry

        lax.fori_loop(n_tiles, MOE_GROUP // rt, blank, 0)


def moe_experts(x_sorted, slot_gate, item_expert, item_block, item_rows, w_gate_up, b_gate_up, w_down, b_down):
    n_items = item_expert.shape[0]
    d = x_sorted.shape[1]
    n_exp, d_ff, d_out = w_down.shape
    n_f = d_ff // MOE_F_TILE
    n_n = d_out // MOE_N_TILE

    def f_idx(i, s, nr):
        return jnp.minimum(jnp.where(nr[i] > 0, s, n_f - 1), n_f - 1)

    def n_idx(i, s, nr):
        return jnp.where(nr[i] > 0, jnp.maximum(s - n_f, 0), n_n - 1)

    single = pl.Buffered(1)
    grid_spec = pltpu.PrefetchScalarGridSpec(
        num_scalar_prefetch=3,
        grid=(n_items, n_f + n_n),
        in_specs=[
            pl.BlockSpec((MOE_GROUP, d), lambda i, s, ie, ib, nr: (ib[i], 0)),
            pl.BlockSpec((MOE_GROUP, 1), lambda i, s, ie, ib, nr: (ib[i], 0)),
            pl.BlockSpec((1, d, MOE_F_TILE), lambda i, s, ie, ib, nr: (ie[i], 0, f_idx(i, s, nr))),
            pl.BlockSpec((1, d, MOE_F_TILE), lambda i, s, ie, ib, nr: (ie[i], 0, n_f + f_idx(i, s, nr))),
            pl.BlockSpec((1, d_ff, MOE_N_TILE), lambda i, s, ie, ib, nr: (ie[i], 0, n_idx(i, s, nr))),
            pl.BlockSpec((1, 1, MOE_F_TILE), lambda i, s, ie, ib, nr: (ie[i], 0, f_idx(i, s, nr))),
            pl.BlockSpec((1, 1, MOE_F_TILE), lambda i, s, ie, ib, nr: (ie[i], 0, n_f + f_idx(i, s, nr))),
            pl.BlockSpec((1, 1, MOE_N_TILE), lambda i, s, ie, ib, nr: (ie[i], 0, n_idx(i, s, nr))),
        ],
        out_specs=pl.BlockSpec((MOE_GROUP * ROW_SLABS, LANES), lambda i, s, ie, ib, nr: (i, 0),
                               pipeline_mode=single),
        scratch_shapes=[pltpu.VMEM((n_f, MOE_GROUP, MOE_F_TILE), BF16)],
    )
    return pl.pallas_call(
        functools.partial(_moe_kernel, n_f=n_f),
        grid_spec=grid_spec,
        out_shape=jax.ShapeDtypeStruct((n_items * MOE_GROUP * ROW_SLABS, LANES), F32),
        compiler_params=_params("arbitrary", "arbitrary"),
        name="moe_experts",
    )(item_expert, item_block, item_rows, x_sorted, slot_gate, w_gate_up, w_gate_up, w_down,
      b_gate_up.reshape(n_exp, 1, 2 * d_ff), b_gate_up.reshape(n_exp, 1, 2 * d_ff), b_down.reshape(n_exp, 1, d_out))


def _combine_kernel(pos_ref, nxt_ref, y_hbm, x1_ref, g_ref, b_ref, oa_ref, ob_ref, buf, sem, *, alpha, n_a):
    tm = COMBINE_TOKENS
    t = pl.program_id(0)
    slot = t % 2

    @pl.when(t == 0)
    def _first():
        _start_row_gather(pos_ref, tm * TOP_K, y_hbm, buf.at[0], sem.at[0])

    @pl.when(t + 1 < pl.num_programs(0))
    def _next():
        _start_row_gather(nxt_ref, tm * TOP_K, y_hbm, buf.at[1 - slot], sem.at[1 - slot])

    cur = buf.at[slot]
    _wait_row_gather(tm * TOP_K, y_hbm, cur, sem.at[slot])
    f = _from_slabs(cur, 0, tm, 0, TOP_K * ROW_SLABS)
    for k in range(1, TOP_K):
        f += _from_slabs(cur, 0, tm, k * ROW_SLABS, TOP_K * ROW_SLABS)
    out = _layer_norm(alpha * _from_slabs(x1_ref, 0, tm) + f, g_ref[...], b_ref[...])

    @pl.when(t < n_a)
    def _():
        oa_ref[...] = out

    @pl.when(t >= n_a)
    def _():
        ob_ref[...] = out


def moe_combine_norm(y_slabs, pos, x1_slabs, ln_g, ln_b, *, alpha, n_first):
    n_tok = pos.shape[0]
    tm = COMBINE_TOKENS
    d = ln_g.shape[1]
    n_t = n_tok // tm
    n_a = n_first // tm
    vec = pl.BlockSpec((1, d), lambda t: (0, 0))
    idx = pos.reshape(n_t, 1, tm * TOP_K)
    return pl.pallas_call(
        functools.partial(_combine_kernel, alpha=alpha, n_a=n_a),
        grid=(n_t,),
        in_specs=[pl.BlockSpec((1, 1, tm * TOP_K), lambda t: (t, 0, 0), memory_space=pltpu.SMEM),
                  pl.BlockSpec((1, 1, tm * TOP_K), lambda t: (jnp.minimum(t + 1, n_t - 1), 0, 0),
                               memory_space=pltpu.SMEM),
                  pl.BlockSpec(memory_space=pl.ANY),
                  pl.BlockSpec((tm * ROW_SLABS, LANES), lambda t: (t, 0)), vec, vec],
        out_specs=[pl.BlockSpec((tm, d), lambda t: (jnp.minimum(t, n_a - 1), 0)),
                   pl.BlockSpec((tm, d), lambda t: (jnp.maximum(t - n_a, 0), 0))],
        out_shape=[jax.ShapeDtypeStruct((n_first, d), F32), jax.ShapeDtypeStruct((n_tok - n_first, d), F32)],
        scratch_shapes=[pltpu.VMEM((2, tm * TOP_K * ROW_SLABS, LANES), F32), pltpu.SemaphoreType.DMA((2,))],
        compiler_params=_params("arbitrary"),
        name="moe_combine",
    )(idx, idx, y_slabs, x1_slabs, ln_g, ln_b)


def _route(logits, n_items):
    n_tok = logits.shape[0]
    n_assign = n_tok * TOP_K
    tiles_per_item = MOE_GROUP // MOE_ROW_TILE
    top_logit, top_idx = lax.top_k(logits, TOP_K)
    gate = jax.nn.softmax(top_logit, axis=-1)
    flat_e = top_idx.reshape(-1).astype(jnp.int32)
    order = jnp.argsort(flat_e).astype(jnp.int32)
    rank = jnp.argsort(order).astype(jnp.int32)
    experts = jnp.arange(N_EXPERTS, dtype=jnp.int32)
    onehot = flat_e[:, None] == experts[None, :]
    counts = jnp.sum(onehot, axis=0, dtype=jnp.int32)
    padded = (counts + MOE_GROUP - 1) // MOE_GROUP * MOE_GROUP
    start = jnp.cumsum(counts) - counts
    padded_end = jnp.cumsum(padded)
    padded_start = padded_end - padded
    shift = padded_start - start
    pos = (rank + jnp.sum(jnp.where(onehot, shift[None, :], 0), axis=1)).reshape(n_tok, TOP_K)

    def expert_of(row0):
        return jnp.minimum(jnp.sum(row0[:, None] >= padded_end[None, :], axis=1, dtype=jnp.int32), N_EXPERTS - 1)

    def table(values, e):
        return jnp.sum(jnp.where(e[:, None] == experts[None, :], values[None, :], 0), axis=1)

    n_rows = n_items * MOE_GROUP
    slot = jnp.arange(n_rows, dtype=jnp.int32)
    e_slot = expert_of(slot)
    within = slot - table(padded_start, e_slot)
    slot_ok = (within < table(counts, e_slot)) & (slot < padded_end[-1])
    src = order[jnp.clip(table(start, e_slot) + within, 0, n_assign - 1)]
    slot_tok = jnp.where(slot_ok, src // TOP_K, 0)
    slot_gate = jnp.where(slot_ok, gate.reshape(-1)[src], 0.0)
    n_real = (padded_end[-1] // MOE_GROUP).astype(jnp.int32)
    item = jnp.minimum(jnp.arange(n_items, dtype=jnp.int32), jnp.maximum(n_real - 1, 0))
    item_expert = expert_of(item * MOE_GROUP)
    filled = jnp.clip(table(padded_start + counts, item_expert) - item * MOE_GROUP, 0, MOE_GROUP)
    item_rows = jnp.where(jnp.arange(n_items) < n_real, filled, 0).astype(jnp.int32)
    tile_row0 = (jnp.arange(tiles_per_item, dtype=jnp.int32) * MOE_ROW_TILE)[None, :]
    tile_valid = jnp.clip(item_rows[:, None] - tile_row0, 0, MOE_ROW_TILE).reshape(-1).astype(jnp.int32)
    return slot_tok, slot_gate, pos, item_expert, item.astype(jnp.int32), item_rows, tile_valid


def _pad_rw_cols(a, rw_cols, pad):
    lead = a.shape[:-1]
    return jnp.concatenate([a[..., :rw_cols], jnp.zeros(lead + (pad,), a.dtype), a[..., rw_cols:]], axis=-1)


def _cache_rows(c, heads, head_dim):
    b, n_mem = c.shape[:2]
    tiles = head_dim // LANES
    c = c.reshape(b, n_mem, heads, tiles, LANES).transpose(0, 1, 3, 2, 4)
    return c.reshape(b, n_mem * tiles * heads, LANES)


def _layer(xp, xs, mem_prompt, cache_k, cache_v, st_rwkv, st_shift, st_conv, wts, depth):
    (w_in, w_w_up, w0, w_a_up, a0, w_g_up, tshift_mu, k_k, k_a, r_k, gn_g, gn_b, conv_w,
     w_mem_k, w_mem_v, w_proj_a, w_proj_b, w_proj_m, w_o, ln1_g, ln1_b,
     router_w, router_b, w_gate_up, b_gate_up, w_down, b_down, ln2_g, ln2_b) = wts
    bp, sp, d = xp.shape
    bs, ss, _ = xs.shape
    n_mem = mem_prompt.shape[1]
    mem_heads, mem_head = cache_k.shape[-2:]
    n_p, n_s = bp * sp, bs * ss
    n_tok = n_p + n_s
    alpha = (2.0 * depth) ** 0.25
    rw_cols = 3 * d + DECAY_LORA + AICL_LORA + GATE_LORA
    pad = LORA_PAD - (rw_cols - 3 * d)
    rw_pad = rw_cols + pad

    x_all = jnp.concatenate([xp.reshape(n_p, d), xs.reshape(n_s, d)], axis=0)
    x_bf = _b16(x_all)
    w_cat = _b16(_pad_rw_cols(w_in, rw_cols, pad))
    mu_pad = _pad_rw_cols(tshift_mu, rw_cols, pad)[None, :rw_pad]
    w_rw = w_cat[:, :rw_pad]
    w_rest = w_cat[:, rw_pad:]

    p_rw = matmul(x_bf, w_rw, F32, 1024, 512, "in_proj_rw")
    p2 = matmul(x_bf, w_rest, BF16, 1024, 512, "in_proj_rest")
    prev_s = matmul(_b16(st_shift), w_rw, F32, bs, 512, "prev_proj").reshape(bs, 1, rw_pad)
    prev_p = jnp.zeros((bp, 1, rw_pad), F32)

    def lora_rows(w, row0):
        return _b16(jnp.zeros((LORA_PAD, d), F32).at[row0:row0 + w.shape[0]].set(w))

    ww = lora_rows(w_w_up, 0)
    wa = lora_rows(w_a_up, DECAY_LORA)
    wg = lora_rows(w_g_up, DECAY_LORA + AICL_LORA)
    r2 = lambda v: v.reshape(1, d)
    rw_vecs = (mu_pad, r2(w0), r2(a0), r2(k_k), r2(k_a), r2(r_k), r2(gn_g), r2(gn_b), ww, wa, wg)
    heads = d // RW_HEAD
    ya_p, rw_p = rwkv_time_mix(p_rw, prev_p, jnp.zeros((bp, heads, RW_HEAD, RW_HEAD), F32), *rw_vecs,
                               batch=bp, seq=sp, row_block0=0, pairs=4, has_state=False)
    ya_s, rw_s = rwkv_time_mix(p_rw, prev_s, st_rwkv, *rw_vecs,
                               batch=bs, seq=ss, row_block0=n_p // RW_ROWS, pairs=2, has_state=True)
    ya = jnp.concatenate([ya_p, ya_s], axis=0)

    yb_p, cv_p = conv_sequences(p2, conv_w, batch=bp, seq=sp, width=d, rows=256, tn=512)
    yb_s, cv_s0, cv_s1 = conv_short(p2, st_conv[:, 0, :], st_conv[:, 1, :], conv_w, row_block0=n_p // 128,
                                    batch=bs, seq=ss, width=d, rows=128, tn=512)
    yb = jnp.concatenate([yb_p, yb_s], axis=0)
    cv_s = jnp.stack([cv_s0, cv_s1], axis=1)

    mem_in = _b16(mem_prompt.reshape(bp * n_mem, d))
    mk = matmul(mem_in, _b16(w_mem_k), F32, bp * n_mem, 512, "mem_k")
    mv = matmul(mem_in, _b16(w_mem_v), F32, bp * n_mem, 512, "mem_v")
    q_col0 = 3 * d // mem_head
    ym_p = attention_sequences(p2, q_col0, mk, mv, batch=bp, seq=sp, n_mem=n_mem, heads=mem_heads,
                               head_dim=mem_head, tq=512)
    bb = 2
    ym_s = attention_cache(p2, 3 * d // d, n_p // (bb * ss), _cache_rows(cache_k, mem_heads, mem_head),
                           _cache_rows(cache_v, mem_heads, mem_head),
                           batch=bs, seq=ss, heads=mem_heads, head_dim=mem_head, bb=bb)
    ym = jnp.concatenate([ym_p, ym_s], axis=0)

    merged = merge_branches(ya, yb, ym, p2, 4 * d // 512, _b16(w_proj_a), _b16(w_proj_b),
                            _b16(w_proj_m), tm=512, tn=512)
    rw_pad_r = jnp.zeros((d, LANES), F32).at[:, :N_EXPERTS].set(router_w)
    rb_pad_r = jnp.zeros((1, LANES), F32).at[0, :N_EXPERTS].set(router_b)
    x1_slabs, logits = project_norm_route(merged, x_all, _b16(w_o), r2(ln1_g), r2(ln1_b),
                                          rw_pad_r, rb_pad_r, alpha=alpha, tm=256)

    n_items = n_tok * TOP_K // MOE_GROUP + N_EXPERTS
    slot_tok, slot_gate, pos, item_expert, item_block, item_rows, tile_valid = _route(logits[:, :N_EXPERTS], n_items)
    x_sorted = moe_dispatch(x1_slabs, slot_tok, tile_valid, d)
    y_slabs = moe_experts(x_sorted, slot_gate[:, None], item_expert, item_block, item_rows,
                          w_gate_up, b_gate_up, w_down, b_down)
    y_p, y_s = moe_combine_norm(y_slabs, pos, x1_slabs, r2(ln2_g), r2(ln2_b), alpha=alpha, n_first=n_p)
    y_p = y_p.reshape(bp, sp, d)
    y_s = y_s.reshape(bs, ss, d)
    mk5 = mk.reshape(bp, n_mem, mem_heads, mem_head)
    mv5 = mv.reshape(bp, n_mem, mem_heads, mem_head)
    return y_p, y_s, mk5, mv5, rw_p, xp[:, -1], cv_p, rw_s, xs[:, -1], cv_s


def kernel(x_prompt, x_sample, mem_prompt, cache_mem_k, cache_mem_v, state_rwkv, state_shift, state_conv, w_in, w_w_up, w0, w_a_up, a0, w_g_up, tshift_mu, k_k, k_a, r_k, gn_g, gn_b, conv_w, w_mem_k, w_mem_v, w_proj_a, w_proj_b, w_proj_m, w_o, ln1_g, ln1_b, router_w, router_b, w_gate_up, b_gate_up, w_down, b_down, ln2_g, ln2_b):
    weights = (w_in, w_w_up, w0, w_a_up, a0, w_g_up, tshift_mu, k_k, k_a, r_k, gn_g, gn_b, conv_w,
               w_mem_k, w_mem_v, w_proj_a, w_proj_b, w_proj_m, w_o, ln1_g, ln1_b,
               router_w, router_b, w_gate_up, b_gate_up, w_down, b_down, ln2_g, ln2_b)
    depth = w_in.shape[0]
    yp, ys = x_prompt, x_sample
    outs = [[] for _ in range(8)]
    for l in range(depth):
        res = _layer(yp, ys, mem_prompt, cache_mem_k[l], cache_mem_v[l], state_rwkv[l], state_shift[l],
                     state_conv[l], tuple(w[l] for w in weights), depth)
        yp, ys = res[0], res[1]
        for acc, r in zip(outs, res[2:]):
            acc.append(r)
    return (yp, ys) + tuple(jnp.stack(o) for o in outs)
```

```python
import functools
import math

import jax
import jax.numpy as jnp
from jax import lax
from jax.experimental import pallas as pl
from jax.experimental.pallas import tpu as pltpu

F32 = jnp.float32
BF16 = jnp.bfloat16

LANES = 128
ROW_SLABS = 16
RW_HEAD = 64
HEADS_PER_BLOCK = LANES // RW_HEAD
RW_ROWS = 64
DECAY_LORA = 96
AICL_LORA = 96
GATE_LORA = 256
LORA_PAD = 512
DECAY_SCALE = math.exp(-0.5)
GN_EPS = 64e-5
LN_EPS = 1e-5
N_EXPERTS = 32
TOP_K = 4
SWIGLU_LIMIT = 7.0
SWIGLU_ALPHA = 1.702
VMEM_LIMIT = 56 * 1024 * 1024


def _dot(a, b):
    return jnp.dot(a, b, preferred_element_type=F32)


def _dot_nt(a, b):
    return lax.dot_general(a, b, (((1,), (1,)), ((), ())), preferred_element_type=F32)


def _dot_tn(a, b):
    return lax.dot_general(a, b, (((0,), (0,)), ((), ())), preferred_element_type=F32)


def _b16(x):
    return x.astype(BF16)


def _params(*sem):
    return pltpu.CompilerParams(dimension_semantics=sem, vmem_limit_bytes=VMEM_LIMIT)


def _split3(x):
    hi = _b16(x)
    r1 = x - hi.astype(F32)
    mid = _b16(r1)
    lo = _b16(r1 - mid.astype(F32))
    return hi, mid, lo


def _rwkv_kernel(pr_ref, pk_ref, pv_ref, pl_ref, qr_ref, qk_ref, qv_ref, ql_ref,
                 mur_ref, muk_ref, muv_ref, mul_ref,
                 w0_ref, a0_ref, kk_ref, ka_ref, rk_ref, gng_ref, gnb_ref,
                 ww_ref, wa_ref, wg_ref, s0_ref,
                 y_ref, sout_ref, *scratch, n_seq, n_chunks, pairs, has_state):
    C = RW_ROWS
    L = C // n_seq
    R = HEADS_PER_BLOCK * C
    W = pairs * LANES
    c = pl.program_id(2)
    carried = n_chunks > 1
    if carried:
        s_sc, cr_sc, ck_sc, cv_sc, cl_sc = scratch

        @pl.when(c == 0)
        def _init_state():
            for p in range(pairs):
                if has_state:
                    s_sc[p] = s0_ref[0, 2 * p:2 * p + 2].reshape(LANES, RW_HEAD)
                else:
                    s_sc[p] = jnp.zeros((LANES, RW_HEAD), F32)

    def prev_rows(q_ref, carry_sc, width):
        if carried:
            @pl.when(c == 0)
            def _():
                carry_sc[0:1, :] = q_ref[0]
            return jnp.broadcast_to(carry_sc[0:1, :], (C, width))
        q = q_ref[...]
        return jnp.broadcast_to(q, (n_seq, L, width)).reshape(C, width)

    def shifted_lerp(p_ref, q_ref, carry_sc, mu_ref, width):
        p = p_ref[...]
        pos = lax.broadcasted_iota(jnp.int32, (C, width), 0) % L
        prev = jnp.where(pos == 0, prev_rows(q_ref, carry_sc, width), pltpu.roll(p, 1, axis=0))
        if carried:
            carry_sc[0:1, :] = p[C - 1:C, :]
        return p + (prev - p) * mu_ref[...]

    zr = shifted_lerp(pr_ref, qr_ref, cr_sc if carried else None, mur_ref, W)
    zk = shifted_lerp(pk_ref, qk_ref, ck_sc if carried else None, muk_ref, W)
    zv = shifted_lerp(pv_ref, qv_ref, cv_sc if carried else None, muv_ref, W)
    zl = shifted_lerp(pl_ref, ql_ref, cl_sc if carried else None, mul_ref, LORA_PAD)

    lw = -DECAY_SCALE * jax.nn.sigmoid(w0_ref[...] + _dot(_b16(jnp.tanh(zl)), ww_ref[...]))
    a_all = jax.nn.sigmoid(a0_ref[...] + _dot(_b16(zl), wa_ref[...]))
    g_all = _dot(_b16(jax.nn.sigmoid(zl)), wg_ref[...])

    ti = lax.broadcasted_iota(jnp.int32, (2 * C, C), 0)
    tj = lax.broadcasted_iota(jnp.int32, (2 * C, C), 1)
    same_seq = (ti % C) // L == tj // L
    cum_lhs = _b16((same_seq & ((ti >= C) | (tj <= ti))).astype(F32))
    hi, mid, lo = _split3(lw)
    cum = _dot(cum_lhs, jnp.concatenate([hi, mid, lo], axis=1))
    cum = cum[:, 0:W] + cum[:, W:2 * W] + cum[:, 2 * W:3 * W]
    cw_all = cum[0:C]
    tot_all = cum[C:2 * C]

    lb_r = lax.broadcasted_iota(jnp.int32, (2 * LANES, LANES), 0) % LANES // RW_HEAD
    lb_c = lax.broadcasted_iota(jnp.int32, (2 * LANES, LANES), 1) // RW_HEAD
    head_ones2 = _b16((lb_r == lb_c).astype(F32))

    def head_sum(x):
        xh = _b16(x)
        xl = _b16(x - xh.astype(F32))
        return _dot(jnp.concatenate([xh, xl], axis=1), head_ones2)

    head0 = lax.broadcasted_iota(jnp.int32, (n_seq, L, LANES), 2) < RW_HEAD

    def stack(x):
        x3 = x.reshape(n_seq, L, LANES)
        return jnp.concatenate([jnp.where(head0, x3, 0.0), jnp.where(head0, 0.0, x3)], axis=1).reshape(R, LANES)

    def unstack(x):
        x3 = x.reshape(n_seq, 2 * L, LANES)
        return (x3[:, 0:L, :] + x3[:, L:2 * L, :]).reshape(C, LANES)

    si = lax.broadcasted_iota(jnp.int32, (R, R), 0)
    sj = lax.broadcasted_iota(jnp.int32, (R, R), 1)
    same = (si // L) == (sj // L)
    strict = same & (si > sj)
    incl = same & (si >= sj)
    eye = (si == sj).astype(F32)
    row_head = (lax.broadcasted_iota(jnp.int32, (R, LANES), 0) // L) % HEADS_PER_BLOCK
    lane_head = lax.broadcasted_iota(jnp.int32, (R, LANES), 1) // RW_HEAD
    head_match = row_head == lane_head
    state_row_head0 = lax.broadcasted_iota(jnp.int32, (LANES, LANES), 0) < RW_HEAD
    n_lvl = int(math.log2(L)) - 1
    S2 = 2 * L

    P = range(pairs)
    lsl = [slice(p * LANES, (p + 1) * LANES) for p in P]

    def rows_cat(xs):
        return jnp.concatenate(xs, axis=0) if len(xs) > 1 else xs[0]

    kk_raw = [zk[:, ls] * kk_ref[:, ls] for ls in lsl]
    kk_n2 = head_sum(rows_cat([x * x for x in kk_raw]))
    kk_l = [kk_raw[p] / jnp.maximum(jnp.sqrt(kk_n2[p * C:(p + 1) * C]), 1e-12) for p in P]
    kmod_l = [zk[:, ls] * (1.0 + (a_all[:, ls] - 1.0) * ka_ref[:, ls]) for ls in lsl]
    beta_l = [kk_l[p] * a_all[:, lsl[p]] for p in P]
    e_neg_l = [jnp.exp(-cw_all[:, ls]) for ls in lsl]
    e_tail_l = [jnp.exp(tot_all[:, ls] - cw_all[:, ls]) for ls in lsl]
    kk_s_l = [stack(kk_l[p] * jnp.exp(cw_all[:, lsl[p]] - lw[:, lsl[p]])) for p in P]
    r_s_l = [stack(zr[:, ls] * jnp.exp(cw_all[:, ls])) for ls in lsl]
    v_b_l = [_b16(stack(zv[:, ls])) for ls in lsl]
    bw_b_l = [_b16(stack(beta_l[p] * e_tail_l[p])) for p in P]
    kw_b_l = [_b16(stack(kmod_l[p] * e_tail_l[p])) for p in P]
    a_lhs = [_b16(jnp.concatenate([kk_s_l[p], r_s_l[p]], axis=0)) for p in P]
    a_rhs = [_b16(jnp.concatenate([stack(beta_l[p] * e_neg_l[p]), stack(kmod_l[p] * e_neg_l[p])], axis=0)) for p in P]

    amat_l = [_dot_nt(a_lhs[p], a_rhs[p]) for p in P]
    a_ak_b = [_b16(jnp.where(strict, m[0:R, R:2 * R], 0.0)) for m in amat_l]
    a_r_b = [_b16(jnp.concatenate([jnp.where(incl, m[R:2 * R, 0:R], 0.0),
                                    jnp.where(incl, m[R:2 * R, R:2 * R], 0.0)], axis=1)) for m in amat_l]

    x_l = [jnp.where(strict, -m[0:R, 0:R], 0.0) for m in amat_l]
    minv_l = [eye + x for x in x_l]
    xb_l = [_b16(x) for x in x_l]
    cur_l = [_dot(xb, xb) for xb in xb_l]
    akv_l = [_dot(a_ak_b[p], v_b_l[p]) for p in P]
    for lvl in range(n_lvl):
        cb_l = [_b16(cur) for cur in cur_l]
        if lvl < n_lvl - 1:
            both_l = [_dot(_b16(jnp.concatenate([minv_l[p], cur_l[p]], axis=0)), cb_l[p]) for p in P]
            minv_l = [minv_l[p] + both_l[p][0:R] for p in P]
            cur_l = [both[R:2 * R] for both in both_l]
        else:
            minv_l = [minv_l[p] + _dot(_b16(minv_l[p]), cb_l[p]) for p in P]

    pq_l = [_dot(_b16(minv_l[p]), _b16(jnp.concatenate([kk_s_l[p], akv_l[p]], axis=1))) for p in P]

    def fold_lanes(x):
        return _b16(x + pltpu.roll(x, RW_HEAD, axis=1))[:, 0:RW_HEAD]

    p_fold = [fold_lanes(-pq[:, 0:LANES]) for pq in pq_l]
    r_fold = [fold_lanes(r_s) for r_s in r_s_l]
    seqs = [(p, b) for p in P for b in range(n_seq)]

    def state_in(p, b):
        if carried:
            return s_sc[p]
        return s0_ref[b, 2 * p:2 * p + 2].reshape(LANES, RW_HEAD)

    states = {pb: state_in(*pb) for pb in seqs}
    sdot = {(p, b): _dot_nt(jnp.concatenate([p_fold[p][b * S2:(b + 1) * S2], r_fold[p][b * S2:(b + 1) * S2]], axis=0),
                            _b16(states[(p, b)])) for (p, b) in seqs}
    u_b_l, rs_l = [], []
    for p in P:
        us = [jnp.where(head_match[b * S2:(b + 1) * S2], sdot[(p, b)][0:S2], 0.0) for b in range(n_seq)]
        rs = [jnp.where(head_match[b * S2:(b + 1) * S2], sdot[(p, b)][S2:2 * S2], 0.0) for b in range(n_seq)]
        u_b_l.append(_b16(rows_cat(us) - pq_l[p][:, LANES:2 * LANES]))
        rs_l.append(rows_cat(rs))

    y_s_l = [rs_l[p] + _dot(a_r_b[p], jnp.concatenate([u_b_l[p], v_b_l[p]], axis=0)) for p in P]
    z_l = {(p, b): _dot_tn(jnp.concatenate([u_b_l[p][b * S2:(b + 1) * S2], v_b_l[p][b * S2:(b + 1) * S2]], axis=0),
                           jnp.concatenate([bw_b_l[p][b * S2:(b + 1) * S2], kw_b_l[p][b * S2:(b + 1) * S2]], axis=0))
           for (p, b) in seqs}
    for (p, b) in seqs:
        z = z_l[(p, b)]
        z = (z + pltpu.roll(z, RW_HEAD, axis=1))[:, 0:RW_HEAD]
        wt = jnp.broadcast_to(jnp.exp(tot_all[b * L:b * L + 1, lsl[p]]), (LANES, LANES))
        wnat = jnp.where(state_row_head0, wt, pltpu.roll(wt, RW_HEAD, axis=1))[:, 0:RW_HEAD]
        s_new = states[(p, b)] * wnat + z
        if carried:
            s_sc[p] = s_new
        sout_ref[b, 2 * p:2 * p + 2] = s_new.reshape(HEADS_PER_BLOCK, RW_HEAD, RW_HEAD)

    y_l = [unstack(y_s) for y_s in y_s_l]
    stats = head_sum(rows_cat(y_l + [zr[:, ls] * kmod_l[p] * rk_ref[:, ls] for p, ls in enumerate(lsl)]))
    d_l = [y_l[p] - stats[p * C:(p + 1) * C] * (1.0 / RW_HEAD) for p in P]
    var = head_sum(rows_cat([d * d for d in d_l])) * (1.0 / RW_HEAD)
    for p, ls in enumerate(lsl):
        yn = d_l[p] * lax.rsqrt(var[p * C:(p + 1) * C] + GN_EPS) * gng_ref[:, ls] + gnb_ref[:, ls]
        bonus = stats[(pairs + p) * C:(pairs + p + 1) * C] * zv[:, ls]
        y_ref[:, ls] = ((yn + bonus) * g_all[:, ls]).astype(y_ref.dtype)


def rwkv_time_mix(p_rw, prev_rw, s0, mu, w0, a0, k_k, k_a, r_k, gn_g, gn_b, ww, wa, wg,
                  *, batch, seq, row_block0, pairs, has_state):
    width = w0.shape[-1]
    wp = pairs * LANES
    n_blk = width // wp
    heads = width // RW_HEAD
    hp = HEADS_PER_BLOCK * pairs
    lora_blk = 3 * width // LORA_PAD
    if seq >= RW_ROWS:
        n_seq, n_chunks, n_outer = 1, seq // RW_ROWS, batch
    else:
        n_seq, n_chunks, n_outer = RW_ROWS // seq, 1, batch * seq // RW_ROWS

    def tok(col0):
        return pl.BlockSpec((RW_ROWS, wp), lambda b, h, c: (row_block0 + b * n_chunks + c, col0 + h))

    def prev(col0):
        return pl.BlockSpec((n_seq, 1, wp), lambda b, h, c: (b, 0, col0 + h))

    def vec(col0):
        return pl.BlockSpec((1, wp), lambda b, h, c: (0, col0 + h))

    lora_w = pl.BlockSpec((LORA_PAD, wp), lambda b, h, c: (0, h))
    state = pl.BlockSpec((n_seq, hp, RW_HEAD, RW_HEAD), lambda b, h, c: (b, h, 0, 0))
    in_specs = [
        tok(0), tok(n_blk), tok(2 * n_blk),
        pl.BlockSpec((RW_ROWS, LORA_PAD), lambda b, h, c: (row_block0 + b * n_chunks + c, lora_blk)),
        prev(0), prev(n_blk), prev(2 * n_blk),
        pl.BlockSpec((n_seq, 1, LORA_PAD), lambda b, h, c: (b, 0, lora_blk)),
        vec(0), vec(n_blk), vec(2 * n_blk),
        pl.BlockSpec((1, LORA_PAD), lambda b, h, c: (0, lora_blk)),
        vec(0), vec(0), vec(0), vec(0), vec(0), vec(0), vec(0),
        lora_w, lora_w, lora_w, state,
    ]
    out_specs = [pl.BlockSpec((RW_ROWS, wp), lambda b, h, c: (b * n_chunks + c, h)), state]
    scratch = []
    if n_chunks > 1:
        scratch = [pltpu.VMEM((pairs, LANES, RW_HEAD), F32), pltpu.VMEM((8, wp), F32), pltpu.VMEM((8, wp), F32),
                   pltpu.VMEM((8, wp), F32), pltpu.VMEM((8, LORA_PAD), F32)]
    kern = functools.partial(_rwkv_kernel, n_seq=n_seq, n_chunks=n_chunks, pairs=pairs, has_state=has_state)
    return pl.pallas_call(
        kern,
        grid=(n_outer, n_blk, n_chunks),
        in_specs=in_specs,
        out_specs=out_specs,
        out_shape=[jax.ShapeDtypeStruct((batch * seq, width), F32),
                   jax.ShapeDtypeStruct((batch, heads, RW_HEAD, RW_HEAD), F32)],
        scratch_shapes=scratch,
        compiler_params=_params("parallel", "parallel", "arbitrary"),
        name=f"rwkv_l{min(seq, RW_ROWS)}",
    )(p_rw, p_rw, p_rw, p_rw, prev_rw, prev_rw, prev_rw, prev_rw, mu, mu, mu, mu,
      w0, a0, k_k, k_a, r_k, gn_g, gn_b, ww, wa, wg, s0)


def _mm_kernel(x_ref, w_ref, o_ref):
    o_ref[...] = _dot(_b16(x_ref[...]), w_ref[...]).astype(o_ref.dtype)


def matmul(x, w, out_dtype, tm, tn, name):
    m, kd = x.shape
    n = w.shape[1]
    return pl.pallas_call(
        _mm_kernel,
        grid=(n // tn, m // tm),
        in_specs=[pl.BlockSpec((tm, kd), lambda j, i: (i, 0)),
                  pl.BlockSpec((kd, tn), lambda j, i: (0, j))],
        out_specs=pl.BlockSpec((tm, tn), lambda j, i: (i, j)),
        out_shape=jax.ShapeDtypeStruct((m, n), out_dtype),
        compiler_params=_params("parallel", "parallel"),
        name=name,
    )(x, w)


def _conv_taps(u, u1, u2, bg_ref, cw_ref, y_ref):
    y = cw_ref[0:1, :] * u2 + cw_ref[1:2, :] * u1 + cw_ref[2:3, :] * u
    y_ref[...] = (bg_ref[...].astype(F32) * y).astype(y_ref.dtype)


def _conv_seq_kernel(bg_ref, cg_ref, xi_ref, cw_ref, y_ref, new_ref, carry_sc, *, rows, n_t):
    i = pl.program_id(2)

    @pl.when(i == 0)
    def _init():
        carry_sc[...] = jnp.zeros(carry_sc.shape, F32)

    u = cg_ref[...].astype(F32) * xi_ref[...].astype(F32)
    r = lax.broadcasted_iota(jnp.int32, u.shape, 0)
    c0 = carry_sc[0:1, :]
    c1 = carry_sc[1:2, :]
    u1 = jnp.where(r == 0, c1, pltpu.roll(u, 1, axis=0))
    u2 = jnp.where(r == 0, c0, jnp.where(r == 1, c1, pltpu.roll(u, 2, axis=0)))
    _conv_taps(u, u1, u2, bg_ref, cw_ref, y_ref)
    carry_sc[0:2, :] = u[rows - 2:rows, :]

    @pl.when(i == n_t - 1)
    def _fin():
        new_ref[0] = u[rows - 2:rows, :]


def conv_sequences(p2, conv_w, *, batch, seq, width, rows, tn):
    nb = width // tn
    n_t = seq // rows

    def tok(sec):
        return pl.BlockSpec((rows, tn), lambda b, j, i: (b * n_t + i, sec * nb + j))

    return pl.pallas_call(
        functools.partial(_conv_seq_kernel, rows=rows, n_t=n_t),
        grid=(batch, nb, n_t),
        in_specs=[tok(0), tok(1), tok(2), pl.BlockSpec((3, tn), lambda b, j, i: (0, j))],
        out_specs=[pl.BlockSpec((rows, tn), lambda b, j, i: (b * n_t + i, j)),
                   pl.BlockSpec((1, 2, tn), lambda b, j, i: (b, 0, j))],
        out_shape=[jax.ShapeDtypeStruct((batch * seq, width), BF16),
                   jax.ShapeDtypeStruct((batch, 2, width), F32)],
        scratch_shapes=[pltpu.VMEM((8, tn), F32)],
        compiler_params=_params("parallel", "parallel", "arbitrary"),
        name="conv_seq",
    )(p2, p2, p2, conv_w)


def _conv_short_kernel(bg_ref, cg_ref, xi_ref, p0_ref, p1_ref, cw_ref, y_ref, n0_ref, n1_ref, *, seq):
    u = cg_ref[...].astype(F32) * xi_ref[...].astype(F32)
    rows, tn = u.shape
    n_seq = rows // seq

    def per_seq(p_ref):
        return jnp.broadcast_to(p_ref[...].reshape(n_seq, 1, tn), (n_seq, seq, tn)).reshape(rows, tn)

    h0, h1 = per_seq(p0_ref), per_seq(p1_ref)
    pos = lax.broadcasted_iota(jnp.int32, u.shape, 0) % seq
    u1 = jnp.where(pos == 0, h1, pltpu.roll(u, 1, axis=0))
    u2 = jnp.where(pos == 0, h0, jnp.where(pos == 1, h1, pltpu.roll(u, 2, axis=0)))
    _conv_taps(u, u1, u2, bg_ref, cw_ref, y_ref)
    u3 = u.reshape(n_seq, seq, tn)
    n0_ref[...] = u3[:, seq - 2, :]
    n1_ref[...] = u3[:, seq - 1, :]


def conv_short(p2, hist0, hist1, conv_w, *, row_block0, batch, seq, width, rows, tn):
    nb = width // tn
    n_seq = rows // seq
    n_r = batch * seq // rows

    def tok(sec):
        return pl.BlockSpec((rows, tn), lambda i, j: (row_block0 + i, sec * nb + j))

    per_seq = pl.BlockSpec((n_seq, tn), lambda i, j: (i, j))
    return pl.pallas_call(
        functools.partial(_conv_short_kernel, seq=seq),
        grid=(n_r, nb),
        in_specs=[tok(0), tok(1), tok(2), per_seq, per_seq, pl.BlockSpec((3, tn), lambda i, j: (0, j))],
        out_specs=[pl.BlockSpec((rows, tn), lambda i, j: (i, j)), per_seq, per_seq],
        out_shape=[jax.ShapeDtypeStruct((batch * seq, width), BF16),
                   jax.ShapeDtypeStruct((batch, width), F32), jax.ShapeDtypeStruct((batch, width), F32)],
        compiler_params=_params("parallel", "parallel"),
        name="conv_short",
    )(p2, p2, p2, hist0, hist1, conv_w)


def _softmax_rows(s):
    m = jnp.max(s, axis=-1, keepdims=True)
    e = jnp.exp(s - m)
    return e / jnp.sum(e, axis=-1, keepdims=True)


def _attn_seq_kernel(q_ref, k_ref, v_ref, o_ref, *, scale):
    s = _dot_nt(q_ref[...], _b16(k_ref[...])) * scale
    p = _softmax_rows(s)
    o_ref[...] = _dot(_b16(p), _b16(v_ref[...])).astype(o_ref.dtype)


def attention_sequences(p2, q_col0, mem_k, mem_v, *, batch, seq, n_mem, heads, head_dim, tq):
    n_t = seq // tq
    kv = pl.BlockSpec((n_mem, head_dim), lambda b, h, i: (b, h))
    return pl.pallas_call(
        functools.partial(_attn_seq_kernel, scale=head_dim ** -0.5),
        grid=(batch, heads, n_t),
        in_specs=[pl.BlockSpec((tq, head_dim), lambda b, h, i: (b * n_t + i, q_col0 + h)), kv, kv],
        out_specs=pl.BlockSpec((tq, head_dim), lambda b, h, i: (b * n_t + i, h)),
        out_shape=jax.ShapeDtypeStruct((batch * seq, heads * head_dim), BF16),
        compiler_params=_params("parallel", "parallel", "arbitrary"),
        name="attn_seq",
    )(p2, mem_k, mem_v)


def _attn_cache_kernel(q_ref, k_ref, v_ref, o_ref, *, scale, bb, seq, heads, head_dim):
    tiles = head_dim // LANES
    slabs = tiles * heads
    n_mem = k_ref.shape[1] // slabs
    q = q_ref[...].astype(F32)

    def head_slab(ref, b, h):
        return jnp.concatenate([ref[b, pl.ds(t * heads + h, n_mem, stride=slabs), :] for t in range(tiles)], axis=1)

    rows = []
    for b in range(bb):
        cols = []
        for h in range(heads):
            qb = _b16(q[b * seq:(b + 1) * seq, h * head_dim:(h + 1) * head_dim])
            s = _dot_nt(qb, _b16(head_slab(k_ref, b, h))) * scale
            p = _softmax_rows(s)
            cols.append(_dot(_b16(p), _b16(head_slab(v_ref, b, h))))
        rows.append(jnp.concatenate(cols, axis=1))
    o_ref[...] = jnp.concatenate(rows, axis=0).astype(o_ref.dtype)


def attention_cache(p2, q_blk0, row0, cache_k, cache_v, *, batch, seq, heads, head_dim, bb):
    d = heads * head_dim
    kv = pl.BlockSpec((bb,) + cache_k.shape[1:], lambda g: (g, 0, 0))
    return pl.pallas_call(
        functools.partial(_attn_cache_kernel, scale=head_dim ** -0.5, bb=bb, seq=seq, heads=heads, head_dim=head_dim),
        grid=(batch // bb,),
        in_specs=[pl.BlockSpec((bb * seq, d), lambda g: (row0 + g, q_blk0)), kv, kv],
        out_specs=pl.BlockSpec((bb * seq, d), lambda g: (g, 0)),
        out_shape=jax.ShapeDtypeStruct((batch * seq, d), BF16),
        compiler_params=_params("parallel"),
        name="attn_cache",
    )(p2, cache_k, cache_v)


def _merge_kernel(ya_ref, yb_ref, ym_ref, g0_ref, g1_ref, g2_ref, wa_ref, wb_ref, wm_ref, o_ref):
    acc = jax.nn.sigmoid(g0_ref[...].astype(F32)) * _dot(_b16(ya_ref[...]), wa_ref[...])
    acc += jax.nn.sigmoid(g1_ref[...].astype(F32)) * _dot(_b16(yb_ref[...]), wb_ref[...])
    acc += jax.nn.sigmoid(g2_ref[...].astype(F32)) * _dot(_b16(ym_ref[...]), wm_ref[...])
    o_ref[...] = acc.astype(o_ref.dtype)


def merge_branches(ya, yb, ym, p2, gate_col0, wa, wb, wm, *, tm, tn):
    m, d = ya.shape
    nb = d // tn
    act = pl.BlockSpec((tm, d), lambda j, i: (i, 0))
    wsp = pl.BlockSpec((d, tn), lambda j, i: (0, j))

    def gate(k):
        return pl.BlockSpec((tm, tn), lambda j, i: (i, gate_col0 + k * nb + j))

    return pl.pallas_call(
        _merge_kernel,
        grid=(nb, m // tm),
        in_specs=[act, act, act, gate(0), gate(1), gate(2), wsp, wsp, wsp],
        out_specs=pl.BlockSpec((tm, tn), lambda j, i: (i, j)),
        out_shape=jax.ShapeDtypeStruct((m, d), BF16),
        compiler_params=_params("parallel", "parallel"),
        name="merge",
    )(ya, yb, ym, p2, p2, p2, wa, wb, wm)


def _layer_norm(y, g, b):
    mu = jnp.mean(y, axis=-1, keepdims=True)
    d = y - mu
    var = jnp.mean(d * d, axis=-1, keepdims=True)
    return d * lax.rsqrt(var + LN_EPS) * g + b


def _to_slabs(ref, row0, x):
    rows = x.shape[0]
    for j in range(x.shape[1] // LANES):
        ref[pl.ds(row0 * ROW_SLABS + j, rows, stride=ROW_SLABS), :] = x[:, j * LANES:(j + 1) * LANES]


def _from_slabs(ref, row0, rows, offset=0, stride=ROW_SLABS):
    return jnp.concatenate(
        [ref[pl.ds(row0 * stride + offset + j, rows, stride=stride), :] for j in range(ROW_SLABS)], axis=1)


def _proj_ln_kernel(m_ref, x_ref, wo_ref, g_ref, b_ref, rw_ref, rb_ref, x1_ref, lg_ref, *, alpha):
    h = _dot(m_ref[...], wo_ref[...])
    x1 = _layer_norm(alpha * x_ref[...] + h, g_ref[...], b_ref[...])
    _to_slabs(x1_ref, 0, x1)
    hi, mid, lo = _split3(x1)
    w_hi, w_mid, w_lo = _split3(rw_ref[...])
    lg = _dot(hi, w_hi) + (_dot(hi, w_mid) + _dot(mid, w_hi)) + (_dot(hi, w_lo) + _dot(mid, w_mid) + _dot(lo, w_hi))
    lg_ref[...] = lg + rb_ref[...]


def project_norm_route(merged, x, wo, ln_g, ln_b, router_w, router_b, *, alpha, tm):
    m, d = x.shape
    n_r = router_w.shape[1]
    row = pl.BlockSpec((tm, d), lambda i: (i, 0))
    vec = pl.BlockSpec((1, d), lambda i: (0, 0))
    return pl.pallas_call(
        functools.partial(_proj_ln_kernel, alpha=alpha),
        grid=(m // tm,),
        in_specs=[row, row, pl.BlockSpec((d, d), lambda i: (0, 0)), vec, vec,
                  pl.BlockSpec((d, n_r), lambda i: (0, 0)), pl.BlockSpec((1, n_r), lambda i: (0, 0))],
        out_specs=[pl.BlockSpec((tm * ROW_SLABS, LANES), lambda i: (i, 0)),
                   pl.BlockSpec((tm, n_r), lambda i: (i, 0))],
        out_shape=[jax.ShapeDtypeStruct((m * ROW_SLABS, LANES), F32), jax.ShapeDtypeStruct((m, n_r), F32)],
        compiler_params=_params("parallel"),
        name="proj_ln_route",
    )(merged, x, wo, ln_g, ln_b, router_w, router_b)


MOE_GROUP = 1280
MOE_ROW_TILE = 640
MOE_F_TILE = 256
MOE_N_TILE = 256
MOE_K_CHUNK = 256
COMBINE_TOKENS = 128


def _row_copy(src_hbm, src_row, dst_vmem, dst_row, sem):
    return pltpu.make_async_copy(
        src_hbm.at[pl.ds(pl.multiple_of(src_row * ROW_SLABS, ROW_SLABS), ROW_SLABS)],
        dst_vmem.at[pl.ds(pl.multiple_of(dst_row * ROW_SLABS, ROW_SLABS), ROW_SLABS)], sem)


def _start_row_gather(src_row, n_rows, src_hbm, buf, sem):
    def issue(r, carry):
        _row_copy(src_hbm, src_row(r), buf, r, sem).start()
        return carry

    lax.fori_loop(0, n_rows, issue, 0, unroll=8)


def _wait_row_gather(n_rows, src_hbm, buf, sem):
    pltpu.make_async_copy(src_hbm.at[pl.ds(0, n_rows * ROW_SLABS)], buf, sem).wait()


def _dispatch_kernel(tv_ref, tj_ref, order_ref, x_hbm, o_ref, buf, sem):
    t = pl.program_id(0)
    n = pl.num_programs(0)
    slot = t % 2
    rt = MOE_ROW_TILE
    last = order_ref.shape[0] - 1

    def start(tile, dst):
        j0 = tj_ref[tile]
        _start_row_gather(lambda r: order_ref[jnp.minimum(j0 + r, last)] // TOP_K, rt, x_hbm, buf.at[dst], sem.at[dst])

    @pl.when((t == 0) & (tv_ref[0] > 0))
    def _first():
        start(0, 0)

    @pl.when((t + 1 < n) & (tv_ref[jnp.minimum(t + 1, n - 1)] > 0))
    def _next():
        start(jnp.minimum(t + 1, n - 1), 1 - slot)

    @pl.when(tv_ref[t] > 0)
    def _():
        _wait_row_gather(rt, x_hbm, buf.at[slot], sem.at[slot])
        o_ref[...] = _b16(_from_slabs(buf.at[slot], 0, rt))

    @pl.when(tv_ref[t] == 0)
    def _():
        o_ref[...] = jnp.zeros(o_ref.shape, o_ref.dtype)


def moe_dispatch(x_slabs, order, tile_valid, tile_first, d):
    n_tiles = tile_valid.shape[0]
    rt = MOE_ROW_TILE
    grid_spec = pltpu.PrefetchScalarGridSpec(
        num_scalar_prefetch=3,
        grid=(n_tiles,),
        in_specs=[pl.BlockSpec(memory_space=pl.ANY)],
        out_specs=pl.BlockSpec((rt, d), lambda t, tv, tj, od: (t, 0)),
        scratch_shapes=[pltpu.VMEM((2, rt * ROW_SLABS, LANES), F32), pltpu.SemaphoreType.DMA((2,))],
    )
    return pl.pallas_call(
        _dispatch_kernel,
        grid_spec=grid_spec,
        out_shape=jax.ShapeDtypeStruct((n_tiles * rt, d), BF16),
        compiler_params=_params("arbitrary"),
        name="moe_dispatch",
    )(tile_valid, tile_first, order, x_slabs)


def _moe_kernel(ie_ref, ib_ref, nr_ref, j0_ref, x_ref, wg_ref, wu_ref, wd_ref, bg_ref, bu_ref, bd_ref,
                y_hbm, h_sc, wgu_sc, wd_sc, o_sc, pend_sc, sem, *, n_f, n_n, n_assign):
    i = pl.program_id(0)
    s = pl.program_id(1)
    rows = nr_ref[i]
    rt = MOE_ROW_TILE
    tile_slabs = rt * ROW_SLABS
    n_tiles = (rows + rt - 1) // rt
    tf = MOE_F_TILE
    last_step = n_f + n_n - 1

    def tile_copy(r, row0):
        return pltpu.make_async_copy(
            o_sc.at[pl.ds(pl.multiple_of(r * tile_slabs, tile_slabs), tile_slabs)],
            y_hbm.at[pl.ds(pl.multiple_of((row0 + r * rt) * ROW_SLABS, ROW_SLABS), tile_slabs)], sem)

    def wait_pending():
        def one(k, carry):
            tile_copy(0, 0).wait()
            return carry

        lax.fori_loop(0, pend_sc[0], one, 0)
        pend_sc[0] = 0

    @pl.when((i == 0) & (s == 0))
    def _reset():
        pend_sc[0] = 0

    @pl.when((rows > 0) & (s < n_f))
    def _hidden():
        def project(r):
            return _dot(x_ref[pl.ds(pl.multiple_of(r * rt, rt), rt), :], wgu_sc[...])

        def activate(h, r):
            hg = jnp.minimum(h[:, 0:tf] + bg_ref[0], SWIGLU_LIMIT)
            hu = jnp.clip(h[:, tf:2 * tf] + bu_ref[0], -SWIGLU_LIMIT, SWIGLU_LIMIT)
            h_sc[s, pl.ds(pl.multiple_of(r * rt, rt), rt), :] = _b16(hg * jax.nn.sigmoid(SWIGLU_ALPHA * hg) * (hu + 1.0))

        kc = MOE_K_CHUNK
        h0 = None
        for c in range(wg_ref.shape[1] // kc):
            ks = slice(c * kc, (c + 1) * kc)
            w_c = jnp.concatenate([_b16(wg_ref[0, ks, :]), _b16(wu_ref[0, ks, :])], axis=1)
            wgu_sc[ks, :] = w_c
            part = _dot(x_ref[0:rt, ks], w_c)
            h0 = part if h0 is None else h0 + part
        activate(h0, 0)

        def tile(r, carry):
            activate(project(r), r)
            return carry

        lax.fori_loop(1, n_tiles, tile, 0)

    @pl.when((rows > 0) & (s >= n_f))
    def _down():
        n = s - n_f

        @pl.when(s == n_f)
        def _():
            wait_pending()

        def project(r):
            sl = pl.ds(pl.multiple_of(r * rt, rt), rt)
            acc = _dot(h_sc[0, sl, :], wd_sc[0:tf, :])
            for f in range(1, n_f):
                acc += _dot(h_sc[f, sl, :], wd_sc[f * tf:(f + 1) * tf, :])
            return acc

        def emit(acc, r):
            val = acc + bd_ref[0]
            for jj in range(MOE_N_TILE // LANES):
                start = r * tile_slabs + n * (MOE_N_TILE // LANES) + jj
                o_sc[pl.ds(start, rt, stride=ROW_SLABS), :] = val[:, jj * LANES:(jj + 1) * LANES]

        acc0 = None
        for f in range(n_f):
            fs = slice(f * tf, (f + 1) * tf)
            w_f = _b16(wd_ref[0, fs, :])
            wd_sc[fs, :] = w_f
            part = _dot(h_sc[f, 0:rt, :], w_f)
            acc0 = part if acc0 is None else acc0 + part
        emit(acc0, 0)

        def tile(r, carry):
            emit(project(r), r)
            return carry

        lax.fori_loop(1, n_tiles, tile, 0)

        @pl.when(s == last_step)
        def _():
            row0 = j0_ref[i]

            def send(r, carry):
                tile_copy(r, row0).start()
                return carry

            lax.fori_loop(0, n_tiles, send, 0)
            pend_sc[0] = n_tiles

    @pl.when((i == pl.num_programs(0) - 1) & (s == last_step))
    def _finish():
        wait_pending()
        o_sc[0:tile_slabs, :] = jnp.zeros((tile_slabs, LANES), F32)
        tail = tile_copy(0, n_assign)
        tail.start()
        tail.wait()


def moe_experts(x_sorted, item_expert, item_block, item_rows, item_first, n_assign,
                w_gate_up, b_gate_up, w_down, b_down):
    n_items = item_expert.shape[0]
    d = x_sorted.shape[1]
    n_exp, d_ff, d_out = w_down.shape
    n_f = d_ff // MOE_F_TILE
    n_n = d_out // MOE_N_TILE

    def f_idx(i, s, nr):
        return jnp.minimum(jnp.where(nr[i] > 0, s, n_f - 1), n_f - 1)

    def n_idx(i, s, nr):
        return jnp.where(nr[i] > 0, jnp.maximum(s - n_f, 0), n_n - 1)

    grid_spec = pltpu.PrefetchScalarGridSpec(
        num_scalar_prefetch=4,
        grid=(n_items, n_f + n_n),
        in_specs=[
            pl.BlockSpec((MOE_GROUP, d), lambda i, s, ie, ib, nr, j0: (ib[i], 0)),
            pl.BlockSpec((1, d, MOE_F_TILE), lambda i, s, ie, ib, nr, j0: (ie[i], 0, f_idx(i, s, nr))),
            pl.BlockSpec((1, d, MOE_F_TILE), lambda i, s, ie, ib, nr, j0: (ie[i], 0, n_f + f_idx(i, s, nr))),
            pl.BlockSpec((1, d_ff, MOE_N_TILE), lambda i, s, ie, ib, nr, j0: (ie[i], 0, n_idx(i, s, nr))),
            pl.BlockSpec((1, 1, MOE_F_TILE), lambda i, s, ie, ib, nr, j0: (ie[i], 0, f_idx(i, s, nr))),
            pl.BlockSpec((1, 1, MOE_F_TILE), lambda i, s, ie, ib, nr, j0: (ie[i], 0, n_f + f_idx(i, s, nr))),
            pl.BlockSpec((1, 1, MOE_N_TILE), lambda i, s, ie, ib, nr, j0: (ie[i], 0, n_idx(i, s, nr))),
        ],
        out_specs=pl.BlockSpec(memory_space=pl.ANY),
        scratch_shapes=[pltpu.VMEM((n_f, MOE_GROUP, MOE_F_TILE), BF16),
                        pltpu.VMEM((d, 2 * MOE_F_TILE), BF16),
                        pltpu.VMEM((d_ff, MOE_N_TILE), BF16),
                        pltpu.VMEM((MOE_GROUP * ROW_SLABS, LANES), F32),
                        pltpu.SMEM((1,), jnp.int32),
                        pltpu.SemaphoreType.DMA],
    )
    return pl.pallas_call(
        functools.partial(_moe_kernel, n_f=n_f, n_n=n_n, n_assign=n_assign),
        grid_spec=grid_spec,
        out_shape=jax.ShapeDtypeStruct(((n_assign + MOE_ROW_TILE) * ROW_SLABS, LANES), F32),
        compiler_params=_params("arbitrary", "arbitrary"),
        name="moe_experts",
    )(item_expert, item_block, item_rows, item_first, x_sorted, w_gate_up, w_gate_up, w_down,
      b_gate_up.reshape(n_exp, 1, 2 * d_ff), b_gate_up.reshape(n_exp, 1, 2 * d_ff), b_down.reshape(n_exp, 1, d_out))


def _combine_kernel(pos_ref, nxt_ref, y_hbm, x1_ref, gate_ref, g_ref, b_ref, oa_ref, ob_ref, buf, sem, *, alpha, n_a):
    tm = COMBINE_TOKENS
    t = pl.program_id(0)
    slot = t % 2

    @pl.when(t == 0)
    def _first():
        _start_row_gather(lambda r: pos_ref[0, 0, r], tm * TOP_K, y_hbm, buf.at[0], sem.at[0])

    @pl.when(t + 1 < pl.num_programs(0))
    def _next():
        _start_row_gather(lambda r: nxt_ref[0, 0, r], tm * TOP_K, y_hbm, buf.at[1 - slot], sem.at[1 - slot])

    cur = buf.at[slot]
    _wait_row_gather(tm * TOP_K, y_hbm, cur, sem.at[slot])
    gate = gate_ref[...]
    f = gate[:, 0:1] * _from_slabs(cur, 0, tm, 0, TOP_K * ROW_SLABS)
    for k in range(1, TOP_K):
        f += gate[:, k:k + 1] * _from_slabs(cur, 0, tm, k * ROW_SLABS, TOP_K * ROW_SLABS)
    out = _layer_norm(alpha * _from_slabs(x1_ref, 0, tm) + f, g_ref[...], b_ref[...])

    @pl.when(t < n_a)
    def _():
        oa_ref[...] = out

    @pl.when(t >= n_a)
    def _():
        ob_ref[...] = out


def moe_combine_norm(y_slabs, pos, gate, x1_slabs, ln_g, ln_b, *, alpha, n_first):
    n_tok = pos.shape[0]
    tm = COMBINE_TOKENS
    d = ln_g.shape[1]
    n_t = n_tok // tm
    n_a = n_first // tm
    vec = pl.BlockSpec((1, d), lambda t: (0, 0))
    idx = pos.reshape(n_t, 1, tm * TOP_K)
    return pl.pallas_call(
        functools.partial(_combine_kernel, alpha=alpha, n_a=n_a),
        grid=(n_t,),
        in_specs=[pl.BlockSpec((1, 1, tm * TOP_K), lambda t: (t, 0, 0), memory_space=pltpu.SMEM),
                  pl.BlockSpec((1, 1, tm * TOP_K), lambda t: (jnp.minimum(t + 1, n_t - 1), 0, 0),
                               memory_space=pltpu.SMEM),
                  pl.BlockSpec(memory_space=pl.ANY),
                  pl.BlockSpec((tm * ROW_SLABS, LANES), lambda t: (t, 0)),
                  pl.BlockSpec((tm, TOP_K), lambda t: (t, 0)), vec, vec],
        out_specs=[pl.BlockSpec((tm, d), lambda t: (jnp.minimum(t, n_a - 1), 0)),
                   pl.BlockSpec((tm, d), lambda t: (jnp.maximum(t - n_a, 0), 0))],
        out_shape=[jax.ShapeDtypeStruct((n_first, d), F32), jax.ShapeDtypeStruct((n_tok - n_first, d), F32)],
        scratch_shapes=[pltpu.VMEM((2, tm * TOP_K * ROW_SLABS, LANES), F32), pltpu.SemaphoreType.DMA((2,))],
        compiler_params=_params("arbitrary"),
        name="moe_combine",
    )(idx, idx, y_slabs, x1_slabs, gate, ln_g, ln_b)


def _route(logits, n_items):
    n_tok = logits.shape[0]
    n_assign = n_tok * TOP_K
    tiles_per_item = MOE_GROUP // MOE_ROW_TILE
    top_logit, top_idx = lax.top_k(logits, TOP_K)
    gate = jax.nn.softmax(top_logit, axis=-1)
    flat_e = top_idx.reshape(-1).astype(jnp.int32)
    order = jnp.argsort(flat_e).astype(jnp.int32)
    rank = jnp.argsort(order).astype(jnp.int32)
    experts = jnp.arange(N_EXPERTS, dtype=jnp.int32)
    onehot = flat_e[:, None] == experts[None, :]
    counts = jnp.sum(onehot, axis=0, dtype=jnp.int32)
    padded = (counts + MOE_GROUP - 1) // MOE_GROUP * MOE_GROUP
    start = jnp.cumsum(counts) - counts
    padded_end = jnp.cumsum(padded)
    padded_start = padded_end - padded

    def expert_of(row0):
        return jnp.minimum(jnp.sum(row0[:, None] >= padded_end[None, :], axis=1, dtype=jnp.int32), N_EXPERTS - 1)

    def table(values, e):
        return jnp.sum(jnp.where(e[:, None] == experts[None, :], values[None, :], 0), axis=1)

    n_real = (padded_end[-1] // MOE_GROUP).astype(jnp.int32)
    item = jnp.minimum(jnp.arange(n_items, dtype=jnp.int32), jnp.maximum(n_real - 1, 0))
    item_expert = expert_of(item * MOE_GROUP)
    filled = jnp.clip(table(padded_start + counts, item_expert) - item * MOE_GROUP, 0, MOE_GROUP)
    item_rows = jnp.where(jnp.arange(n_items) < n_real, filled, 0).astype(jnp.int32)
    item_first = jnp.clip(table(start - padded_start, item_expert) + item * MOE_GROUP, 0, n_assign - 1)
    tile_row0 = (jnp.arange(tiles_per_item, dtype=jnp.int32) * MOE_ROW_TILE)[None, :]
    tile_valid = jnp.clip(item_rows[:, None] - tile_row0, 0, MOE_ROW_TILE).reshape(-1).astype(jnp.int32)
    tile_first = jnp.clip(item_first[:, None] + tile_row0, 0, n_assign - 1).reshape(-1).astype(jnp.int32)
    return (gate, order, rank.reshape(n_tok, TOP_K), item_expert, item.astype(jnp.int32), item_rows,
            item_first.astype(jnp.int32), tile_valid, tile_first)


def _pad_rw_cols(a, rw_cols, pad):
    lead = a.shape[:-1]
    return jnp.concatenate([a[..., :rw_cols], jnp.zeros(lead + (pad,), a.dtype), a[..., rw_cols:]], axis=-1)


def _cache_rows(c, heads, head_dim):
    b, n_mem = c.shape[:2]
    tiles = head_dim // LANES
    c = c.reshape(b, n_mem, heads, tiles, LANES).transpose(0, 1, 3, 2, 4)
    return c.reshape(b, n_mem * tiles * heads, LANES)


def _layer(xp, xs, mem_prompt, cache_k, cache_v, st_rwkv, st_shift, st_conv, wts, depth):
    (w_in, w_w_up, w0, w_a_up, a0, w_g_up, tshift_mu, k_k, k_a, r_k, gn_g, gn_b, conv_w,
     w_mem_k, w_mem_v, w_proj_a, w_proj_b, w_proj_m, w_o, ln1_g, ln1_b,
     router_w, router_b, w_gate_up, b_gate_up, w_down, b_down, ln2_g, ln2_b) = wts
    bp, sp, d = xp.shape
    bs, ss, _ = xs.shape
    n_mem = mem_prompt.shape[1]
    mem_heads, mem_head = cache_k.shape[-2:]
    n_p, n_s = bp * sp, bs * ss
    n_tok = n_p + n_s
    alpha = (2.0 * depth) ** 0.25
    rw_cols = 3 * d + DECAY_LORA + AICL_LORA + GATE_LORA
    pad = LORA_PAD - (rw_cols - 3 * d)
    rw_pad = rw_cols + pad

    x_all = jnp.concatenate([xp.reshape(n_p, d), xs.reshape(n_s, d)], axis=0)
    x_bf = _b16(x_all)
    mu_pad = _pad_rw_cols(tshift_mu, rw_cols, pad)[None, :rw_pad]
    w_rw = _b16(w_in[:, :rw_pad])
    w_rest = _b16(w_in[:, rw_cols:])

    p_rw = matmul(x_bf, w_rw, F32, 1024, 512, "in_proj_rw")
    p2 = matmul(x_bf, w_rest, BF16, 1024, 512, "in_proj_rest")
    prev_s = matmul(_b16(st_shift), w_rw, F32, bs, 512, "prev_proj").reshape(bs, 1, rw_pad)
    prev_p = jnp.zeros((bp, 1, rw_pad), F32)

    def lora_rows(w, row0):
        return _b16(jnp.zeros((LORA_PAD, d), F32).at[row0:row0 + w.shape[0]].set(w))

    ww = lora_rows(w_w_up, 0)
    wa = lora_rows(w_a_up, DECAY_LORA)
    wg = lora_rows(w_g_up, DECAY_LORA + AICL_LORA)
    r2 = lambda v: v.reshape(1, d)
    rw_vecs = (mu_pad, r2(w0), r2(a0), r2(k_k), r2(k_a), r2(r_k), r2(gn_g), r2(gn_b), ww, wa, wg)
    heads = d // RW_HEAD
    ya_p, rw_p = rwkv_time_mix(p_rw, prev_p, jnp.zeros((bp, heads, RW_HEAD, RW_HEAD), F32), *rw_vecs,
                               batch=bp, seq=sp, row_block0=0, pairs=8, has_state=False)
    ya_s, rw_s = rwkv_time_mix(p_rw, prev_s, st_rwkv, *rw_vecs,
                               batch=bs, seq=ss, row_block0=n_p // RW_ROWS, pairs=2, has_state=True)
    ya = jnp.concatenate([ya_p, ya_s], axis=0)

    yb_p, cv_p = conv_sequences(p2, conv_w, batch=bp, seq=sp, width=d, rows=256, tn=512)
    yb_s, cv_s0, cv_s1 = conv_short(p2, st_conv[:, 0, :], st_conv[:, 1, :], conv_w, row_block0=n_p // 128,
                                    batch=bs, seq=ss, width=d, rows=128, tn=512)
    yb = jnp.concatenate([yb_p, yb_s], axis=0)
    cv_s = jnp.stack([cv_s0, cv_s1], axis=1)

    mem_in = _b16(mem_prompt.reshape(bp * n_mem, d))
    mk = matmul(mem_in, _b16(w_mem_k), F32, bp * n_mem, 512, "mem_k")
    mv = matmul(mem_in, _b16(w_mem_v), F32, bp * n_mem, 512, "mem_v")
    q_col0 = 3 * d // mem_head
    ym_p = attention_sequences(p2, q_col0, mk, mv, batch=bp, seq=sp, n_mem=n_mem, heads=mem_heads,
                               head_dim=mem_head, tq=512)
    bb = 2
    ym_s = attention_cache(p2, 3 * d // d, n_p // (bb * ss), _cache_rows(cache_k, mem_heads, mem_head),
                           _cache_rows(cache_v, mem_heads, mem_head),
                           batch=bs, seq=ss, heads=mem_heads, head_dim=mem_head, bb=bb)
    ym = jnp.concatenate([ym_p, ym_s], axis=0)

    merged = merge_branches(ya, yb, ym, p2, 4 * d // 512, _b16(w_proj_a), _b16(w_proj_b),
                            _b16(w_proj_m), tm=512, tn=512)
    rw_pad_r = jnp.zeros((d, LANES), F32).at[:, :N_EXPERTS].set(router_w)
    rb_pad_r = jnp.zeros((1, LANES), F32).at[0, :N_EXPERTS].set(router_b)
    x1_slabs, logits = project_norm_route(merged, x_all, _b16(w_o), r2(ln1_g), r2(ln1_b),
                                          rw_pad_r, rb_pad_r, alpha=alpha, tm=256)

    n_items = n_tok * TOP_K // MOE_GROUP + N_EXPERTS
    (gate, order, rank, item_expert, item_block, item_rows, item_first, tile_valid,
     tile_first) = _route(logits[:, :N_EXPERTS], n_items)
    x_sorted = moe_dispatch(x1_slabs, order, tile_valid, tile_first, d)
    y_slabs = moe_experts(x_sorted, item_expert, item_block, item_rows, item_first, n_tok * TOP_K,
                          w_gate_up, b_gate_up, w_down, b_down)
    y_p, y_s = moe_combine_norm(y_slabs, rank, gate, x1_slabs, r2(ln2_g), r2(ln2_b), alpha=alpha, n_first=n_p)
    y_p = y_p.reshape(bp, sp, d)
    y_s = y_s.reshape(bs, ss, d)
    mk5 = mk.reshape(bp, n_mem, mem_heads, mem_head)
    mv5 = mv.reshape(bp, n_mem, mem_heads, mem_head)
    return y_p, y_s, mk5, mv5, rw_p, xp[:, -1], cv_p, rw_s, xs[:, -1], cv_s


def kernel(x_prompt, x_sample, mem_prompt, cache_mem_k, cache_mem_v, state_rwkv, state_shift, state_conv, w_in, w_w_up, w0, w_a_up, a0, w_g_up, tshift_mu, k_k, k_a, r_k, gn_g, gn_b, conv_w, w_mem_k, w_mem_v, w_proj_a, w_proj_b, w_proj_m, w_o, ln1_g, ln1_b, router_w, router_b, w_gate_up, b_gate_up, w_down, b_down, ln2_g, ln2_b):
    weights = (w_in, w_w_up, w0, w_a_up, a0, w_g_up, tshift_mu, k_k, k_a, r_k, gn_g, gn_b, conv_w,
               w_mem_k, w_mem_v, w_proj_a, w_proj_b, w_proj_m, w_o, ln1_g, ln1_b,
               router_w, router_b, w_gate_up, b_gate_up, w_down, b_down, ln2_g, ln2_b)
    depth = w_in.shape[0]
    yp, ys = x_prompt, x_sample
    outs = [[] for _ in range(8)]
    for l in range(depth):
        res = _layer(yp, ys, mem_prompt, cache_mem_k[l], cache_mem_v[l], state_rwkv[l], state_shift[l],
                     state_conv[l], tuple(w[l] for w in weights), depth)
        yp, ys = res[0], res[1]
        for acc, r in zip(outs, res[2:]):
            acc.append(r)
    return (yp, ys) + tuple(jnp.stack(o) for o in outs)
```

```python
import functools
import math

import jax
import jax.numpy as jnp
from jax import lax
from jax.experimental import pallas as pl
from jax.experimental.pallas import tpu as pltpu

F32 = jnp.float32
BF16 = jnp.bfloat16

LANES = 128
ROW_SLABS = 16
RW_HEAD = 64
HEADS_PER_BLOCK = LANES // RW_HEAD
RW_ROWS = 64
DECAY_LORA = 96
AICL_LORA = 96
GATE_LORA = 256
LORA_PAD = 512
DECAY_SCALE = math.exp(-0.5)
GN_EPS = 64e-5
LN_EPS = 1e-5
N_EXPERTS = 32
TOP_K = 4
SWIGLU_LIMIT = 7.0
SWIGLU_ALPHA = 1.702
VMEM_LIMIT = 56 * 1024 * 1024


def _dot(a, b):
    return jnp.dot(a, b, preferred_element_type=F32)


def _dot_nt(a, b):
    return lax.dot_general(a, b, (((1,), (1,)), ((), ())), preferred_element_type=F32)


def _dot_tn(a, b):
    return lax.dot_general(a, b, (((0,), (0,)), ((), ())), preferred_element_type=F32)


def _b16(x):
    return x.astype(BF16)


def _params(*sem):
    return pltpu.CompilerParams(dimension_semantics=sem, vmem_limit_bytes=VMEM_LIMIT)


def _split3(x):
    hi = _b16(x)
    r1 = x - hi.astype(F32)
    mid = _b16(r1)
    lo = _b16(r1 - mid.astype(F32))
    return hi, mid, lo


def _rwkv_kernel(pr_ref, pk_ref, pv_ref, pl_ref, qr_ref, qk_ref, qv_ref, ql_ref,
                 mur_ref, muk_ref, muv_ref, mul_ref,
                 w0_ref, a0_ref, kk_ref, ka_ref, rk_ref, gng_ref, gnb_ref,
                 ww_ref, wa_ref, wg_ref, s0_ref,
                 y_ref, sout_ref, *scratch, n_seq, n_chunks, pairs, has_state):
    C = RW_ROWS
    L = C // n_seq
    R = HEADS_PER_BLOCK * C
    W = pairs * LANES
    c = pl.program_id(2)
    carried = n_chunks > 1
    if carried:
        s_sc, cr_sc, ck_sc, cv_sc, cl_sc = scratch

        @pl.when(c == 0)
        def _init_state():
            for p in range(pairs):
                if has_state:
                    s_sc[p] = s0_ref[0, 2 * p:2 * p + 2].reshape(LANES, RW_HEAD)
                else:
                    s_sc[p] = jnp.zeros((LANES, RW_HEAD), F32)

    def prev_rows(q_ref, carry_sc, width):
        if carried:
            @pl.when(c == 0)
            def _():
                carry_sc[0:1, :] = q_ref[0]
            return jnp.broadcast_to(carry_sc[0:1, :], (C, width))
        q = q_ref[...]
        return jnp.broadcast_to(q, (n_seq, L, width)).reshape(C, width)

    def shifted_lerp(p_ref, q_ref, carry_sc, mu_ref, width):
        p = p_ref[...]
        pos = lax.broadcasted_iota(jnp.int32, (C, width), 0) % L
        prev = jnp.where(pos == 0, prev_rows(q_ref, carry_sc, width), pltpu.roll(p, 1, axis=0))
        if carried:
            carry_sc[0:1, :] = p[C - 1:C, :]
        return p + (prev - p) * mu_ref[...]

    zr = shifted_lerp(pr_ref, qr_ref, cr_sc if carried else None, mur_ref, W)
    zk = shifted_lerp(pk_ref, qk_ref, ck_sc if carried else None, muk_ref, W)
    zv = shifted_lerp(pv_ref, qv_ref, cv_sc if carried else None, muv_ref, W)
    zl = shifted_lerp(pl_ref, ql_ref, cl_sc if carried else None, mul_ref, LORA_PAD)

    lw = -DECAY_SCALE * jax.nn.sigmoid(w0_ref[...] + _dot(_b16(jnp.tanh(zl)), ww_ref[...]))
    a_all = jax.nn.sigmoid(a0_ref[...] + _dot(_b16(zl), wa_ref[...]))
    g_all = _dot(_b16(jax.nn.sigmoid(zl)), wg_ref[...])

    ti = lax.broadcasted_iota(jnp.int32, (2 * C, C), 0)
    tj = lax.broadcasted_iota(jnp.int32, (2 * C, C), 1)
    same_seq = (ti % C) // L == tj // L
    cum_lhs = _b16((same_seq & ((ti >= C) | (tj <= ti))).astype(F32))
    hi, mid, lo = _split3(lw)
    cum = _dot(cum_lhs, jnp.concatenate([hi, mid, lo], axis=1))
    cum = cum[:, 0:W] + cum[:, W:2 * W] + cum[:, 2 * W:3 * W]
    cw_all = cum[0:C]
    tot_all = cum[C:2 * C]

    lb_r = lax.broadcasted_iota(jnp.int32, (2 * LANES, LANES), 0) % LANES // RW_HEAD
    lb_c = lax.broadcasted_iota(jnp.int32, (2 * LANES, LANES), 1) // RW_HEAD
    head_ones2 = _b16((lb_r == lb_c).astype(F32))

    def head_sum(x):
        xh = _b16(x)
        xl = _b16(x - xh.astype(F32))
        return _dot(jnp.concatenate([xh, xl], axis=1), head_ones2)

    head0 = lax.broadcasted_iota(jnp.int32, (n_seq, L, LANES), 2) < RW_HEAD

    def stack(x):
        x3 = x.reshape(n_seq, L, LANES)
        return jnp.concatenate([jnp.where(head0, x3, 0.0), jnp.where(head0, 0.0, x3)], axis=1).reshape(R, LANES)

    def unstack(x):
        x3 = x.reshape(n_seq, 2 * L, LANES)
        return (x3[:, 0:L, :] + x3[:, L:2 * L, :]).reshape(C, LANES)

    si = lax.broadcasted_iota(jnp.int32, (R, R), 0)
    sj = lax.broadcasted_iota(jnp.int32, (R, R), 1)
    same = (si // L) == (sj // L)
    strict = same & (si > sj)
    incl = same & (si >= sj)
    eye = (si == sj).astype(F32)
    row_head = (lax.broadcasted_iota(jnp.int32, (R, LANES), 0) // L) % HEADS_PER_BLOCK
    lane_head = lax.broadcasted_iota(jnp.int32, (R, LANES), 1) // RW_HEAD
    head_match = row_head == lane_head
    state_row_head0 = lax.broadcasted_iota(jnp.int32, (LANES, LANES), 0) < RW_HEAD
    n_lvl = int(math.log2(L)) - 1
    S2 = 2 * L

    P = range(pairs)
    lsl = [slice(p * LANES, (p + 1) * LANES) for p in P]

    def rows_cat(xs):
        return jnp.concatenate(xs, axis=0) if len(xs) > 1 else xs[0]

    kk_raw = [zk[:, ls] * kk_ref[:, ls] for ls in lsl]
    kk_n2 = head_sum(rows_cat([x * x for x in kk_raw]))
    kk_l = [kk_raw[p] / jnp.maximum(jnp.sqrt(kk_n2[p * C:(p + 1) * C]), 1e-12) for p in P]
    kmod_l = [zk[:, ls] * (1.0 + (a_all[:, ls] - 1.0) * ka_ref[:, ls]) for ls in lsl]
    beta_l = [kk_l[p] * a_all[:, lsl[p]] for p in P]
    e_neg_l = [jnp.exp(-cw_all[:, ls]) for ls in lsl]
    e_tail_l = [jnp.exp(tot_all[:, ls] - cw_all[:, ls]) for ls in lsl]
    kk_s_l = [stack(kk_l[p] * jnp.exp(cw_all[:, lsl[p]] - lw[:, lsl[p]])) for p in P]
    r_s_l = [stack(zr[:, ls] * jnp.exp(cw_all[:, ls])) for ls in lsl]
    v_b_l = [_b16(stack(zv[:, ls])) for ls in lsl]
    bw_b_l = [_b16(stack(beta_l[p] * e_tail_l[p])) for p in P]
    kw_b_l = [_b16(stack(kmod_l[p] * e_tail_l[p])) for p in P]
    a_lhs = [_b16(jnp.concatenate([kk_s_l[p], r_s_l[p]], axis=0)) for p in P]
    a_rhs = [_b16(jnp.concatenate([stack(beta_l[p] * e_neg_l[p]), stack(kmod_l[p] * e_neg_l[p])], axis=0)) for p in P]

    amat_l = [_dot_nt(a_lhs[p], a_rhs[p]) for p in P]
    a_ak_b = [_b16(jnp.where(strict, m[0:R, R:2 * R], 0.0)) for m in amat_l]
    a_r_b = [_b16(jnp.concatenate([jnp.where(incl, m[R:2 * R, 0:R], 0.0),
                                    jnp.where(incl, m[R:2 * R, R:2 * R], 0.0)], axis=1)) for m in amat_l]

    x_l = [jnp.where(strict, -m[0:R, 0:R], 0.0) for m in amat_l]
    minv_l = [eye + x for x in x_l]
    xb_l = [_b16(x) for x in x_l]
    cur_l = [_dot(xb, xb) for xb in xb_l]
    akv_l = [_dot(a_ak_b[p], v_b_l[p]) for p in P]
    for lvl in range(n_lvl):
        cb_l = [_b16(cur) for cur in cur_l]
        if lvl < n_lvl - 1:
            both_l = [_dot(_b16(jnp.concatenate([minv_l[p], cur_l[p]], axis=0)), cb_l[p]) for p in P]
            minv_l = [minv_l[p] + both_l[p][0:R] for p in P]
            cur_l = [both[R:2 * R] for both in both_l]
        else:
            minv_l = [minv_l[p] + _dot(_b16(minv_l[p]), cb_l[p]) for p in P]

    pq_l = [_dot(_b16(minv_l[p]), _b16(jnp.concatenate([kk_s_l[p], akv_l[p]], axis=1))) for p in P]

    def fold_lanes(x):
        return _b16(x + pltpu.roll(x, RW_HEAD, axis=1))[:, 0:RW_HEAD]

    p_fold = [fold_lanes(-pq[:, 0:LANES]) for pq in pq_l]
    r_fold = [fold_lanes(r_s) for r_s in r_s_l]
    seqs = [(p, b) for p in P for b in range(n_seq)]

    def state_in(p, b):
        if carried:
            return s_sc[p]
        return s0_ref[b, 2 * p:2 * p + 2].reshape(LANES, RW_HEAD)

    states = {pb: state_in(*pb) for pb in seqs}
    sdot = {(p, b): _dot_nt(jnp.concatenate([p_fold[p][b * S2:(b + 1) * S2], r_fold[p][b * S2:(b + 1) * S2]], axis=0),
                            _b16(states[(p, b)])) for (p, b) in seqs}
    u_b_l, rs_l = [], []
    for p in P:
        us = [jnp.where(head_match[b * S2:(b + 1) * S2], sdot[(p, b)][0:S2], 0.0) for b in range(n_seq)]
        rs = [jnp.where(head_match[b * S2:(b + 1) * S2], sdot[(p, b)][S2:2 * S2], 0.0) for b in range(n_seq)]
        u_b_l.append(_b16(rows_cat(us) - pq_l[p][:, LANES:2 * LANES]))
        rs_l.append(rows_cat(rs))

    y_s_l = [rs_l[p] + _dot(a_r_b[p], jnp.concatenate([u_b_l[p], v_b_l[p]], axis=0)) for p in P]
    z_l = {(p, b): _dot_tn(jnp.concatenate([u_b_l[p][b * S2:(b + 1) * S2], v_b_l[p][b * S2:(b + 1) * S2]], axis=0),
                           jnp.concatenate([bw_b_l[p][b * S2:(b + 1) * S2], kw_b_l[p][b * S2:(b + 1) * S2]], axis=0))
           for (p, b) in seqs}
    for (p, b) in seqs:
        z = z_l[(p, b)]
        z = (z + pltpu.roll(z, RW_HEAD, axis=1))[:, 0:RW_HEAD]
        wt = jnp.broadcast_to(jnp.exp(tot_all[b * L:b * L + 1, lsl[p]]), (LANES, LANES))
        wnat = jnp.where(state_row_head0, wt, pltpu.roll(wt, RW_HEAD, axis=1))[:, 0:RW_HEAD]
        s_new = states[(p, b)] * wnat + z
        if carried:
            s_sc[p] = s_new
        sout_ref[b, 2 * p:2 * p + 2] = s_new.reshape(HEADS_PER_BLOCK, RW_HEAD, RW_HEAD)

    y_l = [unstack(y_s) for y_s in y_s_l]
    stats = head_sum(rows_cat(y_l + [zr[:, ls] * kmod_l[p] * rk_ref[:, ls] for p, ls in enumerate(lsl)]))
    d_l = [y_l[p] - stats[p * C:(p + 1) * C] * (1.0 / RW_HEAD) for p in P]
    var = head_sum(rows_cat([d * d for d in d_l])) * (1.0 / RW_HEAD)
    for p, ls in enumerate(lsl):
        yn = d_l[p] * lax.rsqrt(var[p * C:(p + 1) * C] + GN_EPS) * gng_ref[:, ls] + gnb_ref[:, ls]
        bonus = stats[(pairs + p) * C:(pairs + p + 1) * C] * zv[:, ls]
        y_ref[:, ls] = ((yn + bonus) * g_all[:, ls]).astype(y_ref.dtype)


def rwkv_time_mix(p_rw, prev_rw, s0, mu, w0, a0, k_k, k_a, r_k, gn_g, gn_b, ww, wa, wg,
                  *, batch, seq, row_block0, pairs, has_state):
    width = w0.shape[-1]
    wp = pairs * LANES
    n_blk = width // wp
    heads = width // RW_HEAD
    hp = HEADS_PER_BLOCK * pairs
    lora_blk = 3 * width // LORA_PAD
    if seq >= RW_ROWS:
        n_seq, n_chunks, n_outer = 1, seq // RW_ROWS, batch
    else:
        n_seq, n_chunks, n_outer = RW_ROWS // seq, 1, batch * seq // RW_ROWS

    def tok(col0):
        return pl.BlockSpec((RW_ROWS, wp), lambda b, h, c: (row_block0 + b * n_chunks + c, col0 + h))

    def prev(col0):
        return pl.BlockSpec((n_seq, 1, wp), lambda b, h, c: (b, 0, col0 + h))

    def vec(col0):
        return pl.BlockSpec((1, wp), lambda b, h, c: (0, col0 + h))

    lora_w = pl.BlockSpec((LORA_PAD, wp), lambda b, h, c: (0, h))
    state = pl.BlockSpec((n_seq, hp, RW_HEAD, RW_HEAD), lambda b, h, c: (b, h, 0, 0))
    in_specs = [
        tok(0), tok(n_blk), tok(2 * n_blk),
        pl.BlockSpec((RW_ROWS, LORA_PAD), lambda b, h, c: (row_block0 + b * n_chunks + c, lora_blk)),
        prev(0), prev(n_blk), prev(2 * n_blk),
        pl.BlockSpec((n_seq, 1, LORA_PAD), lambda b, h, c: (b, 0, lora_blk)),
        vec(0), vec(n_blk), vec(2 * n_blk),
        pl.BlockSpec((1, LORA_PAD), lambda b, h, c: (0, lora_blk)),
        vec(0), vec(0), vec(0), vec(0), vec(0), vec(0), vec(0),
        lora_w, lora_w, lora_w, state,
    ]
    out_specs = [pl.BlockSpec((RW_ROWS, wp), lambda b, h, c: (b * n_chunks + c, h)), state]
    scratch = []
    if n_chunks > 1:
        scratch = [pltpu.VMEM((pairs, LANES, RW_HEAD), F32), pltpu.VMEM((8, wp), F32), pltpu.VMEM((8, wp), F32),
                   pltpu.VMEM((8, wp), F32), pltpu.VMEM((8, LORA_PAD), F32)]
    kern = functools.partial(_rwkv_kernel, n_seq=n_seq, n_chunks=n_chunks, pairs=pairs, has_state=has_state)
    return pl.pallas_call(
        kern,
        grid=(n_outer, n_blk, n_chunks),
        in_specs=in_specs,
        out_specs=out_specs,
        out_shape=[jax.ShapeDtypeStruct((batch * seq, width), BF16),
                   jax.ShapeDtypeStruct((batch, heads, RW_HEAD, RW_HEAD), F32)],
        scratch_shapes=scratch,
        compiler_params=_params("parallel", "parallel", "arbitrary"),
        name=f"rwkv_l{min(seq, RW_ROWS)}",
    )(p_rw, p_rw, p_rw, p_rw, prev_rw, prev_rw, prev_rw, prev_rw, mu, mu, mu, mu,
      w0, a0, k_k, k_a, r_k, gn_g, gn_b, ww, wa, wg, s0)


def _mm_kernel(x_ref, w_ref, o_ref):
    o_ref[...] = _dot(_b16(x_ref[...]), w_ref[...]).astype(o_ref.dtype)


def matmul(x, w, out_dtype, tm, tn, name):
    m, kd = x.shape
    n = w.shape[1]
    return pl.pallas_call(
        _mm_kernel,
        grid=(n // tn, m // tm),
        in_specs=[pl.BlockSpec((tm, kd), lambda j, i: (i, 0)),
                  pl.BlockSpec((kd, tn), lambda j, i: (0, j))],
        out_specs=pl.BlockSpec((tm, tn), lambda j, i: (i, j)),
        out_shape=jax.ShapeDtypeStruct((m, n), out_dtype),
        compiler_params=_params("parallel", "parallel"),
        name=name,
    )(x, w)


def _mm_cast_kernel(x_ref, w_ref, o_ref, w_sc):
    @pl.when(pl.program_id(1) == 0)
    def _():
        w_sc[...] = _b16(w_ref[...])

    o_ref[...] = _dot(_b16(x_ref[...]), w_sc[...]).astype(o_ref.dtype)


def matmul_f32_weights(x, w, n_cols, out_dtype, tm, tn, name):
    m, kd = x.shape
    return pl.pallas_call(
        _mm_cast_kernel,
        grid=(n_cols // tn, m // tm),
        in_specs=[pl.BlockSpec((tm, kd), lambda j, i: (i, 0)),
                  pl.BlockSpec((kd, tn), lambda j, i: (0, j))],
        out_specs=pl.BlockSpec((tm, tn), lambda j, i: (i, j)),
        out_shape=jax.ShapeDtypeStruct((m, n_cols), out_dtype),
        scratch_shapes=[pltpu.VMEM((kd, tn), BF16)],
        compiler_params=_params("parallel", "arbitrary"),
        name=name,
    )(x, w)


def _conv_taps(u, u1, u2, bg_ref, cw_ref, y_ref):
    y = cw_ref[0:1, :] * u2 + cw_ref[1:2, :] * u1 + cw_ref[2:3, :] * u
    y_ref[...] = (bg_ref[...].astype(F32) * y).astype(y_ref.dtype)


def _conv_seq_kernel(bg_ref, cg_ref, xi_ref, cw_ref, y_ref, new_ref, carry_sc, *, rows, n_t):
    i = pl.program_id(2)

    @pl.when(i == 0)
    def _init():
        carry_sc[...] = jnp.zeros(carry_sc.shape, F32)

    u = cg_ref[...].astype(F32) * xi_ref[...].astype(F32)
    r = lax.broadcasted_iota(jnp.int32, u.shape, 0)
    c0 = carry_sc[0:1, :]
    c1 = carry_sc[1:2, :]
    u1 = jnp.where(r == 0, c1, pltpu.roll(u, 1, axis=0))
    u2 = jnp.where(r == 0, c0, jnp.where(r == 1, c1, pltpu.roll(u, 2, axis=0)))
    _conv_taps(u, u1, u2, bg_ref, cw_ref, y_ref)
    carry_sc[0:2, :] = u[rows - 2:rows, :]

    @pl.when(i == n_t - 1)
    def _fin():
        new_ref[0] = u[rows - 2:rows, :]


def conv_sequences(p2, conv_w, *, batch, seq, width, rows, tn):
    nb = width // tn
    n_t = seq // rows

    def tok(sec):
        return pl.BlockSpec((rows, tn), lambda b, j, i: (b * n_t + i, sec * nb + j))

    return pl.pallas_call(
        functools.partial(_conv_seq_kernel, rows=rows, n_t=n_t),
        grid=(batch, nb, n_t),
        in_specs=[tok(0), tok(1), tok(2), pl.BlockSpec((3, tn), lambda b, j, i: (0, j))],
        out_specs=[pl.BlockSpec((rows, tn), lambda b, j, i: (b * n_t + i, j)),
                   pl.BlockSpec((1, 2, tn), lambda b, j, i: (b, 0, j))],
        out_shape=[jax.ShapeDtypeStruct((batch * seq, width), BF16),
                   jax.ShapeDtypeStruct((batch, 2, width), F32)],
        scratch_shapes=[pltpu.VMEM((8, tn), F32)],
        compiler_params=_params("parallel", "parallel", "arbitrary"),
        name="conv_seq",
    )(p2, p2, p2, conv_w)


def _conv_short_kernel(bg_ref, cg_ref, xi_ref, p0_ref, p1_ref, cw_ref, y_ref, n0_ref, n1_ref, *, seq):
    u = cg_ref[...].astype(F32) * xi_ref[...].astype(F32)
    rows, tn = u.shape
    n_seq = rows // seq

    def per_seq(p_ref):
        return jnp.broadcast_to(p_ref[...].reshape(n_seq, 1, tn), (n_seq, seq, tn)).reshape(rows, tn)

    h0, h1 = per_seq(p0_ref), per_seq(p1_ref)
    pos = lax.broadcasted_iota(jnp.int32, u.shape, 0) % seq
    u1 = jnp.where(pos == 0, h1, pltpu.roll(u, 1, axis=0))
    u2 = jnp.where(pos == 0, h0, jnp.where(pos == 1, h1, pltpu.roll(u, 2, axis=0)))
    _conv_taps(u, u1, u2, bg_ref, cw_ref, y_ref)
    u3 = u.reshape(n_seq, seq, tn)
    n0_ref[...] = u3[:, seq - 2, :]
    n1_ref[...] = u3[:, seq - 1, :]


def conv_short(p2, hist0, hist1, conv_w, *, row_block0, batch, seq, width, rows, tn):
    nb = width // tn
    n_seq = rows // seq
    n_r = batch * seq // rows

    def tok(sec):
        return pl.BlockSpec((rows, tn), lambda i, j: (row_block0 + i, sec * nb + j))

    per_seq = pl.BlockSpec((n_seq, tn), lambda i, j: (i, j))
    return pl.pallas_call(
        functools.partial(_conv_short_kernel, seq=seq),
        grid=(n_r, nb),
        in_specs=[tok(0), tok(1), tok(2), per_seq, per_seq, pl.BlockSpec((3, tn), lambda i, j: (0, j))],
        out_specs=[pl.BlockSpec((rows, tn), lambda i, j: (i, j)), per_seq, per_seq],
        out_shape=[jax.ShapeDtypeStruct((batch * seq, width), BF16),
                   jax.ShapeDtypeStruct((batch, width), F32), jax.ShapeDtypeStruct((batch, width), F32)],
        compiler_params=_params("parallel", "parallel"),
        name="conv_short",
    )(p2, p2, p2, hist0, hist1, conv_w)


def _softmax_rows(s):
    m = jnp.max(s, axis=-1, keepdims=True)
    e = jnp.exp(s - m)
    return e / jnp.sum(e, axis=-1, keepdims=True)


def _attn_seq_kernel(q_ref, k_ref, v_ref, o_ref, *, scale):
    s = _dot_nt(q_ref[...], _b16(k_ref[...])) * scale
    p = _softmax_rows(s)
    o_ref[...] = _dot(_b16(p), _b16(v_ref[...])).astype(o_ref.dtype)


def attention_sequences(p2, q_col0, mem_k, mem_v, *, batch, seq, n_mem, heads, head_dim, tq):
    n_t = seq // tq
    kv = pl.BlockSpec((n_mem, head_dim), lambda b, h, i: (b, h))
    return pl.pallas_call(
        functools.partial(_attn_seq_kernel, scale=head_dim ** -0.5),
        grid=(batch, heads, n_t),
        in_specs=[pl.BlockSpec((tq, head_dim), lambda b, h, i: (b * n_t + i, q_col0 + h)), kv, kv],
        out_specs=pl.BlockSpec((tq, head_dim), lambda b, h, i: (b * n_t + i, h)),
        out_shape=jax.ShapeDtypeStruct((batch * seq, heads * head_dim), BF16),
        compiler_params=_params("parallel", "parallel", "arbitrary"),
        name="attn_seq",
    )(p2, mem_k, mem_v)


def _attn_cache_kernel(q_ref, k_ref, v_ref, o_ref, *, scale, bb, seq, heads, head_dim):
    tiles = head_dim // LANES
    slabs = tiles * heads
    n_mem = k_ref.shape[1] // slabs
    q = q_ref[...].astype(F32)

    def head_slab(ref, b, h):
        return jnp.concatenate([ref[b, pl.ds(t * heads + h, n_mem, stride=slabs), :] for t in range(tiles)], axis=1)

    rows = []
    for b in range(bb):
        cols = []
        for h in range(heads):
            qb = _b16(q[b * seq:(b + 1) * seq, h * head_dim:(h + 1) * head_dim])
            s = _dot_nt(qb, _b16(head_slab(k_ref, b, h))) * scale
            p = _softmax_rows(s)
            cols.append(_dot(_b16(p), _b16(head_slab(v_ref, b, h))))
        rows.append(jnp.concatenate(cols, axis=1))
    o_ref[...] = jnp.concatenate(rows, axis=0).astype(o_ref.dtype)


def attention_cache(p2, q_blk0, row0, cache_k, cache_v, *, batch, seq, heads, head_dim, bb):
    d = heads * head_dim
    kv = pl.BlockSpec((bb,) + cache_k.shape[1:], lambda g: (g, 0, 0))
    return pl.pallas_call(
        functools.partial(_attn_cache_kernel, scale=head_dim ** -0.5, bb=bb, seq=seq, heads=heads, head_dim=head_dim),
        grid=(batch // bb,),
        in_specs=[pl.BlockSpec((bb * seq, d), lambda g: (row0 + g, q_blk0)), kv, kv],
        out_specs=pl.BlockSpec((bb * seq, d), lambda g: (g, 0)),
        out_shape=jax.ShapeDtypeStruct((batch * seq, d), BF16),
        compiler_params=_params("parallel"),
        name="attn_cache",
    )(p2, cache_k, cache_v)


def _merge_kernel(ya1_ref, yb1_ref, ym1_ref, ya2_ref, yb2_ref, ym2_ref, g0_ref, g1_ref, g2_ref,
                  wa_ref, wb_ref, wm_ref, o_ref, *, n_first):
    def merge(ya_ref, yb_ref, ym_ref):
        acc = jax.nn.sigmoid(g0_ref[...].astype(F32)) * _dot(_b16(ya_ref[...]), wa_ref[...])
        acc += jax.nn.sigmoid(g1_ref[...].astype(F32)) * _dot(_b16(yb_ref[...]), wb_ref[...])
        acc += jax.nn.sigmoid(g2_ref[...].astype(F32)) * _dot(_b16(ym_ref[...]), wm_ref[...])
        o_ref[...] = acc.astype(o_ref.dtype)

    i = pl.program_id(1)

    @pl.when(i < n_first)
    def _():
        merge(ya1_ref, yb1_ref, ym1_ref)

    @pl.when(i >= n_first)
    def _():
        merge(ya2_ref, yb2_ref, ym2_ref)


def merge_branches(first, second, p2, gate_col0, wa, wb, wm, *, tm, tn):
    m1, d = first[0].shape
    m2 = second[0].shape[0]
    n1 = m1 // tm
    nb = d // tn
    act1 = pl.BlockSpec((tm, d), lambda j, i: (jnp.minimum(i, n1 - 1), 0))
    act2 = pl.BlockSpec((tm, d), lambda j, i: (jnp.maximum(i - n1, 0), 0))
    wsp = pl.BlockSpec((d, tn), lambda j, i: (0, j))

    def gate(k):
        return pl.BlockSpec((tm, tn), lambda j, i: (i, gate_col0 + k * nb + j))

    return pl.pallas_call(
        functools.partial(_merge_kernel, n_first=n1),
        grid=(nb, (m1 + m2) // tm),
        in_specs=[act1, act1, act1, act2, act2, act2, gate(0), gate(1), gate(2), wsp, wsp, wsp],
        out_specs=pl.BlockSpec((tm, tn), lambda j, i: (i, j)),
        out_shape=jax.ShapeDtypeStruct((m1 + m2, d), BF16),
        compiler_params=_params("parallel", "parallel"),
        name="merge",
    )(*first, *second, p2, p2, p2, wa, wb, wm)


def _layer_norm(y, g, b):
    mu = jnp.mean(y, axis=-1, keepdims=True)
    d = y - mu
    var = jnp.mean(d * d, axis=-1, keepdims=True)
    return d * lax.rsqrt(var + LN_EPS) * g + b


def _to_slabs(ref, row0, x):
    rows = x.shape[0]
    for j in range(x.shape[1] // LANES):
        ref[pl.ds(row0 * ROW_SLABS + j, rows, stride=ROW_SLABS), :] = x[:, j * LANES:(j + 1) * LANES]


def _from_slabs(ref, row0, rows, offset=0, stride=ROW_SLABS):
    return jnp.concatenate(
        [ref[pl.ds(row0 * stride + offset + j, rows, stride=stride), :] for j in range(ROW_SLABS)], axis=1)


def _proj_ln_kernel(m_ref, xa_ref, xb_ref, wo_ref, g_ref, b_ref, rw_ref, rb_ref, x1_ref, lg_ref, *, alpha, n_first):
    h = _dot(m_ref[...], wo_ref[...])
    x = jnp.where(pl.program_id(0) < n_first, xa_ref[...], xb_ref[...])
    x1 = _layer_norm(alpha * x + h, g_ref[...], b_ref[...])
    _to_slabs(x1_ref, 0, x1)
    hi, mid, lo = _split3(x1)
    w_hi, w_mid, w_lo = _split3(rw_ref[...])
    lg = _dot(hi, w_hi) + (_dot(hi, w_mid) + _dot(mid, w_hi)) + (_dot(hi, w_lo) + _dot(mid, w_mid) + _dot(lo, w_hi))
    lg_ref[...] = lg + rb_ref[...]


def project_norm_route(merged, xa, xb, wo, ln_g, ln_b, router_w, router_b, *, alpha, tm):
    m, d = merged.shape
    n_r = router_w.shape[1]
    n1 = xa.shape[0] // tm
    row = pl.BlockSpec((tm, d), lambda i: (i, 0))
    vec = pl.BlockSpec((1, d), lambda i: (0, 0))
    return pl.pallas_call(
        functools.partial(_proj_ln_kernel, alpha=alpha, n_first=n1),
        grid=(m // tm,),
        in_specs=[row, pl.BlockSpec((tm, d), lambda i: (jnp.minimum(i, n1 - 1), 0)),
                  pl.BlockSpec((tm, d), lambda i: (jnp.maximum(i - n1, 0), 0)),
                  pl.BlockSpec((d, d), lambda i: (0, 0)), vec, vec,
                  pl.BlockSpec((d, n_r), lambda i: (0, 0)), pl.BlockSpec((1, n_r), lambda i: (0, 0))],
        out_specs=[pl.BlockSpec((tm * ROW_SLABS, LANES), lambda i: (i, 0)),
                   pl.BlockSpec((tm, n_r), lambda i: (i, 0))],
        out_shape=[jax.ShapeDtypeStruct((m * ROW_SLABS, LANES), F32), jax.ShapeDtypeStruct((m, n_r), F32)],
        compiler_params=_params("parallel"),
        name="proj_ln_route",
    )(merged, xa, xb, wo, ln_g, ln_b, router_w, router_b)


MOE_GROUP = 1280
MOE_ROW_TILE = 640
MOE_F_TILE = 256
MOE_N_TILE = 256
MOE_K_CHUNK = 256
COMBINE_TOKENS = 128


def _row_copy(src_hbm, src_row, dst_vmem, dst_row, sem):
    return pltpu.make_async_copy(
        src_hbm.at[pl.ds(pl.multiple_of(src_row * ROW_SLABS, ROW_SLABS), ROW_SLABS)],
        dst_vmem.at[pl.ds(pl.multiple_of(dst_row * ROW_SLABS, ROW_SLABS), ROW_SLABS)], sem)


def _start_row_gather(src_row, n_rows, src_hbm, buf, sem):
    def issue(r, carry):
        _row_copy(src_hbm, src_row(r), buf, r, sem).start()
        return carry

    lax.fori_loop(0, n_rows, issue, 0, unroll=8)


def _wait_row_gather(n_rows, src_hbm, buf, sem):
    pltpu.make_async_copy(src_hbm.at[pl.ds(0, n_rows * ROW_SLABS)], buf, sem).wait()


def _dispatch_kernel(tv_ref, tj_ref, tok_ref, x_hbm, o_ref, buf, sem):
    t = pl.program_id(0)
    n = pl.num_programs(0)
    slot = t % 2
    rt = MOE_ROW_TILE

    def start(tile, dst):
        j0 = tj_ref[tile]
        _start_row_gather(lambda r: tok_ref[j0 + r], rt, x_hbm, buf.at[dst], sem.at[dst])

    @pl.when((t == 0) & (tv_ref[0] > 0))
    def _first():
        start(0, 0)

    @pl.when((t + 1 < n) & (tv_ref[jnp.minimum(t + 1, n - 1)] > 0))
    def _next():
        start(jnp.minimum(t + 1, n - 1), 1 - slot)

    @pl.when(tv_ref[t] > 0)
    def _():
        _wait_row_gather(rt, x_hbm, buf.at[slot], sem.at[slot])
        o_ref[...] = _b16(_from_slabs(buf.at[slot], 0, rt))

    @pl.when(tv_ref[t] == 0)
    def _():
        o_ref[...] = jnp.zeros(o_ref.shape, o_ref.dtype)


def moe_dispatch(x_slabs, order, tile_valid, tile_first, d):
    n_tiles = tile_valid.shape[0]
    rt = MOE_ROW_TILE
    sorted_tok = jnp.concatenate([order // TOP_K, jnp.zeros((rt,), jnp.int32)])
    grid_spec = pltpu.PrefetchScalarGridSpec(
        num_scalar_prefetch=3,
        grid=(n_tiles,),
        in_specs=[pl.BlockSpec(memory_space=pl.ANY)],
        out_specs=pl.BlockSpec((rt, d), lambda t, tv, tj, od: (t, 0)),
        scratch_shapes=[pltpu.VMEM((2, rt * ROW_SLABS, LANES), F32), pltpu.SemaphoreType.DMA((2,))],
    )
    return pl.pallas_call(
        _dispatch_kernel,
        grid_spec=grid_spec,
        out_shape=jax.ShapeDtypeStruct((n_tiles * rt, d), BF16),
        compiler_params=_params("arbitrary"),
        name="moe_dispatch",
    )(tile_valid, tile_first, sorted_tok, x_slabs)


def _moe_kernel(ie_ref, ib_ref, nr_ref, j0_ref, x_ref, wg_ref, wu_ref, wd_ref, bg_ref, bu_ref, bd_ref,
                y_hbm, h_sc, wgu_sc, wd_sc, o_sc, pend_sc, sem, *, n_f, n_n, n_assign):
    i = pl.program_id(0)
    s = pl.program_id(1)
    rows = nr_ref[i]
    rt = MOE_ROW_TILE
    tile_slabs = rt * ROW_SLABS
    n_tiles = (rows + rt - 1) // rt
    tf = MOE_F_TILE
    last_step = n_f + n_n - 1

    def tile_copy(r, row0):
        return pltpu.make_async_copy(
            o_sc.at[pl.ds(pl.multiple_of(r * tile_slabs, tile_slabs), tile_slabs)],
            y_hbm.at[pl.ds(pl.multiple_of((row0 + r * rt) * ROW_SLABS, ROW_SLABS), tile_slabs)], sem)

    def wait_pending():
        def one(k, carry):
            tile_copy(0, 0).wait()
            return carry

        lax.fori_loop(0, pend_sc[0], one, 0)
        pend_sc[0] = 0

    @pl.when((i == 0) & (s == 0))
    def _reset():
        pend_sc[0] = 0

    @pl.when((rows > 0) & (s < n_f))
    def _hidden():
        def project(r):
            return _dot(x_ref[pl.ds(pl.multiple_of(r * rt, rt), rt), :], wgu_sc[...])

        def activate(h, r):
            hg = jnp.minimum(h[:, 0:tf] + bg_ref[0], SWIGLU_LIMIT)
            hu = jnp.clip(h[:, tf:2 * tf] + bu_ref[0], -SWIGLU_LIMIT, SWIGLU_LIMIT)
            h_sc[s, pl.ds(pl.multiple_of(r * rt, rt), rt), :] = _b16(hg * jax.nn.sigmoid(SWIGLU_ALPHA * hg) * (hu + 1.0))

        kc = MOE_K_CHUNK
        h0 = None
        for c in range(wg_ref.shape[1] // kc):
            ks = slice(c * kc, (c + 1) * kc)
            w_c = jnp.concatenate([_b16(wg_ref[0, ks, :]), _b16(wu_ref[0, ks, :])], axis=1)
            wgu_sc[ks, :] = w_c
            part = _dot(x_ref[0:rt, ks], w_c)
            h0 = part if h0 is None else h0 + part
        activate(h0, 0)

        def tile(r, carry):
            activate(project(r), r)
            return carry

        lax.fori_loop(1, n_tiles, tile, 0)

    @pl.when((rows > 0) & (s >= n_f))
    def _down():
        n = s - n_f

        @pl.when(s == n_f)
        def _():
            wait_pending()

        def project(r):
            sl = pl.ds(pl.multiple_of(r * rt, rt), rt)
            acc = _dot(h_sc[0, sl, :], wd_sc[0:tf, :])
            for f in range(1, n_f):
                acc += _dot(h_sc[f, sl, :], wd_sc[f * tf:(f + 1) * tf, :])
            return acc

        def emit(acc, r):
            val = acc + bd_ref[0]
            for jj in range(MOE_N_TILE // LANES):
                start = r * tile_slabs + n * (MOE_N_TILE // LANES) + jj
                o_sc[pl.ds(start, rt, stride=ROW_SLABS), :] = val[:, jj * LANES:(jj + 1) * LANES]

        acc0 = None
        for f in range(n_f):
            fs = slice(f * tf, (f + 1) * tf)
            w_f = _b16(wd_ref[0, fs, :])
            wd_sc[fs, :] = w_f
            part = _dot(h_sc[f, 0:rt, :], w_f)
            acc0 = part if acc0 is None else acc0 + part
        emit(acc0, 0)

        def tile(r, carry):
            emit(project(r), r)
            return carry

        lax.fori_loop(1, n_tiles, tile, 0)

        @pl.when(s == last_step)
        def _():
            row0 = j0_ref[i]

            def send(r, carry):
                tile_copy(r, row0).start()
                return carry

            lax.fori_loop(0, n_tiles, send, 0)
            pend_sc[0] = n_tiles

    @pl.when((i == pl.num_programs(0) - 1) & (s == last_step))
    def _finish():
        wait_pending()
        o_sc[0:tile_slabs, :] = jnp.zeros((tile_slabs, LANES), F32)
        tail = tile_copy(0, n_assign)
        tail.start()
        tail.wait()


def moe_experts(x_sorted, item_expert, item_block, item_rows, item_first, n_assign,
                w_gate_up, b_gate_up, w_down, b_down):
    n_items = item_expert.shape[0]
    d = x_sorted.shape[1]
    n_exp, d_ff, d_out = w_down.shape
    n_f = d_ff // MOE_F_TILE
    n_n = d_out // MOE_N_TILE

    def f_idx(i, s, nr):
        return jnp.minimum(jnp.where(nr[i] > 0, s, n_f - 1), n_f - 1)

    def n_idx(i, s, nr):
        return jnp.where(nr[i] > 0, jnp.maximum(s - n_f, 0), n_n - 1)

    grid_spec = pltpu.PrefetchScalarGridSpec(
        num_scalar_prefetch=4,
        grid=(n_items, n_f + n_n),
        in_specs=[
            pl.BlockSpec((MOE_GROUP, d), lambda i, s, ie, ib, nr, j0: (ib[i], 0)),
            pl.BlockSpec((1, d, MOE_F_TILE), lambda i, s, ie, ib, nr, j0: (ie[i], 0, f_idx(i, s, nr))),
            pl.BlockSpec((1, d, MOE_F_TILE), lambda i, s, ie, ib, nr, j0: (ie[i], 0, n_f + f_idx(i, s, nr))),
            pl.BlockSpec((1, d_ff, MOE_N_TILE), lambda i, s, ie, ib, nr, j0: (ie[i], 0, n_idx(i, s, nr))),
            pl.BlockSpec((1, 1, MOE_F_TILE), lambda i, s, ie, ib, nr, j0: (ie[i], 0, f_idx(i, s, nr))),
            pl.BlockSpec((1, 1, MOE_F_TILE), lambda i, s, ie, ib, nr, j0: (ie[i], 0, n_f + f_idx(i, s, nr))),
            pl.BlockSpec((1, 1, MOE_N_TILE), lambda i, s, ie, ib, nr, j0: (ie[i], 0, n_idx(i, s, nr))),
        ],
        out_specs=pl.BlockSpec(memory_space=pl.ANY),
        scratch_shapes=[pltpu.VMEM((n_f, MOE_GROUP, MOE_F_TILE), BF16),
                        pltpu.VMEM((d, 2 * MOE_F_TILE), BF16),
                        pltpu.VMEM((d_ff, MOE_N_TILE), BF16),
                        pltpu.VMEM((MOE_GROUP * ROW_SLABS, LANES), F32),
                        pltpu.SMEM((1,), jnp.int32),
                        pltpu.SemaphoreType.DMA],
    )
    return pl.pallas_call(
        functools.partial(_moe_kernel, n_f=n_f, n_n=n_n, n_assign=n_assign),
        grid_spec=grid_spec,
        out_shape=jax.ShapeDtypeStruct(((n_assign + MOE_ROW_TILE) * ROW_SLABS, LANES), F32),
        compiler_params=_params("arbitrary", "arbitrary"),
        name="moe_experts",
    )(item_expert, item_block, item_rows, item_first, x_sorted, w_gate_up, w_gate_up, w_down,
      b_gate_up.reshape(n_exp, 1, 2 * d_ff), b_gate_up.reshape(n_exp, 1, 2 * d_ff), b_down.reshape(n_exp, 1, d_out))


def _combine_kernel(pos_ref, nxt_ref, y_hbm, x1_ref, gate_ref, g_ref, b_ref, oa_ref, ob_ref, buf, sem, *, alpha, n_a):
    tm = COMBINE_TOKENS
    t = pl.program_id(0)
    slot = t % 2

    @pl.when(t == 0)
    def _first():
        _start_row_gather(lambda r: pos_ref[0, 0, r], tm * TOP_K, y_hbm, buf.at[0], sem.at[0])

    @pl.when(t + 1 < pl.num_programs(0))
    def _next():
        _start_row_gather(lambda r: nxt_ref[0, 0, r], tm * TOP_K, y_hbm, buf.at[1 - slot], sem.at[1 - slot])

    cur = buf.at[slot]
    _wait_row_gather(tm * TOP_K, y_hbm, cur, sem.at[slot])
    gate = gate_ref[...]
    f = gate[:, 0:1] * _from_slabs(cur, 0, tm, 0, TOP_K * ROW_SLABS)
    for k in range(1, TOP_K):
        f += gate[:, k:k + 1] * _from_slabs(cur, 0, tm, k * ROW_SLABS, TOP_K * ROW_SLABS)
    out = _layer_norm(alpha * _from_slabs(x1_ref, 0, tm) + f, g_ref[...], b_ref[...])

    @pl.when(t < n_a)
    def _():
        oa_ref[...] = out

    @pl.when(t >= n_a)
    def _():
        ob_ref[...] = out


def moe_combine_norm(y_slabs, pos, gate, x1_slabs, ln_g, ln_b, *, alpha, n_first):
    n_tok = pos.shape[0]
    tm = COMBINE_TOKENS
    d = ln_g.shape[1]
    n_t = n_tok // tm
    n_a = n_first // tm
    vec = pl.BlockSpec((1, d), lambda t: (0, 0))
    idx = pos.reshape(n_t, 1, tm * TOP_K)
    return pl.pallas_call(
        functools.partial(_combine_kernel, alpha=alpha, n_a=n_a),
        grid=(n_t,),
        in_specs=[pl.BlockSpec((1, 1, tm * TOP_K), lambda t: (t, 0, 0), memory_space=pltpu.SMEM),
                  pl.BlockSpec((1, 1, tm * TOP_K), lambda t: (jnp.minimum(t + 1, n_t - 1), 0, 0),
                               memory_space=pltpu.SMEM),
                  pl.BlockSpec(memory_space=pl.ANY),
                  pl.BlockSpec((tm * ROW_SLABS, LANES), lambda t: (t, 0)),
                  pl.BlockSpec((tm, TOP_K), lambda t: (t, 0)), vec, vec],
        out_specs=[pl.BlockSpec((tm, d), lambda t: (jnp.minimum(t, n_a - 1), 0)),
                   pl.BlockSpec((tm, d), lambda t: (jnp.maximum(t - n_a, 0), 0))],
        out_shape=[jax.ShapeDtypeStruct((n_first, d), F32), jax.ShapeDtypeStruct((n_tok - n_first, d), F32)],
        scratch_shapes=[pltpu.VMEM((2, tm * TOP_K * ROW_SLABS, LANES), F32), pltpu.SemaphoreType.DMA((2,))],
        compiler_params=_params("arbitrary"),
        name="moe_combine",
    )(idx, idx, y_slabs, x1_slabs, gate, ln_g, ln_b)


def _route(logits, n_items):
    n_tok = logits.shape[0]
    n_assign = n_tok * TOP_K
    tiles_per_item = MOE_GROUP // MOE_ROW_TILE
    top_logit, top_idx = lax.top_k(logits, TOP_K)
    gate = jax.nn.softmax(top_logit, axis=-1)
    flat_e = top_idx.reshape(-1).astype(jnp.int32)
    order = jnp.argsort(flat_e).astype(jnp.int32)
    rank = jnp.argsort(order).astype(jnp.int32)
    experts = jnp.arange(N_EXPERTS, dtype=jnp.int32)
    onehot = flat_e[:, None] == experts[None, :]
    counts = jnp.sum(onehot, axis=0, dtype=jnp.int32)
    padded = (counts + MOE_GROUP - 1) // MOE_GROUP * MOE_GROUP
    start = jnp.cumsum(counts) - counts
    padded_end = jnp.cumsum(padded)
    padded_start = padded_end - padded

    def expert_of(row0):
        return jnp.minimum(jnp.sum(row0[:, None] >= padded_end[None, :], axis=1, dtype=jnp.int32), N_EXPERTS - 1)

    def table(values, e):
        return jnp.sum(jnp.where(e[:, None] == experts[None, :], values[None, :], 0), axis=1)

    n_real = (padded_end[-1] // MOE_GROUP).astype(jnp.int32)
    item = jnp.minimum(jnp.arange(n_items, dtype=jnp.int32), jnp.maximum(n_real - 1, 0))
    item_expert = expert_of(item * MOE_GROUP)
    filled = jnp.clip(table(padded_start + counts, item_expert) - item * MOE_GROUP, 0, MOE_GROUP)
    item_rows = jnp.where(jnp.arange(n_items) < n_real, filled, 0).astype(jnp.int32)
    item_first = jnp.clip(table(start - padded_start, item_expert) + item * MOE_GROUP, 0, n_assign - 1)
    tile_row0 = (jnp.arange(tiles_per_item, dtype=jnp.int32) * MOE_ROW_TILE)[None, :]
    tile_valid = jnp.clip(item_rows[:, None] - tile_row0, 0, MOE_ROW_TILE).reshape(-1).astype(jnp.int32)
    tile_first = jnp.clip(item_first[:, None] + tile_row0, 0, n_assign - 1).reshape(-1).astype(jnp.int32)
    return (gate, order, rank.reshape(n_tok, TOP_K), item_expert, item.astype(jnp.int32), item_rows,
            item_first.astype(jnp.int32), tile_valid, tile_first)


def _pad_rw_cols(a, rw_cols, pad):
    lead = a.shape[:-1]
    return jnp.concatenate([a[..., :rw_cols], jnp.zeros(lead + (pad,), a.dtype), a[..., rw_cols:]], axis=-1)


def _cache_rows(c, heads, head_dim):
    b, n_mem = c.shape[:2]
    tiles = head_dim // LANES
    c = c.reshape(b, n_mem, heads, tiles, LANES).transpose(0, 1, 3, 2, 4)
    return c.reshape(b, n_mem * tiles * heads, LANES)


def _layer(xp, xs, mem_prompt, cache_k, cache_v, st_rwkv, st_shift, st_conv, wts, depth):
    (w_in, w_w_up, w0, w_a_up, a0, w_g_up, tshift_mu, k_k, k_a, r_k, gn_g, gn_b, conv_w,
     w_mem_k, w_mem_v, w_proj_a, w_proj_b, w_proj_m, w_o, ln1_g, ln1_b,
     router_w, router_b, w_gate_up, b_gate_up, w_down, b_down, ln2_g, ln2_b) = wts
    bp, sp, d = xp.shape
    bs, ss, _ = xs.shape
    n_mem = mem_prompt.shape[1]
    mem_heads, mem_head = cache_k.shape[-2:]
    n_p, n_s = bp * sp, bs * ss
    n_tok = n_p + n_s
    alpha = (2.0 * depth) ** 0.25
    rw_cols = 3 * d + DECAY_LORA + AICL_LORA + GATE_LORA
    pad = LORA_PAD - (rw_cols - 3 * d)
    rw_pad = rw_cols + pad

    xp2, xs2 = xp.reshape(n_p, d), xs.reshape(n_s, d)
    x_bf = _b16(jnp.concatenate([xp2, xs2], axis=0))
    mu_pad = _pad_rw_cols(tshift_mu, rw_cols, pad)[None, :rw_pad]
    w_rest = _b16(w_in[:, rw_cols:])

    p_rw = matmul_f32_weights(x_bf, w_in, rw_pad, F32, 1024, 512, "in_proj_rw")
    p2 = matmul(x_bf, w_rest, BF16, 1024, 1024, "in_proj_rest")
    prev_s = matmul_f32_weights(st_shift, w_in, rw_pad, F32, bs, 512, "prev_proj").reshape(bs, 1, rw_pad)
    prev_p = jnp.zeros((bp, 1, rw_pad), F32)

    def lora_rows(w, row0):
        return _b16(jnp.zeros((LORA_PAD, d), F32).at[row0:row0 + w.shape[0]].set(w))

    ww = lora_rows(w_w_up, 0)
    wa = lora_rows(w_a_up, DECAY_LORA)
    wg = lora_rows(w_g_up, DECAY_LORA + AICL_LORA)
    r2 = lambda v: v.reshape(1, d)
    rw_vecs = (mu_pad, r2(w0), r2(a0), r2(k_k), r2(k_a), r2(r_k), r2(gn_g), r2(gn_b), ww, wa, wg)
    heads = d // RW_HEAD
    ya_p, rw_p = rwkv_time_mix(p_rw, prev_p, jnp.zeros((bp, heads, RW_HEAD, RW_HEAD), F32), *rw_vecs,
                               batch=bp, seq=sp, row_block0=0, pairs=8, has_state=False)
    ya_s, rw_s = rwkv_time_mix(p_rw, prev_s, st_rwkv, *rw_vecs,
                               batch=bs, seq=ss, row_block0=n_p // RW_ROWS, pairs=4, has_state=True)

    yb_p, cv_p = conv_sequences(p2, conv_w, batch=bp, seq=sp, width=d, rows=256, tn=512)
    yb_s, cv_s0, cv_s1 = conv_short(p2, st_conv[:, 0, :], st_conv[:, 1, :], conv_w, row_block0=n_p // 128,
                                    batch=bs, seq=ss, width=d, rows=128, tn=512)
    cv_s = jnp.stack([cv_s0, cv_s1], axis=1)

    mem_in = _b16(mem_prompt.reshape(bp * n_mem, d))
    mk = matmul(mem_in, _b16(w_mem_k), F32, bp * n_mem, 512, "mem_k")
    mv = matmul(mem_in, _b16(w_mem_v), F32, bp * n_mem, 512, "mem_v")
    q_col0 = 3 * d // mem_head
    ym_p = attention_sequences(p2, q_col0, mk, mv, batch=bp, seq=sp, n_mem=n_mem, heads=mem_heads,
                               head_dim=mem_head, tq=512)
    bb = 2
    ym_s = attention_cache(p2, 3 * d // d, n_p // (bb * ss), _cache_rows(cache_k, mem_heads, mem_head),
                           _cache_rows(cache_v, mem_heads, mem_head),
                           batch=bs, seq=ss, heads=mem_heads, head_dim=mem_head, bb=bb)

    merged = merge_branches((ya_p, yb_p, ym_p), (ya_s, yb_s, ym_s), p2, 4 * d // 512, _b16(w_proj_a),
                            _b16(w_proj_b), _b16(w_proj_m), tm=512, tn=512)
    rw_pad_r = jnp.zeros((d, LANES), F32).at[:, :N_EXPERTS].set(router_w)
    rb_pad_r = jnp.zeros((1, LANES), F32).at[0, :N_EXPERTS].set(router_b)
    x1_slabs, logits = project_norm_route(merged, xp2, xs2, _b16(w_o), r2(ln1_g), r2(ln1_b),
                                          rw_pad_r, rb_pad_r, alpha=alpha, tm=256)

    n_items = n_tok * TOP_K // MOE_GROUP + N_EXPERTS
    (gate, order, rank, item_expert, item_block, item_rows, item_first, tile_valid,
     tile_first) = _route(logits[:, :N_EXPERTS], n_items)
    x_sorted = moe_dispatch(x1_slabs, order, tile_valid, tile_first, d)
    y_slabs = moe_experts(x_sorted, item_expert, item_block, item_rows, item_first, n_tok * TOP_K,
                          w_gate_up, b_gate_up, w_down, b_down)
    y_p, y_s = moe_combine_norm(y_slabs, rank, gate, x1_slabs, r2(ln2_g), r2(ln2_b), alpha=alpha, n_first=n_p)
    y_p = y_p.reshape(bp, sp, d)
    y_s = y_s.reshape(bs, ss, d)
    mk5 = mk.reshape(bp, n_mem, mem_heads, mem_head)
    mv5 = mv.reshape(bp, n_mem, mem_heads, mem_head)
    return y_p, y_s, mk5, mv5, rw_p, xp[:, -1], cv_p, rw_s, xs[:, -1], cv_s


def kernel(x_prompt, x_sample, mem_prompt, cache_mem_k, cache_mem_v, state_rwkv, state_shift, state_conv, w_in, w_w_up, w0, w_a_up, a0, w_g_up, tshift_mu, k_k, k_a, r_k, gn_g, gn_b, conv_w, w_mem_k, w_mem_v, w_proj_a, w_proj_b, w_proj_m, w_o, ln1_g, ln1_b, router_w, router_b, w_gate_up, b_gate_up, w_down, b_down, ln2_g, ln2_b):
    weights = (w_in, w_w_up, w0, w_a_up, a0, w_g_up, tshift_mu, k_k, k_a, r_k, gn_g, gn_b, conv_w,
               w_mem_k, w_mem_v, w_proj_a, w_proj_b, w_proj_m, w_o, ln1_g, ln1_b,
               router_w, router_b, w_gate_up, b_gate_up, w_down, b_down, ln2_g, ln2_b)
    depth = w_in.shape[0]
    yp, ys = x_prompt, x_sample
    outs = [[] for _ in range(8)]
    for l in range(depth):
        res = _layer(yp, ys, mem_prompt, cache_mem_k[l], cache_mem_v[l], state_rwkv[l], state_shift[l],
                     state_conv[l], tuple(w[l] for w in weights), depth)
        yp, ys = res[0], res[1]
        for acc, r in zip(outs, res[2:]):
            acc.append(r)
    return (yp, ys) + tuple(jnp.stack(o) for o in outs)
```

```python
import functools
import math

import jax
import jax.numpy as jnp
from jax import lax
from jax.experimental import pallas as pl
from jax.experimental.pallas import tpu as pltpu

F32 = jnp.float32
BF16 = jnp.bfloat16

LANES = 128
ROW_SLABS = 16
ROW_PITCH = 24
RW_HEAD = 64
HEADS_PER_BLOCK = LANES // RW_HEAD
RW_ROWS = 64
DECAY_LORA = 96
AICL_LORA = 96
GATE_LORA = 256
LORA_PAD = 512
DECAY_SCALE = math.exp(-0.5)
GN_EPS = 64e-5
LN_EPS = 1e-5
N_EXPERTS = 32
TOP_K = 4
SWIGLU_LIMIT = 7.0
SWIGLU_ALPHA = 1.702
VMEM_LIMIT = 56 * 1024 * 1024


def _dot(a, b):
    return jnp.dot(a, b, preferred_element_type=F32)


def _dot_nt(a, b):
    return lax.dot_general(a, b, (((1,), (1,)), ((), ())), preferred_element_type=F32)


def _dot_tn(a, b):
    return lax.dot_general(a, b, (((0,), (0,)), ((), ())), preferred_element_type=F32)


def _b16(x):
    return x.astype(BF16)


def _params(*sem):
    return pltpu.CompilerParams(dimension_semantics=sem, vmem_limit_bytes=VMEM_LIMIT)


def _split3(x):
    hi = _b16(x)
    r1 = x - hi.astype(F32)
    mid = _b16(r1)
    lo = _b16(r1 - mid.astype(F32))
    return hi, mid, lo


def _rwkv_kernel(pr_ref, pk_ref, pv_ref, pl_ref, qr_ref, qk_ref, qv_ref, ql_ref,
                 mur_ref, muk_ref, muv_ref, mul_ref,
                 w0_ref, a0_ref, kk_ref, ka_ref, rk_ref, gng_ref, gnb_ref,
                 ww_ref, wa_ref, wg_ref, s0_ref,
                 y_ref, sout_ref, *scratch, n_seq, n_chunks, pairs, has_state):
    C = RW_ROWS
    L = C // n_seq
    R = HEADS_PER_BLOCK * C
    W = pairs * LANES
    c = pl.program_id(2)
    carried = n_chunks > 1
    if carried:
        s_sc, cr_sc, ck_sc, cv_sc, cl_sc = scratch

        @pl.when(c == 0)
        def _init_state():
            for p in range(pairs):
                if has_state:
                    s_sc[p] = s0_ref[0, 2 * p:2 * p + 2].reshape(LANES, RW_HEAD)
                else:
                    s_sc[p] = jnp.zeros((LANES, RW_HEAD), F32)

    def prev_rows(q_ref, carry_sc, width):
        if carried:
            @pl.when(c == 0)
            def _():
                carry_sc[0:1, :] = q_ref[0]
            return jnp.broadcast_to(carry_sc[0:1, :], (C, width))
        q = q_ref[...]
        return jnp.broadcast_to(q, (n_seq, L, width)).reshape(C, width)

    def shifted_lerp(p_ref, q_ref, carry_sc, mu_ref, width):
        p = p_ref[...].astype(F32)
        pos = lax.broadcasted_iota(jnp.int32, (C, width), 0) % L
        prev = jnp.where(pos == 0, prev_rows(q_ref, carry_sc, width), pltpu.roll(p, 1, axis=0))
        if carried:
            carry_sc[0:1, :] = p[C - 1:C, :]
        return p + (prev - p) * mu_ref[...]

    zr = shifted_lerp(pr_ref, qr_ref, cr_sc if carried else None, mur_ref, W)
    zk = shifted_lerp(pk_ref, qk_ref, ck_sc if carried else None, muk_ref, W)
    zv = shifted_lerp(pv_ref, qv_ref, cv_sc if carried else None, muv_ref, W)
    zl = shifted_lerp(pl_ref, ql_ref, cl_sc if carried else None, mul_ref, LORA_PAD)

    lw = -DECAY_SCALE * jax.nn.sigmoid(w0_ref[...] + _dot(_b16(jnp.tanh(zl)), ww_ref[...]))
    a_all = jax.nn.sigmoid(a0_ref[...] + _dot(_b16(zl), wa_ref[...]))
    g_all = _dot(_b16(jax.nn.sigmoid(zl)), wg_ref[...])

    ti = lax.broadcasted_iota(jnp.int32, (2 * C, C), 0)
    tj = lax.broadcasted_iota(jnp.int32, (2 * C, C), 1)
    same_seq = (ti % C) // L == tj // L
    cum_lhs = _b16((same_seq & ((ti >= C) | (tj <= ti))).astype(F32))
    hi, mid, lo = _split3(lw)
    cum = _dot(cum_lhs, jnp.concatenate([hi, mid, lo], axis=1))
    cum = cum[:, 0:W] + cum[:, W:2 * W] + cum[:, 2 * W:3 * W]
    cw_all = cum[0:C]
    tot_all = cum[C:2 * C]

    lb_r = lax.broadcasted_iota(jnp.int32, (2 * LANES, LANES), 0) % LANES // RW_HEAD
    lb_c = lax.broadcasted_iota(jnp.int32, (2 * LANES, LANES), 1) // RW_HEAD
    head_ones2 = _b16((lb_r == lb_c).astype(F32))

    def head_sum(x):
        xh = _b16(x)
        xl = _b16(x - xh.astype(F32))
        return _dot(jnp.concatenate([xh, xl], axis=1), head_ones2)

    head0 = lax.broadcasted_iota(jnp.int32, (n_seq, L, LANES), 2) < RW_HEAD

    def stack(x):
        x3 = x.reshape(n_seq, L, LANES)
        return jnp.concatenate([jnp.where(head0, x3, 0.0), jnp.where(head0, 0.0, x3)], axis=1).reshape(R, LANES)

    def unstack(x):
        x3 = x.reshape(n_seq, 2 * L, LANES)
        return (x3[:, 0:L, :] + x3[:, L:2 * L, :]).reshape(C, LANES)

    si = lax.broadcasted_iota(jnp.int32, (R, R), 0)
    sj = lax.broadcasted_iota(jnp.int32, (R, R), 1)
    same = (si // L) == (sj // L)
    strict = same & (si > sj)
    incl = same & (si >= sj)
    eye = (si == sj).astype(F32)
    row_head = (lax.broadcasted_iota(jnp.int32, (R, LANES), 0) // L) % HEADS_PER_BLOCK
    lane_head = lax.broadcasted_iota(jnp.int32, (R, LANES), 1) // RW_HEAD
    head_match = row_head == lane_head
    state_row_head0 = lax.broadcasted_iota(jnp.int32, (LANES, LANES), 0) < RW_HEAD
    n_lvl = int(math.log2(L)) - 1
    S2 = 2 * L

    P = range(pairs)
    lsl = [slice(p * LANES, (p + 1) * LANES) for p in P]

    def rows_cat(xs):
        return jnp.concatenate(xs, axis=0) if len(xs) > 1 else xs[0]

    kk_raw = [zk[:, ls] * kk_ref[:, ls] for ls in lsl]
    kk_n2 = head_sum(rows_cat([x * x for x in kk_raw]))
    kk_l = [kk_raw[p] / jnp.maximum(jnp.sqrt(kk_n2[p * C:(p + 1) * C]), 1e-12) for p in P]
    kmod_l = [zk[:, ls] * (1.0 + (a_all[:, ls] - 1.0) * ka_ref[:, ls]) for ls in lsl]
    beta_l = [kk_l[p] * a_all[:, lsl[p]] for p in P]
    e_neg_l = [jnp.exp(-cw_all[:, ls]) for ls in lsl]
    e_tail_l = [jnp.exp(tot_all[:, ls] - cw_all[:, ls]) for ls in lsl]
    kk_s_l = [stack(kk_l[p] * jnp.exp(cw_all[:, lsl[p]] - lw[:, lsl[p]])) for p in P]
    r_s_l = [stack(zr[:, ls] * jnp.exp(cw_all[:, ls])) for ls in lsl]
    v_b_l = [_b16(stack(zv[:, ls])) for ls in lsl]
    bw_b_l = [_b16(stack(beta_l[p] * e_tail_l[p])) for p in P]
    kw_b_l = [_b16(stack(kmod_l[p] * e_tail_l[p])) for p in P]
    a_lhs = [_b16(jnp.concatenate([kk_s_l[p], r_s_l[p]], axis=0)) for p in P]
    a_rhs = [_b16(jnp.concatenate([stack(beta_l[p] * e_neg_l[p]), stack(kmod_l[p] * e_neg_l[p])], axis=0)) for p in P]

    amat_l = [_dot_nt(a_lhs[p], a_rhs[p]) for p in P]
    a_ak_b = [_b16(jnp.where(strict, m[0:R, R:2 * R], 0.0)) for m in amat_l]
    a_r_b = [_b16(jnp.concatenate([jnp.where(incl, m[R:2 * R, 0:R], 0.0),
                                    jnp.where(incl, m[R:2 * R, R:2 * R], 0.0)], axis=1)) for m in amat_l]

    x_l = [jnp.where(strict, -m[0:R, 0:R], 0.0) for m in amat_l]
    minv_l = [eye + x for x in x_l]
    xb_l = [_b16(x) for x in x_l]
    cur_l = [_dot(xb, xb) for xb in xb_l]
    akv_l = [_dot(a_ak_b[p], v_b_l[p]) for p in P]
    for lvl in range(n_lvl):
        cb_l = [_b16(cur) for cur in cur_l]
        if lvl < n_lvl - 1:
            both_l = [_dot(_b16(jnp.concatenate([minv_l[p], cur_l[p]], axis=0)), cb_l[p]) for p in P]
            minv_l = [minv_l[p] + both_l[p][0:R] for p in P]
            cur_l = [both[R:2 * R] for both in both_l]
        else:
            minv_l = [minv_l[p] + _dot(_b16(minv_l[p]), cb_l[p]) for p in P]

    pq_l = [_dot(_b16(minv_l[p]), _b16(jnp.concatenate([kk_s_l[p], akv_l[p]], axis=1))) for p in P]

    def fold_lanes(x):
        return _b16(x + pltpu.roll(x, RW_HEAD, axis=1))[:, 0:RW_HEAD]

    p_fold = [fold_lanes(-pq[:, 0:LANES]) for pq in pq_l]
    r_fold = [fold_lanes(r_s) for r_s in r_s_l]
    seqs = [(p, b) for p in P for b in range(n_seq)]

    def state_in(p, b):
        if carried:
            return s_sc[p]
        return s0_ref[b, 2 * p:2 * p + 2].reshape(LANES, RW_HEAD)

    states = {pb: state_in(*pb) for pb in seqs}
    sdot = {(p, b): _dot_nt(jnp.concatenate([p_fold[p][b * S2:(b + 1) * S2], r_fold[p][b * S2:(b + 1) * S2]], axis=0),
                            _b16(states[(p, b)])) for (p, b) in seqs}
    u_b_l, rs_l = [], []
    for p in P:
        us = [jnp.where(head_match[b * S2:(b + 1) * S2], sdot[(p, b)][0:S2], 0.0) for b in range(n_seq)]
        rs = [jnp.where(head_match[b * S2:(b + 1) * S2], sdot[(p, b)][S2:2 * S2], 0.0) for b in range(n_seq)]
        u_b_l.append(_b16(rows_cat(us) - pq_l[p][:, LANES:2 * LANES]))
        rs_l.append(rows_cat(rs))

    y_s_l = [rs_l[p] + _dot(a_r_b[p], jnp.concatenate([u_b_l[p], v_b_l[p]], axis=0)) for p in P]
    z_l = {(p, b): _dot_tn(jnp.concatenate([u_b_l[p][b * S2:(b + 1) * S2], v_b_l[p][b * S2:(b + 1) * S2]], axis=0),
                           jnp.concatenate([bw_b_l[p][b * S2:(b + 1) * S2], kw_b_l[p][b * S2:(b + 1) * S2]], axis=0))
           for (p, b) in seqs}
    for (p, b) in seqs:
        z = z_l[(p, b)]
        z = (z + pltpu.roll(z, RW_HEAD, axis=1))[:, 0:RW_HEAD]
        wt = jnp.broadcast_to(jnp.exp(tot_all[b * L:b * L + 1, lsl[p]]), (LANES, LANES))
        wnat = jnp.where(state_row_head0, wt, pltpu.roll(wt, RW_HEAD, axis=1))[:, 0:RW_HEAD]
        s_new = states[(p, b)] * wnat + z
        if carried:
            s_sc[p] = s_new
        sout_ref[b, 2 * p:2 * p + 2] = s_new.reshape(HEADS_PER_BLOCK, RW_HEAD, RW_HEAD)

    y_l = [unstack(y_s) for y_s in y_s_l]
    stats = head_sum(rows_cat(y_l + [zr[:, ls] * kmod_l[p] * rk_ref[:, ls] for p, ls in enumerate(lsl)]))
    d_l = [y_l[p] - stats[p * C:(p + 1) * C] * (1.0 / RW_HEAD) for p in P]
    var = head_sum(rows_cat([d * d for d in d_l])) * (1.0 / RW_HEAD)
    for p, ls in enumerate(lsl):
        yn = d_l[p] * lax.rsqrt(var[p * C:(p + 1) * C] + GN_EPS) * gng_ref[:, ls] + gnb_ref[:, ls]
        bonus = stats[(pairs + p) * C:(pairs + p + 1) * C] * zv[:, ls]
        y_ref[:, ls] = ((yn + bonus) * g_all[:, ls]).astype(y_ref.dtype)


def rwkv_time_mix(p_rw, prev_rw, s0, mu, w0, a0, k_k, k_a, r_k, gn_g, gn_b, ww, wa, wg,
                  *, batch, seq, row_block0, pairs, has_state):
    width = w0.shape[-1]
    wp = pairs * LANES
    n_blk = width // wp
    heads = width // RW_HEAD
    hp = HEADS_PER_BLOCK * pairs
    lora_blk = 3 * width // LORA_PAD
    if seq >= RW_ROWS:
        n_seq, n_chunks, n_outer = 1, seq // RW_ROWS, batch
    else:
        n_seq, n_chunks, n_outer = RW_ROWS // seq, 1, batch * seq // RW_ROWS

    def tok(col0):
        return pl.BlockSpec((RW_ROWS, wp), lambda b, h, c: (row_block0 + b * n_chunks + c, col0 + h))

    def prev(col0):
        return pl.BlockSpec((n_seq, 1, wp), lambda b, h, c: (b, 0, col0 + h))

    def vec(col0):
        return pl.BlockSpec((1, wp), lambda b, h, c: (0, col0 + h))

    lora_w = pl.BlockSpec((LORA_PAD, wp), lambda b, h, c: (0, h))
    state = pl.BlockSpec((n_seq, hp, RW_HEAD, RW_HEAD), lambda b, h, c: (b, h, 0, 0))
    in_specs = [
        tok(0), tok(n_blk), tok(2 * n_blk),
        pl.BlockSpec((RW_ROWS, LORA_PAD), lambda b, h, c: (row_block0 + b * n_chunks + c, lora_blk)),
        prev(0), prev(n_blk), prev(2 * n_blk),
        pl.BlockSpec((n_seq, 1, LORA_PAD), lambda b, h, c: (b, 0, lora_blk)),
        vec(0), vec(n_blk), vec(2 * n_blk),
        pl.BlockSpec((1, LORA_PAD), lambda b, h, c: (0, lora_blk)),
        vec(0), vec(0), vec(0), vec(0), vec(0), vec(0), vec(0),
        lora_w, lora_w, lora_w, state,
    ]
    out_specs = [pl.BlockSpec((RW_ROWS, wp), lambda b, h, c: (b * n_chunks + c, h)), state]
    scratch = []
    if n_chunks > 1:
        scratch = [pltpu.VMEM((pairs, LANES, RW_HEAD), F32), pltpu.VMEM((8, wp), F32), pltpu.VMEM((8, wp), F32),
                   pltpu.VMEM((8, wp), F32), pltpu.VMEM((8, LORA_PAD), F32)]
    kern = functools.partial(_rwkv_kernel, n_seq=n_seq, n_chunks=n_chunks, pairs=pairs, has_state=has_state)
    return pl.pallas_call(
        kern,
        grid=(n_outer, n_blk, n_chunks),
        in_specs=in_specs,
        out_specs=out_specs,
        out_shape=[jax.ShapeDtypeStruct((batch * seq, width), BF16),
                   jax.ShapeDtypeStruct((batch, heads, RW_HEAD, RW_HEAD), F32)],
        scratch_shapes=scratch,
        compiler_params=_params("parallel", "parallel", "arbitrary"),
        name=f"rwkv_l{min(seq, RW_ROWS)}",
    )(p_rw, p_rw, p_rw, p_rw, prev_rw, prev_rw, prev_rw, prev_rw, mu, mu, mu, mu,
      w0, a0, k_k, k_a, r_k, gn_g, gn_b, ww, wa, wg, s0)


def _mm_kernel(x_ref, w_ref, o_ref):
    o_ref[...] = _dot(_b16(x_ref[...]), w_ref[...]).astype(o_ref.dtype)


def matmul(x, w, out_dtype, tm, tn, name):
    m, kd = x.shape
    n = w.shape[1]
    return pl.pallas_call(
        _mm_kernel,
        grid=(n // tn, m // tm),
        in_specs=[pl.BlockSpec((tm, kd), lambda j, i: (i, 0)),
                  pl.BlockSpec((kd, tn), lambda j, i: (0, j))],
        out_specs=pl.BlockSpec((tm, tn), lambda j, i: (i, j)),
        out_shape=jax.ShapeDtypeStruct((m, n), out_dtype),
        compiler_params=_params("parallel", "parallel"),
        name=name,
    )(x, w)


def _mm_cast_kernel(x_ref, w_ref, o_ref, w_sc):
    @pl.when(pl.program_id(1) == 0)
    def _():
        w_sc[...] = _b16(w_ref[...])

    o_ref[...] = _dot(_b16(x_ref[...]), w_sc[...]).astype(o_ref.dtype)


def _mm_shift_kernel(x_ref, a_ref, b_ref, o_ref, w_sc, *, shift):
    @pl.when(pl.program_id(1) == 0)
    def _():
        tn = a_ref.shape[1]
        a = pltpu.roll(a_ref[...].astype(F32), tn - shift, axis=1)
        b = pltpu.roll(b_ref[...].astype(F32), LANES - shift, axis=1)
        lane = lax.broadcasted_iota(jnp.int32, b.shape, 1)
        w_sc[:, 0:tn - LANES] = _b16(a[:, 0:tn - LANES])
        w_sc[:, tn - LANES:tn] = _b16(jnp.where(lane < LANES - shift, a[:, tn - LANES:tn], b))

    o_ref[...] = _dot(x_ref[...], w_sc[...]).astype(o_ref.dtype)


def matmul_shifted_weights(x, w, n_cols, shift, out_dtype, tm, tn, name):
    m, kd = x.shape
    return pl.pallas_call(
        functools.partial(_mm_shift_kernel, shift=shift),
        grid=(n_cols // tn, m // tm),
        in_specs=[pl.BlockSpec((tm, kd), lambda j, i: (i, 0)),
                  pl.BlockSpec((kd, tn), lambda j, i: (0, j)),
                  pl.BlockSpec((kd, LANES), lambda j, i: (0, (j + 1) * (tn // LANES)))],
        out_specs=pl.BlockSpec((tm, tn), lambda j, i: (i, j)),
        out_shape=jax.ShapeDtypeStruct((m, n_cols), out_dtype),
        scratch_shapes=[pltpu.VMEM((kd, tn), BF16)],
        compiler_params=_params("parallel", "arbitrary"),
        name=name,
    )(x, w, w)


def matmul_f32_weights(x, w, n_cols, out_dtype, tm, tn, name):
    m, kd = x.shape
    return pl.pallas_call(
        _mm_cast_kernel,
        grid=(n_cols // tn, m // tm),
        in_specs=[pl.BlockSpec((tm, kd), lambda j, i: (i, 0)),
                  pl.BlockSpec((kd, tn), lambda j, i: (0, j))],
        out_specs=pl.BlockSpec((tm, tn), lambda j, i: (i, j)),
        out_shape=jax.ShapeDtypeStruct((m, n_cols), out_dtype),
        scratch_shapes=[pltpu.VMEM((kd, tn), BF16)],
        compiler_params=_params("parallel", "arbitrary"),
        name=name,
    )(x, w)


def _conv_taps(u, u1, u2, bg_ref, cw_ref, y_ref):
    y = cw_ref[0:1, :] * u2 + cw_ref[1:2, :] * u1 + cw_ref[2:3, :] * u
    y_ref[...] = (bg_ref[...].astype(F32) * y).astype(y_ref.dtype)


def _conv_seq_kernel(bg_ref, cg_ref, xi_ref, cw_ref, y_ref, new_ref, carry_sc, *, rows, n_t):
    i = pl.program_id(2)

    @pl.when(i == 0)
    def _init():
        carry_sc[...] = jnp.zeros(carry_sc.shape, F32)

    u = cg_ref[...].astype(F32) * xi_ref[...].astype(F32)
    r = lax.broadcasted_iota(jnp.int32, u.shape, 0)
    c0 = carry_sc[0:1, :]
    c1 = carry_sc[1:2, :]
    u1 = jnp.where(r == 0, c1, pltpu.roll(u, 1, axis=0))
    u2 = jnp.where(r == 0, c0, jnp.where(r == 1, c1, pltpu.roll(u, 2, axis=0)))
    _conv_taps(u, u1, u2, bg_ref, cw_ref, y_ref)
    carry_sc[0:2, :] = u[rows - 2:rows, :]

    @pl.when(i == n_t - 1)
    def _fin():
        new_ref[0] = u[rows - 2:rows, :]


def conv_sequences(p2, conv_w, *, batch, seq, width, rows, tn):
    nb = width // tn
    n_t = seq // rows

    def tok(sec):
        return pl.BlockSpec((rows, tn), lambda b, j, i: (b * n_t + i, sec * nb + j))

    return pl.pallas_call(
        functools.partial(_conv_seq_kernel, rows=rows, n_t=n_t),
        grid=(batch, nb, n_t),
        in_specs=[tok(0), tok(1), tok(2), pl.BlockSpec((3, tn), lambda b, j, i: (0, j))],
        out_specs=[pl.BlockSpec((rows, tn), lambda b, j, i: (b * n_t + i, j)),
                   pl.BlockSpec((1, 2, tn), lambda b, j, i: (b, 0, j))],
        out_shape=[jax.ShapeDtypeStruct((batch * seq, width), BF16),
                   jax.ShapeDtypeStruct((batch, 2, width), F32)],
        scratch_shapes=[pltpu.VMEM((8, tn), F32)],
        compiler_params=_params("parallel", "parallel", "arbitrary"),
        name="conv_seq",
    )(p2, p2, p2, conv_w)


def _conv_short_kernel(bg_ref, cg_ref, xi_ref, p0_ref, p1_ref, cw_ref, y_ref, n0_ref, n1_ref, *, seq):
    u = cg_ref[...].astype(F32) * xi_ref[...].astype(F32)
    rows, tn = u.shape
    n_seq = rows // seq

    def per_seq(p_ref):
        return jnp.broadcast_to(p_ref[...].reshape(n_seq, 1, tn), (n_seq, seq, tn)).reshape(rows, tn)

    h0, h1 = per_seq(p0_ref), per_seq(p1_ref)
    pos = lax.broadcasted_iota(jnp.int32, u.shape, 0) % seq
    u1 = jnp.where(pos == 0, h1, pltpu.roll(u, 1, axis=0))
    u2 = jnp.where(pos == 0, h0, jnp.where(pos == 1, h1, pltpu.roll(u, 2, axis=0)))
    _conv_taps(u, u1, u2, bg_ref, cw_ref, y_ref)
    u3 = u.reshape(n_seq, seq, tn)
    n0_ref[...] = u3[:, seq - 2, :]
    n1_ref[...] = u3[:, seq - 1, :]


def conv_short(p2, hist0, hist1, conv_w, *, row_block0, batch, seq, width, rows, tn):
    nb = width // tn
    n_seq = rows // seq
    n_r = batch * seq // rows

    def tok(sec):
        return pl.BlockSpec((rows, tn), lambda i, j: (row_block0 + i, sec * nb + j))

    per_seq = pl.BlockSpec((n_seq, tn), lambda i, j: (i, j))
    return pl.pallas_call(
        functools.partial(_conv_short_kernel, seq=seq),
        grid=(n_r, nb),
        in_specs=[tok(0), tok(1), tok(2), per_seq, per_seq, pl.BlockSpec((3, tn), lambda i, j: (0, j))],
        out_specs=[pl.BlockSpec((rows, tn), lambda i, j: (i, j)), per_seq, per_seq],
        out_shape=[jax.ShapeDtypeStruct((batch * seq, width), BF16),
                   jax.ShapeDtypeStruct((batch, width), F32), jax.ShapeDtypeStruct((batch, width), F32)],
        compiler_params=_params("parallel", "parallel"),
        name="conv_short",
    )(p2, p2, p2, hist0, hist1, conv_w)


def _softmax_rows(s):
    m = jnp.max(s, axis=-1, keepdims=True)
    e = jnp.exp(s - m)
    return e / jnp.sum(e, axis=-1, keepdims=True)


def _attn_seq_kernel(q_ref, k_ref, v_ref, o_ref, *, scale):
    s = _dot_nt(q_ref[...], _b16(k_ref[...])) * scale
    p = _softmax_rows(s)
    o_ref[...] = _dot(_b16(p), _b16(v_ref[...])).astype(o_ref.dtype)


def attention_sequences(p2, q_col0, mem_k, mem_v, *, batch, seq, n_mem, heads, head_dim, tq):
    n_t = seq // tq
    kv = pl.BlockSpec((n_mem, head_dim), lambda b, h, i: (b, h))
    return pl.pallas_call(
        functools.partial(_attn_seq_kernel, scale=head_dim ** -0.5),
        grid=(batch, heads, n_t),
        in_specs=[pl.BlockSpec((tq, head_dim), lambda b, h, i: (b * n_t + i, q_col0 + h)), kv, kv],
        out_specs=pl.BlockSpec((tq, head_dim), lambda b, h, i: (b * n_t + i, h)),
        out_shape=jax.ShapeDtypeStruct((batch * seq, heads * head_dim), BF16),
        compiler_params=_params("parallel", "parallel", "arbitrary"),
        name="attn_seq",
    )(p2, mem_k, mem_v)


def _attn_cache_kernel(q_ref, k_ref, v_ref, o_ref, *, scale, bb, seq, heads, head_dim):
    tiles = head_dim // LANES
    slabs = tiles * heads
    n_mem = k_ref.shape[1] // slabs
    q = q_ref[...].astype(F32)

    def head_slab(ref, b, h):
        return jnp.concatenate([ref[b, pl.ds(t * heads + h, n_mem, stride=slabs), :] for t in range(tiles)], axis=1)

    rows = []
    for b in range(bb):
        cols = []
        for h in range(heads):
            qb = _b16(q[b * seq:(b + 1) * seq, h * head_dim:(h + 1) * head_dim])
            s = _dot_nt(qb, _b16(head_slab(k_ref, b, h))) * scale
            p = _softmax_rows(s)
            cols.append(_dot(_b16(p), _b16(head_slab(v_ref, b, h))))
        rows.append(jnp.concatenate(cols, axis=1))
    o_ref[...] = jnp.concatenate(rows, axis=0).astype(o_ref.dtype)


def attention_cache(p2, q_blk0, row0, cache_k, cache_v, *, batch, seq, heads, head_dim, bb):
    d = heads * head_dim
    kv = pl.BlockSpec((bb,) + cache_k.shape[1:], lambda g: (g, 0, 0))
    return pl.pallas_call(
        functools.partial(_attn_cache_kernel, scale=head_dim ** -0.5, bb=bb, seq=seq, heads=heads, head_dim=head_dim),
        grid=(batch // bb,),
        in_specs=[pl.BlockSpec((bb * seq, d), lambda g: (row0 + g, q_blk0)), kv, kv],
        out_specs=pl.BlockSpec((bb * seq, d), lambda g: (g, 0)),
        out_shape=jax.ShapeDtypeStruct((batch * seq, d), BF16),
        compiler_params=_params("parallel"),
        name="attn_cache",
    )(p2, cache_k, cache_v)


def _merge_kernel(ya1_ref, yb1_ref, ym1_ref, ya2_ref, yb2_ref, ym2_ref, g0_ref, g1_ref, g2_ref,
                  wa_ref, wb_ref, wm_ref, o_ref, *, n_first):
    def merge(ya_ref, yb_ref, ym_ref):
        acc = jax.nn.sigmoid(g0_ref[...].astype(F32)) * _dot(_b16(ya_ref[...]), wa_ref[...])
        acc += jax.nn.sigmoid(g1_ref[...].astype(F32)) * _dot(_b16(yb_ref[...]), wb_ref[...])
        acc += jax.nn.sigmoid(g2_ref[...].astype(F32)) * _dot(_b16(ym_ref[...]), wm_ref[...])
        o_ref[...] = acc.astype(o_ref.dtype)

    i = pl.program_id(1)

    @pl.when(i < n_first)
    def _():
        merge(ya1_ref, yb1_ref, ym1_ref)

    @pl.when(i >= n_first)
    def _():
        merge(ya2_ref, yb2_ref, ym2_ref)


def merge_branches(first, second, p2, gate_col0, wa, wb, wm, *, tm, tn):
    m1, d = first[0].shape
    m2 = second[0].shape[0]
    n1 = m1 // tm
    nb = d // tn
    act1 = pl.BlockSpec((tm, d), lambda j, i: (jnp.minimum(i, n1 - 1), 0))
    act2 = pl.BlockSpec((tm, d), lambda j, i: (jnp.maximum(i - n1, 0), 0))
    wsp = pl.BlockSpec((d, tn), lambda j, i: (0, j))

    def gate(k):
        return pl.BlockSpec((tm, tn), lambda j, i: (i, gate_col0 + k * nb + j))

    return pl.pallas_call(
        functools.partial(_merge_kernel, n_first=n1),
        grid=(nb, (m1 + m2) // tm),
        in_specs=[act1, act1, act1, act2, act2, act2, gate(0), gate(1), gate(2), wsp, wsp, wsp],
        out_specs=pl.BlockSpec((tm, tn), lambda j, i: (i, j)),
        out_shape=jax.ShapeDtypeStruct((m1 + m2, d), BF16),
        compiler_params=_params("parallel", "parallel"),
        name="merge",
    )(*first, *second, p2, p2, p2, wa, wb, wm)


def _layer_norm(y, g, b):
    mu = jnp.mean(y, axis=-1, keepdims=True)
    d = y - mu
    var = jnp.mean(d * d, axis=-1, keepdims=True)
    return d * lax.rsqrt(var + LN_EPS) * g + b


def _to_slabs(ref, row0, x):
    rows = x.shape[0]
    for j in range(ROW_SLABS):
        ref[pl.ds(row0 * ROW_PITCH + j, rows, stride=ROW_PITCH), :] = x[:, j * LANES:(j + 1) * LANES]
    for j in range(ROW_SLABS, ROW_PITCH):
        ref[pl.ds(row0 * ROW_PITCH + j, rows, stride=ROW_PITCH), :] = jnp.zeros((rows, LANES), x.dtype)


def _from_slabs(ref, row0, rows):
    return jnp.concatenate(
        [ref[pl.ds(row0 * ROW_PITCH + j, rows, stride=ROW_PITCH), :] for j in range(ROW_SLABS)], axis=1)


def _proj_ln_kernel(m_ref, xa_ref, xb_ref, wo_ref, g_ref, b_ref, rw_ref, rb_ref, x1_ref, lg_ref, *, alpha, n_first):
    h = _dot(m_ref[...], wo_ref[...])
    x = jnp.where(pl.program_id(0) < n_first, xa_ref[...], xb_ref[...])
    x1 = _layer_norm(alpha * x + h, g_ref[...], b_ref[...])
    _to_slabs(x1_ref, 0, x1)
    hi, mid, lo = _split3(x1)
    w_hi, w_mid, w_lo = _split3(rw_ref[...])
    lg = _dot(hi, w_hi) + (_dot(hi, w_mid) + _dot(mid, w_hi)) + (_dot(hi, w_lo) + _dot(mid, w_mid) + _dot(lo, w_hi))
    lg_ref[...] = lg + rb_ref[...]


def project_norm_route(merged, xa, xb, wo, ln_g, ln_b, router_w, router_b, *, alpha, tm):
    m, d = merged.shape
    n_r = router_w.shape[1]
    n1 = xa.shape[0] // tm
    row = pl.BlockSpec((tm, d), lambda i: (i, 0))
    vec = pl.BlockSpec((1, d), lambda i: (0, 0))
    return pl.pallas_call(
        functools.partial(_proj_ln_kernel, alpha=alpha, n_first=n1),
        grid=(m // tm,),
        in_specs=[row, pl.BlockSpec((tm, d), lambda i: (jnp.minimum(i, n1 - 1), 0)),
                  pl.BlockSpec((tm, d), lambda i: (jnp.maximum(i - n1, 0), 0)),
                  pl.BlockSpec((d, d), lambda i: (0, 0)), vec, vec,
                  pl.BlockSpec((d, n_r), lambda i: (0, 0)), pl.BlockSpec((1, n_r), lambda i: (0, 0))],
        out_specs=[pl.BlockSpec((tm * ROW_PITCH, LANES), lambda i: (i, 0)),
                   pl.BlockSpec((tm, n_r), lambda i: (i, 0))],
        out_shape=[jax.ShapeDtypeStruct((m * ROW_PITCH, LANES), F32), jax.ShapeDtypeStruct((m, n_r), F32)],
        compiler_params=_params("parallel"),
        name="proj_ln_route",
    )(merged, xa, xb, wo, ln_g, ln_b, router_w, router_b)


MOE_GROUP = 1280
MOE_ROW_TILE = 640
MOE_F_TILE = 256
MOE_N_TILE = 256
MOE_K_CHUNK = 256
COMBINE_TOKENS = 128


def _row_copy(src_hbm, src_row, dst_vmem, dst_row, sem):
    return pltpu.make_async_copy(
        src_hbm.at[pl.ds(pl.multiple_of(src_row * ROW_PITCH, 8), ROW_SLABS)],
        dst_vmem.at[pl.ds(pl.multiple_of(dst_row * ROW_PITCH, 8), ROW_SLABS)], sem)


def _start_row_gather(src_row, n_rows, src_hbm, buf, sem):
    def issue(r, carry):
        _row_copy(src_hbm, src_row(r), buf, r, sem).start()
        return carry

    lax.fori_loop(0, n_rows, issue, 0, unroll=8)


def _wait_row_gather(n_rows, src_hbm, buf, sem):
    n = n_rows * ROW_SLABS
    pltpu.make_async_copy(src_hbm.at[pl.ds(0, n)], buf.at[pl.ds(0, n)], sem).wait()


def _dispatch_kernel(tv_ref, tj_ref, tok_ref, x_hbm, o_ref, buf, sem):
    t = pl.program_id(0)
    n = pl.num_programs(0)
    slot = t % 2
    rt = MOE_ROW_TILE

    def start(tile, dst):
        j0 = tj_ref[tile]
        _start_row_gather(lambda r: tok_ref[j0 + r], rt, x_hbm, buf.at[dst], sem.at[dst])

    @pl.when((t == 0) & (tv_ref[0] > 0))
    def _first():
        start(0, 0)

    @pl.when((t + 1 < n) & (tv_ref[jnp.minimum(t + 1, n - 1)] > 0))
    def _next():
        start(jnp.minimum(t + 1, n - 1), 1 - slot)

    @pl.when(tv_ref[t] > 0)
    def _():
        _wait_row_gather(rt, x_hbm, buf.at[slot], sem.at[slot])
        o_ref[...] = _b16(_from_slabs(buf.at[slot], 0, rt))

    @pl.when(tv_ref[t] == 0)
    def _():
        o_ref[...] = jnp.zeros(o_ref.shape, o_ref.dtype)


def moe_dispatch(x_slabs, order, tile_valid, tile_first, d):
    n_tiles = tile_valid.shape[0]
    rt = MOE_ROW_TILE
    sorted_tok = jnp.concatenate([order // TOP_K, jnp.zeros((rt,), jnp.int32)])
    grid_spec = pltpu.PrefetchScalarGridSpec(
        num_scalar_prefetch=3,
        grid=(n_tiles,),
        in_specs=[pl.BlockSpec(memory_space=pl.ANY)],
        out_specs=pl.BlockSpec((rt, d), lambda t, tv, tj, od: (t, 0)),
        scratch_shapes=[pltpu.VMEM((2, rt * ROW_PITCH, LANES), F32), pltpu.SemaphoreType.DMA((2,))],
    )
    return pl.pallas_call(
        _dispatch_kernel,
        grid_spec=grid_spec,
        out_shape=jax.ShapeDtypeStruct((n_tiles * rt, d), BF16),
        compiler_params=_params("arbitrary"),
        name="moe_dispatch",
    )(tile_valid, tile_first, sorted_tok, x_slabs)


def _moe_kernel(ie_ref, ib_ref, nr_ref, j0_ref, x_ref, wg_ref, wu_ref, wd_ref, bg_ref, bu_ref, bd_ref,
                y_hbm, h_sc, wgu_sc, wd_sc, o_sc, pend_sc, sem, *, n_f, n_n, n_assign):
    i = pl.program_id(0)
    s = pl.program_id(1)
    rows = nr_ref[i]
    rt = MOE_ROW_TILE
    tile_slabs = rt * ROW_PITCH
    n_tiles = (rows + rt - 1) // rt
    tf = MOE_F_TILE
    last_step = n_f + n_n - 1

    def tile_copy(r, row0):
        return pltpu.make_async_copy(
            o_sc.at[pl.ds(pl.multiple_of(r * tile_slabs, tile_slabs), tile_slabs)],
            y_hbm.at[pl.ds(pl.multiple_of((row0 + r * rt) * ROW_PITCH, 8), tile_slabs)], sem)

    def wait_pending():
        def one(k, carry):
            tile_copy(0, 0).wait()
            return carry

        lax.fori_loop(0, pend_sc[0], one, 0)
        pend_sc[0] = 0

    @pl.when((i == 0) & (s == 0))
    def _reset():
        pend_sc[0] = 0
        o_sc[...] = jnp.zeros(o_sc.shape, F32)

    @pl.when((rows > 0) & (s < n_f))
    def _hidden():
        def project(r):
            return _dot(x_ref[pl.ds(pl.multiple_of(r * rt, rt), rt), :], wgu_sc[...])

        def activate(h, r):
            hg = jnp.minimum(h[:, 0:tf] + bg_ref[0], SWIGLU_LIMIT)
            hu = jnp.clip(h[:, tf:2 * tf] + bu_ref[0], -SWIGLU_LIMIT, SWIGLU_LIMIT)
            h_sc[s, pl.ds(pl.multiple_of(r * rt, rt), rt), :] = _b16(hg * jax.nn.sigmoid(SWIGLU_ALPHA * hg) * (hu + 1.0))

        kc = MOE_K_CHUNK
        h0 = None
        for c in range(wg_ref.shape[1] // kc):
            ks = slice(c * kc, (c + 1) * kc)
            w_c = jnp.concatenate([_b16(wg_ref[0, ks, :]), _b16(wu_ref[0, ks, :])], axis=1)
            wgu_sc[ks, :] = w_c
            part = _dot(x_ref[0:rt, ks], w_c)
            h0 = part if h0 is None else h0 + part
        activate(h0, 0)

        def tile(r, carry):
            activate(project(r), r)
            return carry

        lax.fori_loop(1, n_tiles, tile, 0)

    @pl.when((rows > 0) & (s >= n_f))
    def _down():
        n = s - n_f

        @pl.when(s == n_f)
        def _():
            wait_pending()

        def project(r):
            sl = pl.ds(pl.multiple_of(r * rt, rt), rt)
            acc = _dot(h_sc[0, sl, :], wd_sc[0:tf, :])
            for f in range(1, n_f):
                acc += _dot(h_sc[f, sl, :], wd_sc[f * tf:(f + 1) * tf, :])
            return acc

        def emit(acc, r):
            val = acc + bd_ref[0]
            for jj in range(MOE_N_TILE // LANES):
                start = r * tile_slabs + n * (MOE_N_TILE // LANES) + jj
                o_sc[pl.ds(start, rt, stride=ROW_PITCH), :] = val[:, jj * LANES:(jj + 1) * LANES]

        acc0 = None
        for f in range(n_f):
            fs = slice(f * tf, (f + 1) * tf)
            w_f = _b16(wd_ref[0, fs, :])
            wd_sc[fs, :] = w_f
            part = _dot(h_sc[f, 0:rt, :], w_f)
            acc0 = part if acc0 is None else acc0 + part
        emit(acc0, 0)

        def tile(r, carry):
            emit(project(r), r)
            return carry

        lax.fori_loop(1, n_tiles, tile, 0)

        @pl.when(s == last_step)
        def _():
            row0 = j0_ref[i]

            def send(r, carry):
                tile_copy(r, row0).start()
                return carry

            lax.fori_loop(0, n_tiles, send, 0)
            pend_sc[0] = n_tiles

    @pl.when((i == pl.num_programs(0) - 1) & (s == last_step))
    def _finish():
        wait_pending()
        o_sc[0:tile_slabs, :] = jnp.zeros((tile_slabs, LANES), F32)
        tail = tile_copy(0, n_assign)
        tail.start()
        tail.wait()


def moe_experts(x_sorted, item_expert, item_block, item_rows, item_first, n_assign,
                w_gate_up, b_gate_up, w_down, b_down):
    n_items = item_expert.shape[0]
    d = x_sorted.shape[1]
    n_exp, d_ff, d_out = w_down.shape
    n_f = d_ff // MOE_F_TILE
    n_n = d_out // MOE_N_TILE

    def f_idx(i, s, nr):
        return jnp.minimum(jnp.where(nr[i] > 0, s, n_f - 1), n_f - 1)

    def n_idx(i, s, nr):
        return jnp.where(nr[i] > 0, jnp.maximum(s - n_f, 0), n_n - 1)

    grid_spec = pltpu.PrefetchScalarGridSpec(
        num_scalar_prefetch=4,
        grid=(n_items, n_f + n_n),
        in_specs=[
            pl.BlockSpec((MOE_GROUP, d), lambda i, s, ie, ib, nr, j0: (ib[i], 0)),
            pl.BlockSpec((1, d, MOE_F_TILE), lambda i, s, ie, ib, nr, j0: (ie[i], 0, f_idx(i, s, nr))),
            pl.BlockSpec((1, d, MOE_F_TILE), lambda i, s, ie, ib, nr, j0: (ie[i], 0, n_f + f_idx(i, s, nr))),
            pl.BlockSpec((1, d_ff, MOE_N_TILE), lambda i, s, ie, ib, nr, j0: (ie[i], 0, n_idx(i, s, nr))),
            pl.BlockSpec((1, 1, MOE_F_TILE), lambda i, s, ie, ib, nr, j0: (ie[i], 0, f_idx(i, s, nr))),
            pl.BlockSpec((1, 1, MOE_F_TILE), lambda i, s, ie, ib, nr, j0: (ie[i], 0, n_f + f_idx(i, s, nr))),
            pl.BlockSpec((1, 1, MOE_N_TILE), lambda i, s, ie, ib, nr, j0: (ie[i], 0, n_idx(i, s, nr))),
        ],
        out_specs=pl.BlockSpec(memory_space=pl.ANY),
        scratch_shapes=[pltpu.VMEM((n_f, MOE_GROUP, MOE_F_TILE), BF16),
                        pltpu.VMEM((d, 2 * MOE_F_TILE), BF16),
                        pltpu.VMEM((d_ff, MOE_N_TILE), BF16),
                        pltpu.VMEM((MOE_GROUP * ROW_PITCH, LANES), F32),
                        pltpu.SMEM((1,), jnp.int32),
                        pltpu.SemaphoreType.DMA],
    )
    return pl.pallas_call(
        functools.partial(_moe_kernel, n_f=n_f, n_n=n_n, n_assign=n_assign),
        grid_spec=grid_spec,
        out_shape=jax.ShapeDtypeStruct(((n_assign + MOE_ROW_TILE) * ROW_PITCH, LANES), F32),
        compiler_params=_params("arbitrary", "arbitrary"),
        name="moe_experts",
    )(item_expert, item_block, item_rows, item_first, x_sorted, w_gate_up, w_gate_up, w_down,
      b_gate_up.reshape(n_exp, 1, 2 * d_ff), b_gate_up.reshape(n_exp, 1, 2 * d_ff), b_down.reshape(n_exp, 1, d_out))


def _combine_kernel(pos_ref, nxt_ref, y_hbm, x1_ref, gate_ref, g_ref, b_ref, oa_ref, ob_ref, buf, sem, *, alpha, n_a):
    tm = COMBINE_TOKENS
    t = pl.program_id(0)
    slot = t % 2

    @pl.when(t == 0)
    def _first():
        _start_row_gather(lambda r: pos_ref[0, 0, r], tm * TOP_K, y_hbm, buf.at[0], sem.at[0])

    @pl.when(t + 1 < pl.num_programs(0))
    def _next():
        _start_row_gather(lambda r: nxt_ref[0, 0, r], tm * TOP_K, y_hbm, buf.at[1 - slot], sem.at[1 - slot])

    cur = buf.at[slot]
    _wait_row_gather(tm * TOP_K, y_hbm, cur, sem.at[slot])

    def finish(o_ref):
        d_model = o_ref.shape[1]
        slabs = [slice(j * LANES, (j + 1) * LANES) for j in range(ROW_SLABS)]
        gate = gate_ref[...]
        gk = [jnp.broadcast_to(gate[:, k:k + 1], (tm, LANES)) for k in range(TOP_K)]
        total = jnp.zeros((tm, 1), F32)
        for j, sl in enumerate(slabs):
            y = alpha * x1_ref[pl.ds(j, tm, stride=ROW_PITCH), :]
            for k in range(TOP_K):
                y += gk[k] * cur[pl.ds(k * tm * ROW_PITCH + j, tm, stride=ROW_PITCH), :]
            o_ref[:, sl] = y
            total += jnp.sum(y, axis=1, keepdims=True)
        mean = total * (1.0 / d_model)
        sq = jnp.zeros((tm, 1), F32)
        for sl in slabs:
            d = o_ref[:, sl] - mean
            sq += jnp.sum(d * d, axis=1, keepdims=True)
        rstd = lax.rsqrt(sq * (1.0 / d_model) + LN_EPS)
        for sl in slabs:
            o_ref[:, sl] = (o_ref[:, sl] - mean) * rstd * g_ref[:, sl] + b_ref[:, sl]

    @pl.when(t < n_a)
    def _():
        finish(oa_ref)

    @pl.when(t >= n_a)
    def _():
        finish(ob_ref)


def moe_combine_norm(y_slabs, pos, gate, x1_slabs, ln_g, ln_b, *, alpha, n_first):
    n_tok = pos.shape[0]
    tm = COMBINE_TOKENS
    d = ln_g.shape[1]
    n_t = n_tok // tm
    n_a = n_first // tm
    vec = pl.BlockSpec((1, d), lambda t: (0, 0))
    idx = pos.reshape(n_t, tm, TOP_K).transpose(0, 2, 1).reshape(n_t, 1, tm * TOP_K)
    return pl.pallas_call(
        functools.partial(_combine_kernel, alpha=alpha, n_a=n_a),
        grid=(n_t,),
        in_specs=[pl.BlockSpec((1, 1, tm * TOP_K), lambda t: (t, 0, 0), memory_space=pltpu.SMEM),
                  pl.BlockSpec((1, 1, tm * TOP_K), lambda t: (jnp.minimum(t + 1, n_t - 1), 0, 0),
                               memory_space=pltpu.SMEM),
                  pl.BlockSpec(memory_space=pl.ANY),
                  pl.BlockSpec((tm * ROW_PITCH, LANES), lambda t: (t, 0)),
                  pl.BlockSpec((tm, TOP_K), lambda t: (t, 0)), vec, vec],
        out_specs=[pl.BlockSpec((tm, d), lambda t: (jnp.minimum(t, n_a - 1), 0)),
                   pl.BlockSpec((tm, d), lambda t: (jnp.maximum(t - n_a, 0), 0))],
        out_shape=[jax.ShapeDtypeStruct((n_first, d), F32), jax.ShapeDtypeStruct((n_tok - n_first, d), F32)],
        scratch_shapes=[pltpu.VMEM((2, tm * TOP_K * ROW_PITCH, LANES), F32), pltpu.SemaphoreType.DMA((2,))],
        compiler_params=_params("arbitrary"),
        name="moe_combine",
    )(idx, idx, y_slabs, x1_slabs, gate, ln_g, ln_b)


def _route(logits, n_items):
    n_tok = logits.shape[0]
    n_assign = n_tok * TOP_K
    tiles_per_item = MOE_GROUP // MOE_ROW_TILE
    top_logit, top_idx = lax.top_k(logits, TOP_K)
    gate = jax.nn.softmax(top_logit, axis=-1)
    flat_e = top_idx.reshape(-1).astype(jnp.int32)
    order = jnp.argsort(flat_e).astype(jnp.int32)
    rank = jnp.argsort(order).astype(jnp.int32)
    experts = jnp.arange(N_EXPERTS, dtype=jnp.int32)
    onehot = flat_e[:, None] == experts[None, :]
    counts = jnp.sum(onehot, axis=0, dtype=jnp.int32)
    padded = (counts + MOE_GROUP - 1) // MOE_GROUP * MOE_GROUP
    start = jnp.cumsum(counts) - counts
    padded_end = jnp.cumsum(padded)
    padded_start = padded_end - padded

    def expert_of(row0):
        return jnp.minimum(jnp.sum(row0[:, None] >= padded_end[None, :], axis=1, dtype=jnp.int32), N_EXPERTS - 1)

    def table(values, e):
        return jnp.sum(jnp.where(e[:, None] == experts[None, :], values[None, :], 0), axis=1)

    n_real = (padded_end[-1] // MOE_GROUP).astype(jnp.int32)
    item = jnp.minimum(jnp.arange(n_items, dtype=jnp.int32), jnp.maximum(n_real - 1, 0))
    item_expert = expert_of(item * MOE_GROUP)
    filled = jnp.clip(table(padded_start + counts, item_expert) - item * MOE_GROUP, 0, MOE_GROUP)
    item_rows = jnp.where(jnp.arange(n_items) < n_real, filled, 0).astype(jnp.int32)
    item_first = jnp.clip(table(start - padded_start, item_expert) + item * MOE_GROUP, 0, n_assign - 1)
    tile_row0 = (jnp.arange(tiles_per_item, dtype=jnp.int32) * MOE_ROW_TILE)[None, :]
    tile_valid = jnp.clip(item_rows[:, None] - tile_row0, 0, MOE_ROW_TILE).reshape(-1).astype(jnp.int32)
    tile_first = jnp.clip(item_first[:, None] + tile_row0, 0, n_assign - 1).reshape(-1).astype(jnp.int32)
    return (gate, order, rank.reshape(n_tok, TOP_K), item_expert, item.astype(jnp.int32), item_rows,
            item_first.astype(jnp.int32), tile_valid, tile_first)


def _pad_rw_cols(a, rw_cols, pad):
    lead = a.shape[:-1]
    return jnp.concatenate([a[..., :rw_cols], jnp.zeros(lead + (pad,), a.dtype), a[..., rw_cols:]], axis=-1)


def _cache_rows(c, heads, head_dim):
    b, n_mem = c.shape[:2]
    tiles = head_dim // LANES
    c = c.reshape(b, n_mem, heads, tiles, LANES).transpose(0, 1, 3, 2, 4)
    return c.reshape(b, n_mem * tiles * heads, LANES)


def _layer(xp, xs, mem_prompt, cache_k, cache_v, st_rwkv, st_shift, st_conv, wts, depth):
    (w_in, w_w_up, w0, w_a_up, a0, w_g_up, tshift_mu, k_k, k_a, r_k, gn_g, gn_b, conv_w,
     w_mem_k, w_mem_v, w_proj_a, w_proj_b, w_proj_m, w_o, ln1_g, ln1_b,
     router_w, router_b, w_gate_up, b_gate_up, w_down, b_down, ln2_g, ln2_b) = wts
    bp, sp, d = xp.shape
    bs, ss, _ = xs.shape
    n_mem = mem_prompt.shape[1]
    mem_heads, mem_head = cache_k.shape[-2:]
    n_p, n_s = bp * sp, bs * ss
    n_tok = n_p + n_s
    alpha = (2.0 * depth) ** 0.25
    rw_cols = 3 * d + DECAY_LORA + AICL_LORA + GATE_LORA
    pad = LORA_PAD - (rw_cols - 3 * d)
    rw_pad = rw_cols + pad

    xp2, xs2 = xp.reshape(n_p, d), xs.reshape(n_s, d)
    x_bf = _b16(jnp.concatenate([xp2, xs2], axis=0))
    mu_pad = _pad_rw_cols(tshift_mu, rw_cols, pad)[None, :rw_pad]
    lane_off = rw_cols % LANES
    w_rest = _b16(jnp.pad(w_in[:, rw_cols - lane_off:], ((0, 0), (0, LANES - lane_off))))
    rest_cols = w_in.shape[1] - rw_cols

    p_rw = matmul_f32_weights(x_bf, w_in, rw_pad, BF16, 1024, 512, "in_proj_rw")
    p2 = matmul_shifted_weights(x_bf, w_rest, rest_cols, lane_off, BF16, 1024, 1024, "in_proj_rest")
    prev_s = matmul_f32_weights(st_shift, w_in, rw_pad, F32, bs, 512, "prev_proj").reshape(bs, 1, rw_pad)
    prev_p = jnp.zeros((bp, 1, rw_pad), F32)

    def lora_rows(w, row0):
        return _b16(jnp.zeros((LORA_PAD, d), F32).at[row0:row0 + w.shape[0]].set(w))

    ww = lora_rows(w_w_up, 0)
    wa = lora_rows(w_a_up, DECAY_LORA)
    wg = lora_rows(w_g_up, DECAY_LORA + AICL_LORA)
    r2 = lambda v: v.reshape(1, d)
    rw_vecs = (mu_pad, r2(w0), r2(a0), r2(k_k), r2(k_a), r2(r_k), r2(gn_g), r2(gn_b), ww, wa, wg)
    heads = d // RW_HEAD
    ya_p, rw_p = rwkv_time_mix(p_rw, prev_p, jnp.zeros((bp, heads, RW_HEAD, RW_HEAD), F32), *rw_vecs,
                               batch=bp, seq=sp, row_block0=0, pairs=8, has_state=False)
    ya_s, rw_s = rwkv_time_mix(p_rw, prev_s, st_rwkv, *rw_vecs,
                               batch=bs, seq=ss, row_block0=n_p // RW_ROWS, pairs=4, has_state=True)

    yb_p, cv_p = conv_sequences(p2, conv_w, batch=bp, seq=sp, width=d, rows=256, tn=512)
    yb_s, cv_s0, cv_s1 = conv_short(p2, st_conv[:, 0, :], st_conv[:, 1, :], conv_w, row_block0=n_p // 128,
                                    batch=bs, seq=ss, width=d, rows=128, tn=512)
    cv_s = jnp.stack([cv_s0, cv_s1], axis=1)

    mem_in = _b16(mem_prompt.reshape(bp * n_mem, d))
    mk = matmul(mem_in, _b16(w_mem_k), F32, bp * n_mem, 512, "mem_k")
    mv = matmul(mem_in, _b16(w_mem_v), F32, bp * n_mem, 512, "mem_v")
    q_col0 = 3 * d // mem_head
    ym_p = attention_sequences(p2, q_col0, mk, mv, batch=bp, seq=sp, n_mem=n_mem, heads=mem_heads,
                               head_dim=mem_head, tq=512)
    bb = 4
    ym_s = attention_cache(p2, 3 * d // d, n_p // (bb * ss), _cache_rows(cache_k, mem_heads, mem_head),
                           _cache_rows(cache_v, mem_heads, mem_head),
                           batch=bs, seq=ss, heads=mem_heads, head_dim=mem_head, bb=bb)

    merged = merge_branches((ya_p, yb_p, ym_p), (ya_s, yb_s, ym_s), p2, 4 * d // 512, _b16(w_proj_a),
                            _b16(w_proj_b), _b16(w_proj_m), tm=512, tn=512)
    rw_pad_r = jnp.zeros((d, LANES), F32).at[:, :N_EXPERTS].set(router_w)
    rb_pad_r = jnp.zeros((1, LANES), F32).at[0, :N_EXPERTS].set(router_b)
    x1_slabs, logits = project_norm_route(merged, xp2, xs2, _b16(w_o), r2(ln1_g), r2(ln1_b),
                                          rw_pad_r, rb_pad_r, alpha=alpha, tm=256)

    n_items = n_tok * TOP_K // MOE_GROUP + N_EXPERTS
    (gate, order, rank, item_expert, item_block, item_rows, item_first, tile_valid,
     tile_first) = _route(logits[:, :N_EXPERTS], n_items)
    x_sorted = moe_dispatch(x1_slabs, order, tile_valid, tile_first, d)
    y_slabs = moe_experts(x_sorted, item_expert, item_block, item_rows, item_first, n_tok * TOP_K,
                          w_gate_up, b_gate_up, w_down, b_down)
    y_p, y_s = moe_combine_norm(y_slabs, rank, gate, x1_slabs, r2(ln2_g), r2(ln2_b), alpha=alpha, n_first=n_p)
    y_p = y_p.reshape(bp, sp, d)
    y_s = y_s.reshape(bs, ss, d)
    mk5 = mk.reshape(bp, n_mem, mem_heads, mem_head)
    mv5 = mv.reshape(bp, n_mem, mem_heads, mem_head)
    return y_p, y_s, mk5, mv5, rw_p, xp[:, -1], cv_p, rw_s, xs[:, -1], cv_s


def kernel(x_prompt, x_sample, mem_prompt, cache_mem_k, cache_mem_v, state_rwkv, state_shift, state_conv, w_in, w_w_up, w0, w_a_up, a0, w_g_up, tshift_mu, k_k, k_a, r_k, gn_g, gn_b, conv_w, w_mem_k, w_mem_v, w_proj_a, w_proj_b, w_proj_m, w_o, ln1_g, ln1_b, router_w, router_b, w_gate_up, b_gate_up, w_down, b_down, ln2_g, ln2_b):
    weights = (w_in, w_w_up, w0, w_a_up, a0, w_g_up, tshift_mu, k_k, k_a, r_k, gn_g, gn_b, conv_w,
               w_mem_k, w_mem_v, w_proj_a, w_proj_b, w_proj_m, w_o, ln1_g, ln1_b,
               router_w, router_b, w_gate_up, b_gate_up, w_down, b_down, ln2_g, ln2_b)
    depth = w_in.shape[0]
    yp, ys = x_prompt, x_sample
    outs = [[] for _ in range(8)]
    for l in range(depth):
        res = _layer(yp, ys, mem_prompt, cache_mem_k[l], cache_mem_v[l], state_rwkv[l], state_shift[l],
                     state_conv[l], tuple(w[l] for w in weights), depth)
        yp, ys = res[0], res[1]
        for acc, r in zip(outs, res[2:]):
            acc.append(r)
    return (yp, ys) + tuple(jnp.stack(o) for o in outs)
```

```python
import functools
import math

import jax
import jax.numpy as jnp
from jax import lax
from jax.experimental import pallas as pl
from jax.experimental.pallas import tpu as pltpu

F32 = jnp.float32
BF16 = jnp.bfloat16

LANES = 128
ROW_SLABS = 16
ROW_PITCH = 24
RW_HEAD = 64
HEADS_PER_BLOCK = LANES // RW_HEAD
RW_ROWS = 64
DECAY_LORA = 96
AICL_LORA = 96
GATE_LORA = 256
LORA_PAD = 512
DECAY_SCALE = math.exp(-0.5)
GN_EPS = 64e-5
LN_EPS = 1e-5
N_EXPERTS = 32
TOP_K = 4
SWIGLU_LIMIT = 7.0
SWIGLU_ALPHA = 1.702
VMEM_LIMIT = 56 * 1024 * 1024


def _dot(a, b):
    return jnp.dot(a, b, preferred_element_type=F32)


def _dot_nt(a, b):
    return lax.dot_general(a, b, (((1,), (1,)), ((), ())), preferred_element_type=F32)


def _dot_tn(a, b):
    return lax.dot_general(a, b, (((0,), (0,)), ((), ())), preferred_element_type=F32)


def _b16(x):
    return x.astype(BF16)


def _params(*sem):
    return pltpu.CompilerParams(dimension_semantics=sem, vmem_limit_bytes=VMEM_LIMIT)


def _split3(x):
    hi = _b16(x)
    r1 = x - hi.astype(F32)
    mid = _b16(r1)
    lo = _b16(r1 - mid.astype(F32))
    return hi, mid, lo


def _rwkv_kernel(pr_ref, pk_ref, pv_ref, pl_ref, qr_ref, qk_ref, qv_ref, ql_ref,
                 mur_ref, muk_ref, muv_ref, mul_ref,
                 w0_ref, a0_ref, kk_ref, ka_ref, rk_ref, gng_ref, gnb_ref,
                 ww_ref, wa_ref, wg_ref, s0_ref,
                 y_ref, sout_ref, *scratch, n_seq, n_chunks, pairs, has_state):
    C = RW_ROWS
    L = C // n_seq
    R = HEADS_PER_BLOCK * C
    W = pairs * LANES
    c = pl.program_id(2)
    carried = n_chunks > 1
    if carried:
        s_sc, cr_sc, ck_sc, cv_sc, cl_sc = scratch

        @pl.when(c == 0)
        def _init_state():
            for p in range(pairs):
                if has_state:
                    s_sc[p] = s0_ref[0, 2 * p:2 * p + 2].reshape(LANES, RW_HEAD)
                else:
                    s_sc[p] = jnp.zeros((LANES, RW_HEAD), F32)

    def prev_rows(q_ref, carry_sc, width):
        if carried:
            @pl.when(c == 0)
            def _():
                carry_sc[0:1, :] = q_ref[0]
            return jnp.broadcast_to(carry_sc[0:1, :], (C, width))
        q = q_ref[...]
        return jnp.broadcast_to(q, (n_seq, L, width)).reshape(C, width)

    def shifted_lerp(p_ref, q_ref, carry_sc, mu_ref, width):
        p = p_ref[...].astype(F32)
        pos = lax.broadcasted_iota(jnp.int32, (C, width), 0) % L
        prev = jnp.where(pos == 0, prev_rows(q_ref, carry_sc, width), pltpu.roll(p, 1, axis=0))
        if carried:
            carry_sc[0:1, :] = p[C - 1:C, :]
        return p + (prev - p) * mu_ref[...]

    zr = shifted_lerp(pr_ref, qr_ref, cr_sc if carried else None, mur_ref, W)
    zk = shifted_lerp(pk_ref, qk_ref, ck_sc if carried else None, muk_ref, W)
    zv = shifted_lerp(pv_ref, qv_ref, cv_sc if carried else None, muv_ref, W)
    zl = shifted_lerp(pl_ref, ql_ref, cl_sc if carried else None, mul_ref, LORA_PAD)

    lw = -DECAY_SCALE * jax.nn.sigmoid(w0_ref[...] + _dot(_b16(jnp.tanh(zl)), ww_ref[...]))
    a_all = jax.nn.sigmoid(a0_ref[...] + _dot(_b16(zl), wa_ref[...]))
    g_all = _dot(_b16(jax.nn.sigmoid(zl)), wg_ref[...])

    ti = lax.broadcasted_iota(jnp.int32, (2 * C, C), 0)
    tj = lax.broadcasted_iota(jnp.int32, (2 * C, C), 1)
    same_seq = (ti % C) // L == tj // L
    cum_lhs = _b16((same_seq & ((ti >= C) | (tj <= ti))).astype(F32))
    hi, mid, lo = _split3(lw)
    cum = _dot(cum_lhs, jnp.concatenate([hi, mid, lo], axis=1))
    cum = cum[:, 0:W] + cum[:, W:2 * W] + cum[:, 2 * W:3 * W]
    cw_all = cum[0:C]
    tot_all = cum[C:2 * C]

    lb_r = lax.broadcasted_iota(jnp.int32, (2 * LANES, LANES), 0) % LANES // RW_HEAD
    lb_c = lax.broadcasted_iota(jnp.int32, (2 * LANES, LANES), 1) // RW_HEAD
    head_ones2 = _b16((lb_r == lb_c).astype(F32))

    def head_sum(x):
        xh = _b16(x)
        xl = _b16(x - xh.astype(F32))
        return _dot(jnp.concatenate([xh, xl], axis=1), head_ones2)

    head0 = lax.broadcasted_iota(jnp.int32, (n_seq, L, LANES), 2) < RW_HEAD

    def stack(x):
        x3 = x.reshape(n_seq, L, LANES)
        return jnp.concatenate([jnp.where(head0, x3, 0.0), jnp.where(head0, 0.0, x3)], axis=1).reshape(R, LANES)

    def unstack(x):
        x3 = x.reshape(n_seq, 2 * L, LANES)
        return (x3[:, 0:L, :] + x3[:, L:2 * L, :]).reshape(C, LANES)

    si = lax.broadcasted_iota(jnp.int32, (R, R), 0)
    sj = lax.broadcasted_iota(jnp.int32, (R, R), 1)
    same = (si // L) == (sj // L)
    strict = same & (si > sj)
    incl = same & (si >= sj)
    eye = (si == sj).astype(F32)
    row_head = (lax.broadcasted_iota(jnp.int32, (R, LANES), 0) // L) % HEADS_PER_BLOCK
    lane_head = lax.broadcasted_iota(jnp.int32, (R, LANES), 1) // RW_HEAD
    head_match = row_head == lane_head
    state_row_head0 = lax.broadcasted_iota(jnp.int32, (LANES, LANES), 0) < RW_HEAD
    n_lvl = int(math.log2(L)) - 1
    S2 = 2 * L

    P = range(pairs)
    lsl = [slice(p * LANES, (p + 1) * LANES) for p in P]

    def rows_cat(xs):
        return jnp.concatenate(xs, axis=0) if len(xs) > 1 else xs[0]

    kk_raw = [zk[:, ls] * kk_ref[:, ls] for ls in lsl]
    kk_n2 = head_sum(rows_cat([x * x for x in kk_raw]))
    kk_l = [kk_raw[p] / jnp.maximum(jnp.sqrt(kk_n2[p * C:(p + 1) * C]), 1e-12) for p in P]
    kmod_l = [zk[:, ls] * (1.0 + (a_all[:, ls] - 1.0) * ka_ref[:, ls]) for ls in lsl]
    beta_l = [kk_l[p] * a_all[:, lsl[p]] for p in P]
    e_neg_l = [jnp.exp(-cw_all[:, ls]) for ls in lsl]
    e_tail_l = [jnp.exp(tot_all[:, ls] - cw_all[:, ls]) for ls in lsl]
    kk_s_l = [stack(kk_l[p] * jnp.exp(cw_all[:, lsl[p]] - lw[:, lsl[p]])) for p in P]
    r_s_l = [stack(zr[:, ls] * jnp.exp(cw_all[:, ls])) for ls in lsl]
    v_b_l = [_b16(stack(zv[:, ls])) for ls in lsl]
    bw_b_l = [_b16(stack(beta_l[p] * e_tail_l[p])) for p in P]
    kw_b_l = [_b16(stack(kmod_l[p] * e_tail_l[p])) for p in P]
    a_lhs = [_b16(jnp.concatenate([kk_s_l[p], r_s_l[p]], axis=0)) for p in P]
    a_rhs = [_b16(jnp.concatenate([stack(beta_l[p] * e_neg_l[p]), stack(kmod_l[p] * e_neg_l[p])], axis=0)) for p in P]

    amat_l = [_dot_nt(a_lhs[p], a_rhs[p]) for p in P]
    a_ak_b = [_b16(jnp.where(strict, m[0:R, R:2 * R], 0.0)) for m in amat_l]
    a_r_b = [_b16(jnp.concatenate([jnp.where(incl, m[R:2 * R, 0:R], 0.0),
                                    jnp.where(incl, m[R:2 * R, R:2 * R], 0.0)], axis=1)) for m in amat_l]

    x_l = [jnp.where(strict, -m[0:R, 0:R], 0.0) for m in amat_l]
    minv_l = [eye + x for x in x_l]
    xb_l = [_b16(x) for x in x_l]
    cur_l = [_dot(xb, xb) for xb in xb_l]
    akv_l = [_dot(a_ak_b[p], v_b_l[p]) for p in P]
    for lvl in range(n_lvl):
        cb_l = [_b16(cur) for cur in cur_l]
        if lvl < n_lvl - 1:
            both_l = [_dot(_b16(jnp.concatenate([minv_l[p], cur_l[p]], axis=0)), cb_l[p]) for p in P]
            minv_l = [minv_l[p] + both_l[p][0:R] for p in P]
            cur_l = [both[R:2 * R] for both in both_l]
        else:
            minv_l = [minv_l[p] + _dot(_b16(minv_l[p]), cb_l[p]) for p in P]

    pq_l = [_dot(_b16(minv_l[p]), _b16(jnp.concatenate([kk_s_l[p], akv_l[p]], axis=1))) for p in P]

    def fold_lanes(x):
        return _b16(x + pltpu.roll(x, RW_HEAD, axis=1))[:, 0:RW_HEAD]

    p_fold = [fold_lanes(-pq[:, 0:LANES]) for pq in pq_l]
    r_fold = [fold_lanes(r_s) for r_s in r_s_l]
    seqs = [(p, b) for p in P for b in range(n_seq)]

    def state_in(p, b):
        if carried:
            return s_sc[p]
        return s0_ref[b, 2 * p:2 * p + 2].reshape(LANES, RW_HEAD)

    states = {pb: state_in(*pb) for pb in seqs}
    sdot = {(p, b): _dot_nt(jnp.concatenate([p_fold[p][b * S2:(b + 1) * S2], r_fold[p][b * S2:(b + 1) * S2]], axis=0),
                            _b16(states[(p, b)])) for (p, b) in seqs}
    u_b_l, rs_l = [], []
    for p in P:
        us = [jnp.where(head_match[b * S2:(b + 1) * S2], sdot[(p, b)][0:S2], 0.0) for b in range(n_seq)]
        rs = [jnp.where(head_match[b * S2:(b + 1) * S2], sdot[(p, b)][S2:2 * S2], 0.0) for b in range(n_seq)]
        u_b_l.append(_b16(rows_cat(us) - pq_l[p][:, LANES:2 * LANES]))
        rs_l.append(rows_cat(rs))

    y_s_l = [rs_l[p] + _dot(a_r_b[p], jnp.concatenate([u_b_l[p], v_b_l[p]], axis=0)) for p in P]
    z_l = {(p, b): _dot_tn(jnp.concatenate([u_b_l[p][b * S2:(b + 1) * S2], v_b_l[p][b * S2:(b + 1) * S2]], axis=0),
                           jnp.concatenate([bw_b_l[p][b * S2:(b + 1) * S2], kw_b_l[p][b * S2:(b + 1) * S2]], axis=0))
           for (p, b) in seqs}
    for (p, b) in seqs:
        z = z_l[(p, b)]
        z = (z + pltpu.roll(z, RW_HEAD, axis=1))[:, 0:RW_HEAD]
        wt = jnp.broadcast_to(jnp.exp(tot_all[b * L:b * L + 1, lsl[p]]), (LANES, LANES))
        wnat = jnp.where(state_row_head0, wt, pltpu.roll(wt, RW_HEAD, axis=1))[:, 0:RW_HEAD]
        s_new = states[(p, b)] * wnat + z
        if carried:
            s_sc[p] = s_new
        sout_ref[b, 2 * p:2 * p + 2] = s_new.reshape(HEADS_PER_BLOCK, RW_HEAD, RW_HEAD)

    y_l = [unstack(y_s) for y_s in y_s_l]
    stats = head_sum(rows_cat(y_l + [zr[:, ls] * kmod_l[p] * rk_ref[:, ls] for p, ls in enumerate(lsl)]))
    d_l = [y_l[p] - stats[p * C:(p + 1) * C] * (1.0 / RW_HEAD) for p in P]
    var = head_sum(rows_cat([d * d for d in d_l])) * (1.0 / RW_HEAD)
    for p, ls in enumerate(lsl):
        yn = d_l[p] * lax.rsqrt(var[p * C:(p + 1) * C] + GN_EPS) * gng_ref[:, ls] + gnb_ref[:, ls]
        bonus = stats[(pairs + p) * C:(pairs + p + 1) * C] * zv[:, ls]
        y_ref[:, ls] = ((yn + bonus) * g_all[:, ls]).astype(y_ref.dtype)


def rwkv_time_mix(p_rw, prev_rw, s0, mu, w0, a0, k_k, k_a, r_k, gn_g, gn_b, ww, wa, wg,
                  *, batch, seq, row_block0, pairs, has_state):
    width = w0.shape[-1]
    wp = pairs * LANES
    n_blk = width // wp
    heads = width // RW_HEAD
    hp = HEADS_PER_BLOCK * pairs
    lora_blk = 3 * width // LORA_PAD
    if seq >= RW_ROWS:
        n_seq, n_chunks, n_outer = 1, seq // RW_ROWS, batch
    else:
        n_seq, n_chunks, n_outer = RW_ROWS // seq, 1, batch * seq // RW_ROWS

    def tok(col0):
        return pl.BlockSpec((RW_ROWS, wp), lambda b, h, c: (row_block0 + b * n_chunks + c, col0 + h))

    def prev(col0):
        return pl.BlockSpec((n_seq, 1, wp), lambda b, h, c: (b, 0, col0 + h))

    def vec(col0):
        return pl.BlockSpec((1, wp), lambda b, h, c: (0, col0 + h))

    lora_w = pl.BlockSpec((LORA_PAD, wp), lambda b, h, c: (0, h))
    state = pl.BlockSpec((n_seq, hp, RW_HEAD, RW_HEAD), lambda b, h, c: (b, h, 0, 0))
    in_specs = [
        tok(0), tok(n_blk), tok(2 * n_blk),
        pl.BlockSpec((RW_ROWS, LORA_PAD), lambda b, h, c: (row_block0 + b * n_chunks + c, lora_blk)),
        prev(0), prev(n_blk), prev(2 * n_blk),
        pl.BlockSpec((n_seq, 1, LORA_PAD), lambda b, h, c: (b, 0, lora_blk)),
        vec(0), vec(n_blk), vec(2 * n_blk),
        pl.BlockSpec((1, LORA_PAD), lambda b, h, c: (0, lora_blk)),
        vec(0), vec(0), vec(0), vec(0), vec(0), vec(0), vec(0),
        lora_w, lora_w, lora_w, state,
    ]
    out_specs = [pl.BlockSpec((RW_ROWS, wp), lambda b, h, c: (b * n_chunks + c, h)), state]
    scratch = []
    if n_chunks > 1:
        scratch = [pltpu.VMEM((pairs, LANES, RW_HEAD), F32), pltpu.VMEM((8, wp), F32), pltpu.VMEM((8, wp), F32),
                   pltpu.VMEM((8, wp), F32), pltpu.VMEM((8, LORA_PAD), F32)]
    kern = functools.partial(_rwkv_kernel, n_seq=n_seq, n_chunks=n_chunks, pairs=pairs, has_state=has_state)
    return pl.pallas_call(
        kern,
        grid=(n_outer, n_blk, n_chunks),
        in_specs=in_specs,
        out_specs=out_specs,
        out_shape=[jax.ShapeDtypeStruct((batch * seq, width), BF16),
                   jax.ShapeDtypeStruct((batch, heads, RW_HEAD, RW_HEAD), F32)],
        scratch_shapes=scratch,
        compiler_params=_params("parallel", "parallel", "arbitrary"),
        name=f"rwkv_l{min(seq, RW_ROWS)}",
    )(p_rw, p_rw, p_rw, p_rw, prev_rw, prev_rw, prev_rw, prev_rw, mu, mu, mu, mu,
      w0, a0, k_k, k_a, r_k, gn_g, gn_b, ww, wa, wg, s0)


def _mm_kernel(x_ref, w_ref, o_ref):
    o_ref[...] = _dot(_b16(x_ref[...]), w_ref[...]).astype(o_ref.dtype)


def matmul(x, w, out_dtype, tm, tn, name):
    m, kd = x.shape
    n = w.shape[1]
    return pl.pallas_call(
        _mm_kernel,
        grid=(n // tn, m // tm),
        in_specs=[pl.BlockSpec((tm, kd), lambda j, i: (i, 0)),
                  pl.BlockSpec((kd, tn), lambda j, i: (0, j))],
        out_specs=pl.BlockSpec((tm, tn), lambda j, i: (i, j)),
        out_shape=jax.ShapeDtypeStruct((m, n), out_dtype),
        compiler_params=_params("parallel", "parallel"),
        name=name,
    )(x, w)


def _mm_cast_kernel(x_ref, w_ref, o_ref, w_sc):
    @pl.when(pl.program_id(1) == 0)
    def _():
        w_sc[...] = _b16(w_ref[...])

    o_ref[...] = _dot(_b16(x_ref[...]), w_sc[...]).astype(o_ref.dtype)


def _mm_shift_kernel(x_ref, *refs, shift):
    w_refs, o_ref, w_sc = refs[:-2], refs[-2], refs[-1]

    @pl.when(pl.program_id(1) == 0)
    def _():
        tn = w_sc.shape[1]
        a = jnp.concatenate([r[...] for r in w_refs[:-1]], axis=1)
        a = pltpu.roll(a, tn - shift, axis=1)
        b = pltpu.roll(w_refs[-1][...], LANES - shift, axis=1)
        lane = lax.broadcasted_iota(jnp.int32, b.shape, 1)
        w_sc[:, 0:tn - LANES] = _b16(a[:, 0:tn - LANES])
        w_sc[:, tn - LANES:tn] = _b16(jnp.where(lane < LANES - shift, a[:, tn - LANES:tn], b))

    o_ref[...] = _dot(x_ref[...], w_sc[...]).astype(o_ref.dtype)


def matmul_shifted_weights(x, w, col0, n_cols, out_dtype, tm, tn, name):
    m, kd = x.shape
    blk0, shift = divmod(col0, LANES)
    per_tile = tn // LANES
    windows = [pl.BlockSpec((kd, LANES), functools.partial(lambda j, i, k: (0, blk0 + j * per_tile + k), k=k))
               for k in range(per_tile + 1)]
    return pl.pallas_call(
        functools.partial(_mm_shift_kernel, shift=shift),
        grid=(n_cols // tn, m // tm),
        in_specs=[pl.BlockSpec((tm, kd), lambda j, i: (i, 0))] + windows,
        out_specs=pl.BlockSpec((tm, tn), lambda j, i: (i, j)),
        out_shape=jax.ShapeDtypeStruct((m, n_cols), out_dtype),
        scratch_shapes=[pltpu.VMEM((kd, tn), BF16)],
        compiler_params=_params("parallel", "arbitrary"),
        name=name,
    )(x, *([w] * (per_tile + 1)))


def matmul_f32_weights(x, w, n_cols, out_dtype, tm, tn, name):
    m, kd = x.shape
    return pl.pallas_call(
        _mm_cast_kernel,
        grid=(n_cols // tn, m // tm),
        in_specs=[pl.BlockSpec((tm, kd), lambda j, i: (i, 0)),
                  pl.BlockSpec((kd, tn), lambda j, i: (0, j))],
        out_specs=pl.BlockSpec((tm, tn), lambda j, i: (i, j)),
        out_shape=jax.ShapeDtypeStruct((m, n_cols), out_dtype),
        scratch_shapes=[pltpu.VMEM((kd, tn), BF16)],
        compiler_params=_params("parallel", "arbitrary"),
        name=name,
    )(x, w)


def _conv_taps(u, u1, u2, bg_ref, cw_ref, y_ref):
    y = cw_ref[0:1, :] * u2 + cw_ref[1:2, :] * u1 + cw_ref[2:3, :] * u
    y_ref[...] = (bg_ref[...].astype(F32) * y).astype(y_ref.dtype)


def _conv_seq_kernel(bg_ref, cg_ref, xi_ref, cw_ref, y_ref, new_ref, carry_sc, *, rows, n_t):
    i = pl.program_id(2)

    @pl.when(i == 0)
    def _init():
        carry_sc[...] = jnp.zeros(carry_sc.shape, F32)

    u = cg_ref[...].astype(F32) * xi_ref[...].astype(F32)
    r = lax.broadcasted_iota(jnp.int32, u.shape, 0)
    c0 = carry_sc[0:1, :]
    c1 = carry_sc[1:2, :]
    u1 = jnp.where(r == 0, c1, pltpu.roll(u, 1, axis=0))
    u2 = jnp.where(r == 0, c0, jnp.where(r == 1, c1, pltpu.roll(u, 2, axis=0)))
    _conv_taps(u, u1, u2, bg_ref, cw_ref, y_ref)
    carry_sc[0:2, :] = u[rows - 2:rows, :]

    @pl.when(i == n_t - 1)
    def _fin():
        new_ref[0] = u[rows - 2:rows, :]


def conv_sequences(p2, conv_w, *, batch, seq, width, rows, tn):
    nb = width // tn
    n_t = seq // rows

    def tok(sec):
        return pl.BlockSpec((rows, tn), lambda b, j, i: (b * n_t + i, sec * nb + j))

    return pl.pallas_call(
        functools.partial(_conv_seq_kernel, rows=rows, n_t=n_t),
        grid=(batch, nb, n_t),
        in_specs=[tok(0), tok(1), tok(2), pl.BlockSpec((3, tn), lambda b, j, i: (0, j))],
        out_specs=[pl.BlockSpec((rows, tn), lambda b, j, i: (b * n_t + i, j)),
                   pl.BlockSpec((1, 2, tn), lambda b, j, i: (b, 0, j))],
        out_shape=[jax.ShapeDtypeStruct((batch * seq, width), BF16),
                   jax.ShapeDtypeStruct((batch, 2, width), F32)],
        scratch_shapes=[pltpu.VMEM((8, tn), F32)],
        compiler_params=_params("parallel", "parallel", "arbitrary"),
        name="conv_seq",
    )(p2, p2, p2, conv_w)


def _conv_short_kernel(bg_ref, cg_ref, xi_ref, p0_ref, p1_ref, cw_ref, y_ref, n0_ref, n1_ref, *, seq):
    u = cg_ref[...].astype(F32) * xi_ref[...].astype(F32)
    rows, tn = u.shape
    n_seq = rows // seq

    def per_seq(p_ref):
        return jnp.broadcast_to(p_ref[...].reshape(n_seq, 1, tn), (n_seq, seq, tn)).reshape(rows, tn)

    h0, h1 = per_seq(p0_ref), per_seq(p1_ref)
    pos = lax.broadcasted_iota(jnp.int32, u.shape, 0) % seq
    u1 = jnp.where(pos == 0, h1, pltpu.roll(u, 1, axis=0))
    u2 = jnp.where(pos == 0, h0, jnp.where(pos == 1, h1, pltpu.roll(u, 2, axis=0)))
    _conv_taps(u, u1, u2, bg_ref, cw_ref, y_ref)
    u3 = u.reshape(n_seq, seq, tn)
    n0_ref[...] = u3[:, seq - 2, :]
    n1_ref[...] = u3[:, seq - 1, :]


def conv_short(p2, hist0, hist1, conv_w, *, row_block0, batch, seq, width, rows, tn):
    nb = width // tn
    n_seq = rows // seq
    n_r = batch * seq // rows

    def tok(sec):
        return pl.BlockSpec((rows, tn), lambda i, j: (row_block0 + i, sec * nb + j))

    per_seq = pl.BlockSpec((n_seq, tn), lambda i, j: (i, j))
    return pl.pallas_call(
        functools.partial(_conv_short_kernel, seq=seq),
        grid=(n_r, nb),
        in_specs=[tok(0), tok(1), tok(2), per_seq, per_seq, pl.BlockSpec((3, tn), lambda i, j: (0, j))],
        out_specs=[pl.BlockSpec((rows, tn), lambda i, j: (i, j)), per_seq, per_seq],
        out_shape=[jax.ShapeDtypeStruct((batch * seq, width), BF16),
                   jax.ShapeDtypeStruct((batch, width), F32), jax.ShapeDtypeStruct((batch, width), F32)],
        compiler_params=_params("parallel", "parallel"),
        name="conv_short",
    )(p2, p2, p2, hist0, hist1, conv_w)


def _softmax_rows(s):
    m = jnp.max(s, axis=-1, keepdims=True)
    e = jnp.exp(s - m)
    return e / jnp.sum(e, axis=-1, keepdims=True)


def _attn_seq_kernel(q_ref, k_ref, v_ref, o_ref, *, scale):
    s = _dot_nt(q_ref[...], _b16(k_ref[...])) * scale
    p = _softmax_rows(s)
    o_ref[...] = _dot(_b16(p), _b16(v_ref[...])).astype(o_ref.dtype)


def attention_sequences(p2, q_col0, mem_k, mem_v, *, batch, seq, n_mem, heads, head_dim, tq):
    n_t = seq // tq
    kv = pl.BlockSpec((n_mem, head_dim), lambda b, h, i: (b, h))
    return pl.pallas_call(
        functools.partial(_attn_seq_kernel, scale=head_dim ** -0.5),
        grid=(batch, heads, n_t),
        in_specs=[pl.BlockSpec((tq, head_dim), lambda b, h, i: (b * n_t + i, q_col0 + h)), kv, kv],
        out_specs=pl.BlockSpec((tq, head_dim), lambda b, h, i: (b * n_t + i, h)),
        out_shape=jax.ShapeDtypeStruct((batch * seq, heads * head_dim), BF16),
        compiler_params=_params("parallel", "parallel", "arbitrary"),
        name="attn_seq",
    )(p2, mem_k, mem_v)


def _attn_cache_kernel(q_ref, k_ref, v_ref, o_ref, *, scale, bb, seq, heads, head_dim):
    tiles = head_dim // LANES
    slabs = tiles * heads
    n_mem = k_ref.shape[1] // slabs
    q = q_ref[...].astype(F32)

    def head_slab(ref, b, h):
        return jnp.concatenate([ref[b, pl.ds(t * heads + h, n_mem, stride=slabs), :] for t in range(tiles)], axis=1)

    rows = []
    for b in range(bb):
        cols = []
        for h in range(heads):
            qb = _b16(q[b * seq:(b + 1) * seq, h * head_dim:(h + 1) * head_dim])
            s = _dot_nt(qb, _b16(head_slab(k_ref, b, h))) * scale
            p = _softmax_rows(s)
            cols.append(_dot(_b16(p), _b16(head_slab(v_ref, b, h))))
        rows.append(jnp.concatenate(cols, axis=1))
    o_ref[...] = jnp.concatenate(rows, axis=0).astype(o_ref.dtype)


def attention_cache(p2, q_blk0, row0, cache_k, cache_v, *, batch, seq, heads, head_dim, bb):
    d = heads * head_dim
    kv = pl.BlockSpec((bb,) + cache_k.shape[1:], lambda g: (g, 0, 0))
    return pl.pallas_call(
        functools.partial(_attn_cache_kernel, scale=head_dim ** -0.5, bb=bb, seq=seq, heads=heads, head_dim=head_dim),
        grid=(batch // bb,),
        in_specs=[pl.BlockSpec((bb * seq, d), lambda g: (row0 + g, q_blk0)), kv, kv],
        out_specs=pl.BlockSpec((bb * seq, d), lambda g: (g, 0)),
        out_shape=jax.ShapeDtypeStruct((batch * seq, d), BF16),
        compiler_params=_params("parallel"),
        name="attn_cache",
    )(p2, cache_k, cache_v)


def _merge_kernel(ya1_ref, yb1_ref, ym1_ref, ya2_ref, yb2_ref, ym2_ref, g0_ref, g1_ref, g2_ref,
                  wa_ref, wb_ref, wm_ref, o_ref, *, n_first):
    def merge(ya_ref, yb_ref, ym_ref):
        acc = jax.nn.sigmoid(g0_ref[...].astype(F32)) * _dot(_b16(ya_ref[...]), wa_ref[...])
        acc += jax.nn.sigmoid(g1_ref[...].astype(F32)) * _dot(_b16(yb_ref[...]), wb_ref[...])
        acc += jax.nn.sigmoid(g2_ref[...].astype(F32)) * _dot(_b16(ym_ref[...]), wm_ref[...])
        o_ref[...] = acc.astype(o_ref.dtype)

    i = pl.program_id(1)

    @pl.when(i < n_first)
    def _():
        merge(ya1_ref, yb1_ref, ym1_ref)

    @pl.when(i >= n_first)
    def _():
        merge(ya2_ref, yb2_ref, ym2_ref)


def merge_branches(first, second, p2, gate_col0, wa, wb, wm, *, tm, tn):
    m1, d = first[0].shape
    m2 = second[0].shape[0]
    n1 = m1 // tm
    nb = d // tn
    act1 = pl.BlockSpec((tm, d), lambda j, i: (jnp.minimum(i, n1 - 1), 0))
    act2 = pl.BlockSpec((tm, d), lambda j, i: (jnp.maximum(i - n1, 0), 0))
    wsp = pl.BlockSpec((d, tn), lambda j, i: (0, j))

    def gate(k):
        return pl.BlockSpec((tm, tn), lambda j, i: (i, gate_col0 + k * nb + j))

    return pl.pallas_call(
        functools.partial(_merge_kernel, n_first=n1),
        grid=(nb, (m1 + m2) // tm),
        in_specs=[act1, act1, act1, act2, act2, act2, gate(0), gate(1), gate(2), wsp, wsp, wsp],
        out_specs=pl.BlockSpec((tm, tn), lambda j, i: (i, j)),
        out_shape=jax.ShapeDtypeStruct((m1 + m2, d), BF16),
        compiler_params=_params("parallel", "parallel"),
        name="merge",
    )(*first, *second, p2, p2, p2, wa, wb, wm)


def _layer_norm(y, g, b):
    mu = jnp.mean(y, axis=-1, keepdims=True)
    d = y - mu
    var = jnp.mean(d * d, axis=-1, keepdims=True)
    return d * lax.rsqrt(var + LN_EPS) * g + b


def _to_slabs(ref, row0, x):
    rows = x.shape[0]
    for j in range(ROW_SLABS):
        ref[pl.ds(row0 * ROW_PITCH + j, rows, stride=ROW_PITCH), :] = x[:, j * LANES:(j + 1) * LANES]
    for j in range(ROW_SLABS, ROW_PITCH):
        ref[pl.ds(row0 * ROW_PITCH + j, rows, stride=ROW_PITCH), :] = jnp.zeros((rows, LANES), x.dtype)


def _from_slabs(ref, row0, rows):
    return jnp.concatenate(
        [ref[pl.ds(row0 * ROW_PITCH + j, rows, stride=ROW_PITCH), :] for j in range(ROW_SLABS)], axis=1)


def _proj_ln_kernel(m_ref, xa_ref, xb_ref, wo_ref, g_ref, b_ref, rw_ref, rb_ref, x1_ref, lg_ref, *, alpha, n_first):
    h = _dot(m_ref[...], wo_ref[...])
    x = jnp.where(pl.program_id(0) < n_first, xa_ref[...], xb_ref[...])
    x1 = _layer_norm(alpha * x + h, g_ref[...], b_ref[...])
    _to_slabs(x1_ref, 0, x1)
    hi, mid, lo = _split3(x1)
    w_hi, w_mid, w_lo = _split3(rw_ref[...])
    lg = _dot(hi, w_hi) + (_dot(hi, w_mid) + _dot(mid, w_hi)) + (_dot(hi, w_lo) + _dot(mid, w_mid) + _dot(lo, w_hi))
    lg_ref[...] = lg + rb_ref[...]


def project_norm_route(merged, xa, xb, wo, ln_g, ln_b, router_w, router_b, *, alpha, tm):
    m, d = merged.shape
    n_r = router_w.shape[1]
    n1 = xa.shape[0] // tm
    row = pl.BlockSpec((tm, d), lambda i: (i, 0))
    vec = pl.BlockSpec((1, d), lambda i: (0, 0))
    return pl.pallas_call(
        functools.partial(_proj_ln_kernel, alpha=alpha, n_first=n1),
        grid=(m // tm,),
        in_specs=[row, pl.BlockSpec((tm, d), lambda i: (jnp.minimum(i, n1 - 1), 0)),
                  pl.BlockSpec((tm, d), lambda i: (jnp.maximum(i - n1, 0), 0)),
                  pl.BlockSpec((d, d), lambda i: (0, 0)), vec, vec,
                  pl.BlockSpec((d, n_r), lambda i: (0, 0)), pl.BlockSpec((1, n_r), lambda i: (0, 0))],
        out_specs=[pl.BlockSpec((tm * ROW_PITCH, LANES), lambda i: (i, 0)),
                   pl.BlockSpec((tm, n_r), lambda i: (i, 0))],
        out_shape=[jax.ShapeDtypeStruct((m * ROW_PITCH, LANES), F32), jax.ShapeDtypeStruct((m, n_r), F32)],
        compiler_params=_params("parallel"),
        name="proj_ln_route",
    )(merged, xa, xb, wo, ln_g, ln_b, router_w, router_b)


MOE_GROUP = 1280
MOE_ROW_TILE = 640
MOE_F_TILE = 256
MOE_N_TILE = 256
MOE_K_CHUNK = 256
COMBINE_TOKENS = 128


def _row_copy(src_hbm, src_row, dst_vmem, dst_row, sem):
    return pltpu.make_async_copy(
        src_hbm.at[pl.ds(pl.multiple_of(src_row * ROW_PITCH, 8), ROW_SLABS)],
        dst_vmem.at[pl.ds(pl.multiple_of(dst_row * ROW_PITCH, 8), ROW_SLABS)], sem)


def _start_row_gather(src_row, n_rows, src_hbm, buf, sem):
    def issue(r, carry):
        _row_copy(src_hbm, src_row(r), buf, r, sem).start()
        return carry

    lax.fori_loop(0, n_rows, issue, 0, unroll=8)


def _wait_row_gather(n_rows, src_hbm, buf, sem):
    n = n_rows * ROW_SLABS
    pltpu.make_async_copy(src_hbm.at[pl.ds(0, n)], buf.at[pl.ds(0, n)], sem).wait()


def _dispatch_kernel(tv_ref, tj_ref, tok_ref, x_hbm, o_ref, buf, sem):
    t = pl.program_id(0)
    n = pl.num_programs(0)
    slot = t % 2
    rt = MOE_ROW_TILE

    def start(tile, dst):
        j0 = tj_ref[tile]
        _start_row_gather(lambda r: tok_ref[j0 + r], rt, x_hbm, buf.at[dst], sem.at[dst])

    @pl.when((t == 0) & (tv_ref[0] > 0))
    def _first():
        start(0, 0)

    @pl.when((t + 1 < n) & (tv_ref[jnp.minimum(t + 1, n - 1)] > 0))
    def _next():
        start(jnp.minimum(t + 1, n - 1), 1 - slot)

    @pl.when(tv_ref[t] > 0)
    def _():
        _wait_row_gather(rt, x_hbm, buf.at[slot], sem.at[slot])
        o_ref[...] = _b16(_from_slabs(buf.at[slot], 0, rt))

    @pl.when(tv_ref[t] == 0)
    def _():
        o_ref[...] = jnp.zeros(o_ref.shape, o_ref.dtype)


def moe_dispatch(x_slabs, order, tile_valid, tile_first, d):
    n_tiles = tile_valid.shape[0]
    rt = MOE_ROW_TILE
    sorted_tok = jnp.concatenate([order // TOP_K, jnp.zeros((rt,), jnp.int32)])
    grid_spec = pltpu.PrefetchScalarGridSpec(
        num_scalar_prefetch=3,
        grid=(n_tiles,),
        in_specs=[pl.BlockSpec(memory_space=pl.ANY)],
        out_specs=pl.BlockSpec((rt, d), lambda t, tv, tj, od: (t, 0)),
        scratch_shapes=[pltpu.VMEM((2, rt * ROW_PITCH, LANES), F32), pltpu.SemaphoreType.DMA((2,))],
    )
    return pl.pallas_call(
        _dispatch_kernel,
        grid_spec=grid_spec,
        out_shape=jax.ShapeDtypeStruct((n_tiles * rt, d), BF16),
        compiler_params=_params("arbitrary"),
        name="moe_dispatch",
    )(tile_valid, tile_first, sorted_tok, x_slabs)


def _moe_kernel(ie_ref, ib_ref, nr_ref, j0_ref, x_ref, wg_ref, wu_ref, wd_ref, bg_ref, bu_ref, bd_ref,
                y_hbm, h_sc, wgu_sc, wd_sc, o_sc, pend_sc, sem, *, n_f, n_n, n_assign):
    i = pl.program_id(0)
    s = pl.program_id(1)
    rows = nr_ref[i]
    rt = MOE_ROW_TILE
    tile_slabs = rt * ROW_PITCH
    n_tiles = (rows + rt - 1) // rt
    tf = MOE_F_TILE
    last_step = n_f + n_n - 1

    def tile_copy(r, row0):
        return pltpu.make_async_copy(
            o_sc.at[pl.ds(pl.multiple_of(r * tile_slabs, tile_slabs), tile_slabs)],
            y_hbm.at[pl.ds(pl.multiple_of((row0 + r * rt) * ROW_PITCH, 8), tile_slabs)], sem)

    def wait_pending():
        def one(k, carry):
            tile_copy(0, 0).wait()
            return carry

        lax.fori_loop(0, pend_sc[0], one, 0)
        pend_sc[0] = 0

    @pl.when((i == 0) & (s == 0))
    def _reset():
        pend_sc[0] = 0
        o_sc[...] = jnp.zeros(o_sc.shape, F32)

    @pl.when((rows > 0) & (s < n_f))
    def _hidden():
        def project(r):
            return _dot(x_ref[pl.ds(pl.multiple_of(r * rt, rt), rt), :], wgu_sc[...])

        def activate(h, r):
            hg = jnp.minimum(h[:, 0:tf] + bg_ref[0], SWIGLU_LIMIT)
            hu = jnp.clip(h[:, tf:2 * tf] + bu_ref[0], -SWIGLU_LIMIT, SWIGLU_LIMIT)
            h_sc[s, pl.ds(pl.multiple_of(r * rt, rt), rt), :] = _b16(hg * jax.nn.sigmoid(SWIGLU_ALPHA * hg) * (hu + 1.0))

        kc = MOE_K_CHUNK
        h0 = None
        for c in range(wg_ref.shape[1] // kc):
            ks = slice(c * kc, (c + 1) * kc)
            w_c = jnp.concatenate([_b16(wg_ref[0, ks, :]), _b16(wu_ref[0, ks, :])], axis=1)
            wgu_sc[ks, :] = w_c
            part = _dot(x_ref[0:rt, ks], w_c)
            h0 = part if h0 is None else h0 + part
        activate(h0, 0)

        def tile(r, carry):
            activate(project(r), r)
            return carry

        lax.fori_loop(1, n_tiles, tile, 0)

    @pl.when((rows > 0) & (s >= n_f))
    def _down():
        n = s - n_f

        @pl.when(s == n_f)
        def _():
            wait_pending()

        def project(r):
            sl = pl.ds(pl.multiple_of(r * rt, rt), rt)
            acc = _dot(h_sc[0, sl, :], wd_sc[0:tf, :])
            for f in range(1, n_f):
                acc += _dot(h_sc[f, sl, :], wd_sc[f * tf:(f + 1) * tf, :])
            return acc

        def emit(acc, r):
            val = acc + bd_ref[0]
            for jj in range(MOE_N_TILE // LANES):
                start = r * tile_slabs + n * (MOE_N_TILE // LANES) + jj
                o_sc[pl.ds(start, rt, stride=ROW_PITCH), :] = val[:, jj * LANES:(jj + 1) * LANES]

        acc0 = None
        for f in range(n_f):
            fs = slice(f * tf, (f + 1) * tf)
            w_f = _b16(wd_ref[0, fs, :])
            wd_sc[fs, :] = w_f
            part = _dot(h_sc[f, 0:rt, :], w_f)
            acc0 = part if acc0 is None else acc0 + part
        emit(acc0, 0)

        def tile(r, carry):
            emit(project(r), r)
            return carry

        lax.fori_loop(1, n_tiles, tile, 0)

        @pl.when(s == last_step)
        def _():
            row0 = j0_ref[i]

            def send(r, carry):
                tile_copy(r, row0).start()
                return carry

            lax.fori_loop(0, n_tiles, send, 0)
            pend_sc[0] = n_tiles

    @pl.when((i == pl.num_programs(0) - 1) & (s == last_step))
    def _finish():
        wait_pending()
        o_sc[0:tile_slabs, :] = jnp.zeros((tile_slabs, LANES), F32)
        tail = tile_copy(0, n_assign)
        tail.start()
        tail.wait()


def moe_experts(x_sorted, item_expert, item_block, item_rows, item_first, n_assign,
                w_gate_up, b_gate_up, w_down, b_down):
    n_items = item_expert.shape[0]
    d = x_sorted.shape[1]
    n_exp, d_ff, d_out = w_down.shape
    n_f = d_ff // MOE_F_TILE
    n_n = d_out // MOE_N_TILE

    def f_idx(i, s, nr):
        return jnp.minimum(jnp.where(nr[i] > 0, s, n_f - 1), n_f - 1)

    def n_idx(i, s, nr):
        return jnp.where(nr[i] > 0, jnp.maximum(s - n_f, 0), n_n - 1)

    grid_spec = pltpu.PrefetchScalarGridSpec(
        num_scalar_prefetch=4,
        grid=(n_items, n_f + n_n),
        in_specs=[
            pl.BlockSpec((MOE_GROUP, d), lambda i, s, ie, ib, nr, j0: (ib[i], 0)),
            pl.BlockSpec((1, d, MOE_F_TILE), lambda i, s, ie, ib, nr, j0: (ie[i], 0, f_idx(i, s, nr))),
            pl.BlockSpec((1, d, MOE_F_TILE), lambda i, s, ie, ib, nr, j0: (ie[i], 0, n_f + f_idx(i, s, nr))),
            pl.BlockSpec((1, d_ff, MOE_N_TILE), lambda i, s, ie, ib, nr, j0: (ie[i], 0, n_idx(i, s, nr))),
            pl.BlockSpec((1, 1, MOE_F_TILE), lambda i, s, ie, ib, nr, j0: (ie[i], 0, f_idx(i, s, nr))),
            pl.BlockSpec((1, 1, MOE_F_TILE), lambda i, s, ie, ib, nr, j0: (ie[i], 0, n_f + f_idx(i, s, nr))),
            pl.BlockSpec((1, 1, MOE_N_TILE), lambda i, s, ie, ib, nr, j0: (ie[i], 0, n_idx(i, s, nr))),
        ],
        out_specs=pl.BlockSpec(memory_space=pl.ANY),
        scratch_shapes=[pltpu.VMEM((n_f, MOE_GROUP, MOE_F_TILE), BF16),
                        pltpu.VMEM((d, 2 * MOE_F_TILE), BF16),
                        pltpu.VMEM((d_ff, MOE_N_TILE), BF16),
                        pltpu.VMEM((MOE_GROUP * ROW_PITCH, LANES), F32),
                        pltpu.SMEM((1,), jnp.int32),
                        pltpu.SemaphoreType.DMA],
    )
    return pl.pallas_call(
        functools.partial(_moe_kernel, n_f=n_f, n_n=n_n, n_assign=n_assign),
        grid_spec=grid_spec,
        out_shape=jax.ShapeDtypeStruct(((n_assign + MOE_ROW_TILE) * ROW_PITCH, LANES), F32),
        compiler_params=_params("arbitrary", "arbitrary"),
        name="moe_experts",
    )(item_expert, item_block, item_rows, item_first, x_sorted, w_gate_up, w_gate_up, w_down,
      b_gate_up.reshape(n_exp, 1, 2 * d_ff), b_gate_up.reshape(n_exp, 1, 2 * d_ff), b_down.reshape(n_exp, 1, d_out))


def _combine_kernel(pos_ref, nxt_ref, y_hbm, x1_ref, gate_ref, g_ref, b_ref, oa_ref, ob_ref, buf, sem, *, alpha, n_a):
    tm = COMBINE_TOKENS
    t = pl.program_id(0)
    slot = t % 2

    @pl.when(t == 0)
    def _first():
        _start_row_gather(lambda r: pos_ref[0, 0, r], tm * TOP_K, y_hbm, buf.at[0], sem.at[0])

    @pl.when(t + 1 < pl.num_programs(0))
    def _next():
        _start_row_gather(lambda r: nxt_ref[0, 0, r], tm * TOP_K, y_hbm, buf.at[1 - slot], sem.at[1 - slot])

    cur = buf.at[slot]
    _wait_row_gather(tm * TOP_K, y_hbm, cur, sem.at[slot])

    def finish(o_ref):
        d_model = o_ref.shape[1]
        slabs = [slice(j * LANES, (j + 1) * LANES) for j in range(ROW_SLABS)]
        gate = gate_ref[...]
        gk = [jnp.broadcast_to(gate[:, k:k + 1], (tm, LANES)) for k in range(TOP_K)]
        total = jnp.zeros((tm, 1), F32)
        for j, sl in enumerate(slabs):
            y = alpha * x1_ref[pl.ds(j, tm, stride=ROW_PITCH), :]
            for k in range(TOP_K):
                y += gk[k] * cur[pl.ds(k * tm * ROW_PITCH + j, tm, stride=ROW_PITCH), :]
            o_ref[:, sl] = y
            total += jnp.sum(y, axis=1, keepdims=True)
        mean = total * (1.0 / d_model)
        sq = jnp.zeros((tm, 1), F32)
        for sl in slabs:
            d = o_ref[:, sl] - mean
            sq += jnp.sum(d * d, axis=1, keepdims=True)
        rstd = lax.rsqrt(sq * (1.0 / d_model) + LN_EPS)
        for sl in slabs:
            o_ref[:, sl] = (o_ref[:, sl] - mean) * rstd * g_ref[:, sl] + b_ref[:, sl]

    @pl.when(t < n_a)
    def _():
        finish(oa_ref)

    @pl.when(t >= n_a)
    def _():
        finish(ob_ref)


def moe_combine_norm(y_slabs, pos, gate, x1_slabs, ln_g, ln_b, *, alpha, n_first):
    n_tok = pos.shape[0]
    tm = COMBINE_TOKENS
    d = ln_g.shape[1]
    n_t = n_tok // tm
    n_a = n_first // tm
    vec = pl.BlockSpec((1, d), lambda t: (0, 0))
    idx = pos.reshape(n_t, tm, TOP_K).transpose(0, 2, 1).reshape(n_t, 1, tm * TOP_K)
    return pl.pallas_call(
        functools.partial(_combine_kernel, alpha=alpha, n_a=n_a),
        grid=(n_t,),
        in_specs=[pl.BlockSpec((1, 1, tm * TOP_K), lambda t: (t, 0, 0), memory_space=pltpu.SMEM),
                  pl.BlockSpec((1, 1, tm * TOP_K), lambda t: (jnp.minimum(t + 1, n_t - 1), 0, 0),
                               memory_space=pltpu.SMEM),
                  pl.BlockSpec(memory_space=pl.ANY),
                  pl.BlockSpec((tm * ROW_PITCH, LANES), lambda t: (t, 0)),
                  pl.BlockSpec((tm, TOP_K), lambda t: (t, 0)), vec, vec],
        out_specs=[pl.BlockSpec((tm, d), lambda t: (jnp.minimum(t, n_a - 1), 0)),
                   pl.BlockSpec((tm, d), lambda t: (jnp.maximum(t - n_a, 0), 0))],
        out_shape=[jax.ShapeDtypeStruct((n_first, d), F32), jax.ShapeDtypeStruct((n_tok - n_first, d), F32)],
        scratch_shapes=[pltpu.VMEM((2, tm * TOP_K * ROW_PITCH, LANES), F32), pltpu.SemaphoreType.DMA((2,))],
        compiler_params=_params("arbitrary"),
        name="moe_combine",
    )(idx, idx, y_slabs, x1_slabs, gate, ln_g, ln_b)


def _route(logits, n_items):
    n_tok = logits.shape[0]
    n_assign = n_tok * TOP_K
    tiles_per_item = MOE_GROUP // MOE_ROW_TILE
    top_logit, top_idx = lax.top_k(logits, TOP_K)
    gate = jax.nn.softmax(top_logit, axis=-1)
    flat_e = top_idx.reshape(-1).astype(jnp.int32)
    order = jnp.argsort(flat_e).astype(jnp.int32)
    rank = jnp.argsort(order).astype(jnp.int32)
    experts = jnp.arange(N_EXPERTS, dtype=jnp.int32)
    onehot = flat_e[:, None] == experts[None, :]
    counts = jnp.sum(onehot, axis=0, dtype=jnp.int32)
    padded = (counts + MOE_GROUP - 1) // MOE_GROUP * MOE_GROUP
    start = jnp.cumsum(counts) - counts
    padded_end = jnp.cumsum(padded)
    padded_start = padded_end - padded

    def expert_of(row0):
        return jnp.minimum(jnp.sum(row0[:, None] >= padded_end[None, :], axis=1, dtype=jnp.int32), N_EXPERTS - 1)

    def table(values, e):
        return jnp.sum(jnp.where(e[:, None] == experts[None, :], values[None, :], 0), axis=1)

    n_real = (padded_end[-1] // MOE_GROUP).astype(jnp.int32)
    item = jnp.minimum(jnp.arange(n_items, dtype=jnp.int32), jnp.maximum(n_real - 1, 0))
    item_expert = expert_of(item * MOE_GROUP)
    filled = jnp.clip(table(padded_start + counts, item_expert) - item * MOE_GROUP, 0, MOE_GROUP)
    item_rows = jnp.where(jnp.arange(n_items) < n_real, filled, 0).astype(jnp.int32)
    item_first = jnp.clip(table(start - padded_start, item_expert) + item * MOE_GROUP, 0, n_assign - 1)
    tile_row0 = (jnp.arange(tiles_per_item, dtype=jnp.int32) * MOE_ROW_TILE)[None, :]
    tile_valid = jnp.clip(item_rows[:, None] - tile_row0, 0, MOE_ROW_TILE).reshape(-1).astype(jnp.int32)
    tile_first = jnp.clip(item_first[:, None] + tile_row0, 0, n_assign - 1).reshape(-1).astype(jnp.int32)
    return (gate, order, rank.reshape(n_tok, TOP_K), item_expert, item.astype(jnp.int32), item_rows,
            item_first.astype(jnp.int32), tile_valid, tile_first)


def _pad_rw_cols(a, rw_cols, pad):
    lead = a.shape[:-1]
    return jnp.concatenate([a[..., :rw_cols], jnp.zeros(lead + (pad,), a.dtype), a[..., rw_cols:]], axis=-1)


def _cache_rows(c, heads, head_dim):
    b, n_mem = c.shape[:2]
    tiles = head_dim // LANES
    c = c.reshape(b, n_mem, heads, tiles, LANES).transpose(0, 1, 3, 2, 4)
    return c.reshape(b, n_mem * tiles * heads, LANES)


def _layer(xp, xs, mem_prompt, cache_k, cache_v, st_rwkv, st_shift, st_conv, wts, depth):
    (w_in, w_w_up, w0, w_a_up, a0, w_g_up, tshift_mu, k_k, k_a, r_k, gn_g, gn_b, conv_w,
     w_mem_k, w_mem_v, w_proj_a, w_proj_b, w_proj_m, w_o, ln1_g, ln1_b,
     router_w, router_b, w_gate_up, b_gate_up, w_down, b_down, ln2_g, ln2_b) = wts
    bp, sp, d = xp.shape
    bs, ss, _ = xs.shape
    n_mem = mem_prompt.shape[1]
    mem_heads, mem_head = cache_k.shape[-2:]
    n_p, n_s = bp * sp, bs * ss
    n_tok = n_p + n_s
    alpha = (2.0 * depth) ** 0.25
    rw_cols = 3 * d + DECAY_LORA + AICL_LORA + GATE_LORA
    pad = LORA_PAD - (rw_cols - 3 * d)
    rw_pad = rw_cols + pad

    xp2, xs2 = xp.reshape(n_p, d), xs.reshape(n_s, d)
    x_bf = _b16(jnp.concatenate([xp2, xs2], axis=0))
    mu_pad = _pad_rw_cols(tshift_mu, rw_cols, pad)[None, :rw_pad]
    rest_cols = w_in.shape[1] - rw_cols

    p_rw = matmul_f32_weights(x_bf, w_in, rw_pad, BF16, 1024, 512, "in_proj_rw")
    p2 = matmul_shifted_weights(x_bf, w_in, rw_cols, rest_cols, BF16, 1024, 1024, "in_proj_rest")
    prev_s = matmul_f32_weights(st_shift, w_in, rw_pad, F32, bs, 512, "prev_proj").reshape(bs, 1, rw_pad)
    prev_p = jnp.zeros((bp, 1, rw_pad), F32)

    def lora_rows(w, row0):
        return _b16(jnp.zeros((LORA_PAD, d), F32).at[row0:row0 + w.shape[0]].set(w))

    ww = lora_rows(w_w_up, 0)
    wa = lora_rows(w_a_up, DECAY_LORA)
    wg = lora_rows(w_g_up, DECAY_LORA + AICL_LORA)
    r2 = lambda v: v.reshape(1, d)
    rw_vecs = (mu_pad, r2(w0), r2(a0), r2(k_k), r2(k_a), r2(r_k), r2(gn_g), r2(gn_b), ww, wa, wg)
    heads = d // RW_HEAD
    ya_p, rw_p = rwkv_time_mix(p_rw, prev_p, jnp.zeros((bp, heads, RW_HEAD, RW_HEAD), F32), *rw_vecs,
                               batch=bp, seq=sp, row_block0=0, pairs=8, has_state=False)
    ya_s, rw_s = rwkv_time_mix(p_rw, prev_s, st_rwkv, *rw_vecs,
                               batch=bs, seq=ss, row_block0=n_p // RW_ROWS, pairs=4, has_state=True)

    yb_p, cv_p = conv_sequences(p2, conv_w, batch=bp, seq=sp, width=d, rows=256, tn=512)
    yb_s, cv_s0, cv_s1 = conv_short(p2, st_conv[:, 0, :], st_conv[:, 1, :], conv_w, row_block0=n_p // 128,
                                    batch=bs, seq=ss, width=d, rows=128, tn=512)
    cv_s = jnp.stack([cv_s0, cv_s1], axis=1)

    mem_in = _b16(mem_prompt.reshape(bp * n_mem, d))
    mk = matmul(mem_in, _b16(w_mem_k), F32, bp * n_mem, 512, "mem_k")
    mv = matmul(mem_in, _b16(w_mem_v), F32, bp * n_mem, 512, "mem_v")
    q_col0 = 3 * d // mem_head
    ym_p = attention_sequences(p2, q_col0, mk, mv, batch=bp, seq=sp, n_mem=n_mem, heads=mem_heads,
                               head_dim=mem_head, tq=512)
    bb = 4
    ym_s = attention_cache(p2, 3 * d // d, n_p // (bb * ss), _cache_rows(cache_k, mem_heads, mem_head),
                           _cache_rows(cache_v, mem_heads, mem_head),
                           batch=bs, seq=ss, heads=mem_heads, head_dim=mem_head, bb=bb)

    merged = merge_branches((ya_p, yb_p, ym_p), (ya_s, yb_s, ym_s), p2, 4 * d // 512, _b16(w_proj_a),
                            _b16(w_proj_b), _b16(w_proj_m), tm=512, tn=512)
    rw_pad_r = jnp.zeros((d, LANES), F32).at[:, :N_EXPERTS].set(router_w)
    rb_pad_r = jnp.zeros((1, LANES), F32).at[0, :N_EXPERTS].set(router_b)
    x1_slabs, logits = project_norm_route(merged, xp2, xs2, _b16(w_o), r2(ln1_g), r2(ln1_b),
                                          rw_pad_r, rb_pad_r, alpha=alpha, tm=256)

    n_items = n_tok * TOP_K // MOE_GROUP + N_EXPERTS
    (gate, order, rank, item_expert, item_block, item_rows, item_first, tile_valid,
     tile_first) = _route(logits[:, :N_EXPERTS], n_items)
    x_sorted = moe_dispatch(x1_slabs, order, tile_valid, tile_first, d)
    y_slabs = moe_experts(x_sorted, item_expert, item_block, item_rows, item_first, n_tok * TOP_K,
                          w_gate_up, b_gate_up, w_down, b_down)
    y_p, y_s = moe_combine_norm(y_slabs, rank, gate, x1_slabs, r2(ln2_g), r2(ln2_b), alpha=alpha, n_first=n_p)
    y_p = y_p.reshape(bp, sp, d)
    y_s = y_s.reshape(bs, ss, d)
    mk5 = mk.reshape(bp, n_mem, mem_heads, mem_head)
    mv5 = mv.reshape(bp, n_mem, mem_heads, mem_head)
    return y_p, y_s, mk5, mv5, rw_p, xp[:, -1], cv_p, rw_s, xs[:, -1], cv_s


def kernel(x_prompt, x_sample, mem_prompt, cache_mem_k, cache_mem_v, state_rwkv, state_shift, state_conv, w_in, w_w_up, w0, w_a_up, a0, w_g_up, tshift_mu, k_k, k_a, r_k, gn_g, gn_b, conv_w, w_mem_k, w_mem_v, w_proj_a, w_proj_b, w_proj_m, w_o, ln1_g, ln1_b, router_w, router_b, w_gate_up, b_gate_up, w_down, b_down, ln2_g, ln2_b):
    weights = (w_in, w_w_up, w0, w_a_up, a0, w_g_up, tshift_mu, k_k, k_a, r_k, gn_g, gn_b, conv_w,
               w_mem_k, w_mem_v, w_proj_a, w_proj_b, w_proj_m, w_o, ln1_g, ln1_b,
               router_w, router_b, w_gate_up, b_gate_up, w_down, b_down, ln2_g, ln2_b)
    depth = w_in.shape[0]
    yp, ys = x_prompt, x_sample
    outs = [[] for _ in range(8)]
    for l in range(depth):
        res = _layer(yp, ys, mem_prompt, cache_mem_k[l], cache_mem_v[l], state_rwkv[l], state_shift[l],
                     state_conv[l], tuple(w[l] for w in weights), depth)
        yp, ys = res[0], res[1]
        for acc, r in zip(outs, res[2:]):
            acc.append(r)
    return (yp, ys) + tuple(jnp.stack(o) for o in outs)
```

```python
import functools
import math

import jax
import jax.numpy as jnp
from jax import lax
from jax.experimental import pallas as pl
from jax.experimental.pallas import tpu as pltpu

F32 = jnp.float32
BF16 = jnp.bfloat16

LANES = 128
ROW_SLABS = 16
ROW_PITCH = 24
RW_HEAD = 64
HEADS_PER_BLOCK = LANES // RW_HEAD
RW_ROWS = 64
DECAY_LORA = 96
AICL_LORA = 96
GATE_LORA = 256
LORA_PAD = 512
DECAY_SCALE = math.exp(-0.5)
GN_EPS = 64e-5
LN_EPS = 1e-5
N_EXPERTS = 32
TOP_K = 4
SWIGLU_LIMIT = 7.0
SWIGLU_ALPHA = 1.702
VMEM_LIMIT = 56 * 1024 * 1024


def _dot(a, b):
    return jnp.dot(a, b, preferred_element_type=F32)


def _dot_nt(a, b):
    return lax.dot_general(a, b, (((1,), (1,)), ((), ())), preferred_element_type=F32)


def _dot_tn(a, b):
    return lax.dot_general(a, b, (((0,), (0,)), ((), ())), preferred_element_type=F32)


def _b16(x):
    return x.astype(BF16)


def _params(*sem):
    return pltpu.CompilerParams(dimension_semantics=sem, vmem_limit_bytes=VMEM_LIMIT)


def _split3(x):
    hi = _b16(x)
    r1 = x - hi.astype(F32)
    mid = _b16(r1)
    lo = _b16(r1 - mid.astype(F32))
    return hi, mid, lo


def _rwkv_kernel(pr_ref, pk_ref, pv_ref, pl_ref, qr_ref, qk_ref, qv_ref, ql_ref,
                 mur_ref, muk_ref, muv_ref, mul_ref,
                 w0_ref, a0_ref, kk_ref, ka_ref, rk_ref, gng_ref, gnb_ref,
                 ww_ref, wa_ref, wg_ref, s0_ref,
                 y_ref, sout_ref, *scratch, n_seq, n_chunks, pairs, has_state):
    C = RW_ROWS
    L = C // n_seq
    R = HEADS_PER_BLOCK * C
    W = pairs * LANES
    c = pl.program_id(2)
    carried = n_chunks > 1
    if carried:
        s_sc, cr_sc, ck_sc, cv_sc, cl_sc = scratch

        @pl.when(c == 0)
        def _init_state():
            for p in range(pairs):
                if has_state:
                    s_sc[p] = s0_ref[0, 2 * p:2 * p + 2].reshape(LANES, RW_HEAD)
                else:
                    s_sc[p] = jnp.zeros((LANES, RW_HEAD), F32)

    def prev_rows(q_ref, carry_sc, width):
        if carried:
            @pl.when(c == 0)
            def _():
                carry_sc[0:1, :] = q_ref[0]
            return jnp.broadcast_to(carry_sc[0:1, :], (C, width))
        q = q_ref[...]
        return jnp.broadcast_to(q, (n_seq, L, width)).reshape(C, width)

    def shifted_lerp(p_ref, q_ref, carry_sc, mu_ref, width):
        p = p_ref[...].astype(F32)
        pos = lax.broadcasted_iota(jnp.int32, (C, width), 0) % L
        prev = jnp.where(pos == 0, prev_rows(q_ref, carry_sc, width), pltpu.roll(p, 1, axis=0))
        if carried:
            carry_sc[0:1, :] = p[C - 1:C, :]
        return p + (prev - p) * mu_ref[...]

    zr = shifted_lerp(pr_ref, qr_ref, cr_sc if carried else None, mur_ref, W)
    zk = shifted_lerp(pk_ref, qk_ref, ck_sc if carried else None, muk_ref, W)
    zv = shifted_lerp(pv_ref, qv_ref, cv_sc if carried else None, muv_ref, W)
    zl = shifted_lerp(pl_ref, ql_ref, cl_sc if carried else None, mul_ref, LORA_PAD)

    lw = -DECAY_SCALE * jax.nn.sigmoid(w0_ref[...] + _dot(_b16(jnp.tanh(zl)), ww_ref[...]))
    a_all = jax.nn.sigmoid(a0_ref[...] + _dot(_b16(zl), wa_ref[...]))
    g_all = _dot(_b16(jax.nn.sigmoid(zl)), wg_ref[...])

    ti = lax.broadcasted_iota(jnp.int32, (2 * C, C), 0)
    tj = lax.broadcasted_iota(jnp.int32, (2 * C, C), 1)
    same_seq = (ti % C) // L == tj // L
    cum_lhs = _b16((same_seq & ((ti >= C) | (tj <= ti))).astype(F32))
    hi, mid, lo = _split3(lw)
    cum = _dot(cum_lhs, jnp.concatenate([hi, mid, lo], axis=1))
    cum = cum[:, 0:W] + cum[:, W:2 * W] + cum[:, 2 * W:3 * W]
    cw_all = cum[0:C]
    tot_all = cum[C:2 * C]

    lb_r = lax.broadcasted_iota(jnp.int32, (2 * LANES, LANES), 0) % LANES // RW_HEAD
    lb_c = lax.broadcasted_iota(jnp.int32, (2 * LANES, LANES), 1) // RW_HEAD
    head_ones2 = _b16((lb_r == lb_c).astype(F32))

    def head_sum(x):
        xh = _b16(x)
        xl = _b16(x - xh.astype(F32))
        return _dot(jnp.concatenate([xh, xl], axis=1), head_ones2)

    head0 = lax.broadcasted_iota(jnp.int32, (n_seq, L, LANES), 2) < RW_HEAD

    def stack(x):
        x3 = x.reshape(n_seq, L, LANES)
        return jnp.concatenate([jnp.where(head0, x3, 0.0), jnp.where(head0, 0.0, x3)], axis=1).reshape(R, LANES)

    def unstack(x):
        x3 = x.reshape(n_seq, 2 * L, LANES)
        return (x3[:, 0:L, :] + x3[:, L:2 * L, :]).reshape(C, LANES)

    si = lax.broadcasted_iota(jnp.int32, (R, R), 0)
    sj = lax.broadcasted_iota(jnp.int32, (R, R), 1)
    same = (si // L) == (sj // L)
    strict = same & (si > sj)
    incl = same & (si >= sj)
    eye = (si == sj).astype(F32)
    row_head = (lax.broadcasted_iota(jnp.int32, (R, LANES), 0) // L) % HEADS_PER_BLOCK
    lane_head = lax.broadcasted_iota(jnp.int32, (R, LANES), 1) // RW_HEAD
    head_match = row_head == lane_head
    state_row_head0 = lax.broadcasted_iota(jnp.int32, (LANES, LANES), 0) < RW_HEAD
    n_lvl = int(math.log2(L)) - 1
    S2 = 2 * L

    P = range(pairs)
    lsl = [slice(p * LANES, (p + 1) * LANES) for p in P]

    def rows_cat(xs):
        return jnp.concatenate(xs, axis=0) if len(xs) > 1 else xs[0]

    kk_raw = [zk[:, ls] * kk_ref[:, ls] for ls in lsl]
    kk_n2 = head_sum(rows_cat([x * x for x in kk_raw]))
    kk_l = [kk_raw[p] / jnp.maximum(jnp.sqrt(kk_n2[p * C:(p + 1) * C]), 1e-12) for p in P]
    kmod_l = [zk[:, ls] * (1.0 + (a_all[:, ls] - 1.0) * ka_ref[:, ls]) for ls in lsl]
    beta_l = [kk_l[p] * a_all[:, lsl[p]] for p in P]
    e_neg_l = [jnp.exp(-cw_all[:, ls]) for ls in lsl]
    e_tail_l = [jnp.exp(tot_all[:, ls] - cw_all[:, ls]) for ls in lsl]
    kk_s_l = [stack(kk_l[p] * jnp.exp(cw_all[:, lsl[p]] - lw[:, lsl[p]])) for p in P]
    r_s_l = [stack(zr[:, ls] * jnp.exp(cw_all[:, ls])) for ls in lsl]
    v_b_l = [_b16(stack(zv[:, ls])) for ls in lsl]
    bw_b_l = [_b16(stack(beta_l[p] * e_tail_l[p])) for p in P]
    kw_b_l = [_b16(stack(kmod_l[p] * e_tail_l[p])) for p in P]
    a_lhs = [_b16(jnp.concatenate([kk_s_l[p], r_s_l[p]], axis=0)) for p in P]
    a_rhs = [_b16(jnp.concatenate([stack(beta_l[p] * e_neg_l[p]), stack(kmod_l[p] * e_neg_l[p])], axis=0)) for p in P]

    amat_l = [_dot_nt(a_lhs[p], a_rhs[p]) for p in P]
    a_ak_b = [_b16(jnp.where(strict, m[0:R, R:2 * R], 0.0)) for m in amat_l]
    a_r_b = [_b16(jnp.concatenate([jnp.where(incl, m[R:2 * R, 0:R], 0.0),
                                    jnp.where(incl, m[R:2 * R, R:2 * R], 0.0)], axis=1)) for m in amat_l]

    x_l = [jnp.where(strict, -m[0:R, 0:R], 0.0) for m in amat_l]
    minv_l = [eye + x for x in x_l]
    xb_l = [_b16(x) for x in x_l]
    cur_l = [_dot(xb, xb) for xb in xb_l]
    akv_l = [_dot(a_ak_b[p], v_b_l[p]) for p in P]
    for lvl in range(n_lvl):
        cb_l = [_b16(cur) for cur in cur_l]
        if lvl < n_lvl - 1:
            both_l = [_dot(_b16(jnp.concatenate([minv_l[p], cur_l[p]], axis=0)), cb_l[p]) for p in P]
            minv_l = [minv_l[p] + both_l[p][0:R] for p in P]
            cur_l = [both[R:2 * R] for both in both_l]
        else:
            minv_l = [minv_l[p] + _dot(_b16(minv_l[p]), cb_l[p]) for p in P]

    pq_l = [_dot(_b16(minv_l[p]), _b16(jnp.concatenate([kk_s_l[p], akv_l[p]], axis=1))) for p in P]

    def fold_lanes(x):
        return _b16(x + pltpu.roll(x, RW_HEAD, axis=1))[:, 0:RW_HEAD]

    p_fold = [fold_lanes(-pq[:, 0:LANES]) for pq in pq_l]
    r_fold = [fold_lanes(r_s) for r_s in r_s_l]
    seqs = [(p, b) for p in P for b in range(n_seq)]

    def state_in(p, b):
        if carried:
            return s_sc[p]
        return s0_ref[b, 2 * p:2 * p + 2].reshape(LANES, RW_HEAD)

    states = {pb: state_in(*pb) for pb in seqs}
    sdot = {(p, b): _dot_nt(jnp.concatenate([p_fold[p][b * S2:(b + 1) * S2], r_fold[p][b * S2:(b + 1) * S2]], axis=0),
                            _b16(states[(p, b)])) for (p, b) in seqs}
    u_b_l, rs_l = [], []
    for p in P:
        us = [jnp.where(head_match[b * S2:(b + 1) * S2], sdot[(p, b)][0:S2], 0.0) for b in range(n_seq)]
        rs = [jnp.where(head_match[b * S2:(b + 1) * S2], sdot[(p, b)][S2:2 * S2], 0.0) for b in range(n_seq)]
        u_b_l.append(_b16(rows_cat(us) - pq_l[p][:, LANES:2 * LANES]))
        rs_l.append(rows_cat(rs))

    y_s_l = [rs_l[p] + _dot(a_r_b[p], jnp.concatenate([u_b_l[p], v_b_l[p]], axis=0)) for p in P]
    z_l = {(p, b): _dot_tn(jnp.concatenate([u_b_l[p][b * S2:(b + 1) * S2], v_b_l[p][b * S2:(b + 1) * S2]], axis=0),
                           jnp.concatenate([bw_b_l[p][b * S2:(b + 1) * S2], kw_b_l[p][b * S2:(b + 1) * S2]], axis=0))
           for (p, b) in seqs}
    for (p, b) in seqs:
        z = z_l[(p, b)]
        z = (z + pltpu.roll(z, RW_HEAD, axis=1))[:, 0:RW_HEAD]
        wt = jnp.broadcast_to(jnp.exp(tot_all[b * L:b * L + 1, lsl[p]]), (LANES, LANES))
        wnat = jnp.where(state_row_head0, wt, pltpu.roll(wt, RW_HEAD, axis=1))[:, 0:RW_HEAD]
        s_new = states[(p, b)] * wnat + z
        if carried:
            s_sc[p] = s_new
        sout_ref[b, 2 * p:2 * p + 2] = s_new.reshape(HEADS_PER_BLOCK, RW_HEAD, RW_HEAD)

    y_l = [unstack(y_s) for y_s in y_s_l]
    stats = head_sum(rows_cat(y_l + [zr[:, ls] * kmod_l[p] * rk_ref[:, ls] for p, ls in enumerate(lsl)]))
    d_l = [y_l[p] - stats[p * C:(p + 1) * C] * (1.0 / RW_HEAD) for p in P]
    var = head_sum(rows_cat([d * d for d in d_l])) * (1.0 / RW_HEAD)
    for p, ls in enumerate(lsl):
        yn = d_l[p] * lax.rsqrt(var[p * C:(p + 1) * C] + GN_EPS) * gng_ref[:, ls] + gnb_ref[:, ls]
        bonus = stats[(pairs + p) * C:(pairs + p + 1) * C] * zv[:, ls]
        y_ref[:, ls] = ((yn + bonus) * g_all[:, ls]).astype(y_ref.dtype)


def rwkv_time_mix(p_rw, prev_rw, s0, mu, w0, a0, k_k, k_a, r_k, gn_g, gn_b, ww, wa, wg,
                  *, batch, seq, row_block0, pairs, has_state):
    width = w0.shape[-1]
    wp = pairs * LANES
    n_blk = width // wp
    heads = width // RW_HEAD
    hp = HEADS_PER_BLOCK * pairs
    lora_blk = 3 * width // LORA_PAD
    if seq >= RW_ROWS:
        n_seq, n_chunks, n_outer = 1, seq // RW_ROWS, batch
    else:
        n_seq, n_chunks, n_outer = RW_ROWS // seq, 1, batch * seq // RW_ROWS

    def tok(col0):
        return pl.BlockSpec((RW_ROWS, wp), lambda b, h, c: (row_block0 + b * n_chunks + c, col0 + h))

    def prev(col0):
        return pl.BlockSpec((n_seq, 1, wp), lambda b, h, c: (b, 0, col0 + h))

    def vec(col0):
        return pl.BlockSpec((1, wp), lambda b, h, c: (0, col0 + h))

    lora_w = pl.BlockSpec((LORA_PAD, wp), lambda b, h, c: (0, h))
    state = pl.BlockSpec((n_seq, hp, RW_HEAD, RW_HEAD), lambda b, h, c: (b, h, 0, 0))
    in_specs = [
        tok(0), tok(n_blk), tok(2 * n_blk),
        pl.BlockSpec((RW_ROWS, LORA_PAD), lambda b, h, c: (row_block0 + b * n_chunks + c, lora_blk)),
        prev(0), prev(n_blk), prev(2 * n_blk),
        pl.BlockSpec((n_seq, 1, LORA_PAD), lambda b, h, c: (b, 0, lora_blk)),
        vec(0), vec(n_blk), vec(2 * n_blk),
        pl.BlockSpec((1, LORA_PAD), lambda b, h, c: (0, lora_blk)),
        vec(0), vec(0), vec(0), vec(0), vec(0), vec(0), vec(0),
        lora_w, lora_w, lora_w, state,
    ]
    out_specs = [pl.BlockSpec((RW_ROWS, wp), lambda b, h, c: (b * n_chunks + c, h)), state]
    scratch = []
    if n_chunks > 1:
        scratch = [pltpu.VMEM((pairs, LANES, RW_HEAD), F32), pltpu.VMEM((8, wp), F32), pltpu.VMEM((8, wp), F32),
                   pltpu.VMEM((8, wp), F32), pltpu.VMEM((8, LORA_PAD), F32)]
    kern = functools.partial(_rwkv_kernel, n_seq=n_seq, n_chunks=n_chunks, pairs=pairs, has_state=has_state)
    return pl.pallas_call(
        kern,
        grid=(n_outer, n_blk, n_chunks),
        in_specs=in_specs,
        out_specs=out_specs,
        out_shape=[jax.ShapeDtypeStruct((batch * seq, width), BF16),
                   jax.ShapeDtypeStruct((batch, heads, RW_HEAD, RW_HEAD), F32)],
        scratch_shapes=scratch,
        compiler_params=_params("parallel", "parallel", "arbitrary"),
        name=f"rwkv_l{min(seq, RW_ROWS)}",
    )(p_rw, p_rw, p_rw, p_rw, prev_rw, prev_rw, prev_rw, prev_rw, mu, mu, mu, mu,
      w0, a0, k_k, k_a, r_k, gn_g, gn_b, ww, wa, wg, s0)


def _mm_kernel(x_ref, w_ref, o_ref):
    o_ref[...] = _dot(_b16(x_ref[...]), w_ref[...]).astype(o_ref.dtype)


def matmul(x, w, out_dtype, tm, tn, name):
    m, kd = x.shape
    n = w.shape[1]
    return pl.pallas_call(
        _mm_kernel,
        grid=(n // tn, m // tm),
        in_specs=[pl.BlockSpec((tm, kd), lambda j, i: (i, 0)),
                  pl.BlockSpec((kd, tn), lambda j, i: (0, j))],
        out_specs=pl.BlockSpec((tm, tn), lambda j, i: (i, j)),
        out_shape=jax.ShapeDtypeStruct((m, n), out_dtype),
        compiler_params=_params("parallel", "parallel"),
        name=name,
    )(x, w)


def _mm_cast_kernel(x_ref, w_ref, o_ref, w_sc):
    @pl.when(pl.program_id(1) == 0)
    def _():
        w_sc[...] = _b16(w_ref[...])

    o_ref[...] = _dot_nt(_b16(x_ref[...]), w_sc[...]).astype(o_ref.dtype)


def _mm_shift_kernel(x_ref, *refs, shift):
    w_refs, o_ref, w_sc = refs[:-2], refs[-2], refs[-1]

    @pl.when(pl.program_id(1) == 0)
    def _():
        tn = w_sc.shape[0]
        w_sc[0:LANES - shift, :] = _b16(w_refs[0][shift:LANES, :])
        for k in range(1, len(w_refs) - 1):
            w_sc[k * LANES - shift:(k + 1) * LANES - shift, :] = _b16(w_refs[k][...])
        w_sc[tn - shift:tn, :] = _b16(w_refs[-1][0:shift, :])

    o_ref[...] = _dot_nt(x_ref[...], w_sc[...]).astype(o_ref.dtype)


def matmul_shifted_weights(x, wt, col0, n_cols, out_dtype, tm, tn, name):
    m, kd = x.shape
    blk0, shift = divmod(col0, LANES)
    per_tile = tn // LANES
    windows = [pl.BlockSpec((LANES, kd), functools.partial(lambda j, i, k: (blk0 + j * per_tile + k, 0), k=k))
               for k in range(per_tile + 1)]
    return pl.pallas_call(
        functools.partial(_mm_shift_kernel, shift=shift),
        grid=(n_cols // tn, m // tm),
        in_specs=[pl.BlockSpec((tm, kd), lambda j, i: (i, 0))] + windows,
        out_specs=pl.BlockSpec((tm, tn), lambda j, i: (i, j)),
        out_shape=jax.ShapeDtypeStruct((m, n_cols), out_dtype),
        scratch_shapes=[pltpu.VMEM((tn, kd), BF16)],
        compiler_params=_params("parallel", "arbitrary"),
        name=name,
    )(x, *([wt] * (per_tile + 1)))


def matmul_f32_weights(x, wt, n_cols, out_dtype, tm, tn, name):
    m, kd = x.shape
    return pl.pallas_call(
        _mm_cast_kernel,
        grid=(n_cols // tn, m // tm),
        in_specs=[pl.BlockSpec((tm, kd), lambda j, i: (i, 0)),
                  pl.BlockSpec((tn, kd), lambda j, i: (j, 0))],
        out_specs=pl.BlockSpec((tm, tn), lambda j, i: (i, j)),
        out_shape=jax.ShapeDtypeStruct((m, n_cols), out_dtype),
        scratch_shapes=[pltpu.VMEM((tn, kd), BF16)],
        compiler_params=_params("parallel", "arbitrary"),
        name=name,
    )(x, wt)


def _conv_taps(u, u1, u2, bg_ref, cw_ref, y_ref):
    y = cw_ref[0:1, :] * u2 + cw_ref[1:2, :] * u1 + cw_ref[2:3, :] * u
    y_ref[...] = (bg_ref[...].astype(F32) * y).astype(y_ref.dtype)


def _conv_seq_kernel(bg_ref, cg_ref, xi_ref, cw_ref, y_ref, new_ref, carry_sc, *, rows, n_t):
    i = pl.program_id(2)

    @pl.when(i == 0)
    def _init():
        carry_sc[...] = jnp.zeros(carry_sc.shape, F32)

    u = cg_ref[...].astype(F32) * xi_ref[...].astype(F32)
    r = lax.broadcasted_iota(jnp.int32, u.shape, 0)
    c0 = carry_sc[0:1, :]
    c1 = carry_sc[1:2, :]
    u1 = jnp.where(r == 0, c1, pltpu.roll(u, 1, axis=0))
    u2 = jnp.where(r == 0, c0, jnp.where(r == 1, c1, pltpu.roll(u, 2, axis=0)))
    _conv_taps(u, u1, u2, bg_ref, cw_ref, y_ref)
    carry_sc[0:2, :] = u[rows - 2:rows, :]

    @pl.when(i == n_t - 1)
    def _fin():
        new_ref[0] = u[rows - 2:rows, :]


def conv_sequences(p2, conv_w, *, batch, seq, width, rows, tn):
    nb = width // tn
    n_t = seq // rows

    def tok(sec):
        return pl.BlockSpec((rows, tn), lambda b, j, i: (b * n_t + i, sec * nb + j))

    return pl.pallas_call(
        functools.partial(_conv_seq_kernel, rows=rows, n_t=n_t),
        grid=(batch, nb, n_t),
        in_specs=[tok(0), tok(1), tok(2), pl.BlockSpec((3, tn), lambda b, j, i: (0, j))],
        out_specs=[pl.BlockSpec((rows, tn), lambda b, j, i: (b * n_t + i, j)),
                   pl.BlockSpec((1, 2, tn), lambda b, j, i: (b, 0, j))],
        out_shape=[jax.ShapeDtypeStruct((batch * seq, width), BF16),
                   jax.ShapeDtypeStruct((batch, 2, width), F32)],
        scratch_shapes=[pltpu.VMEM((8, tn), F32)],
        compiler_params=_params("parallel", "parallel", "arbitrary"),
        name="conv_seq",
    )(p2, p2, p2, conv_w)


def _conv_short_kernel(bg_ref, cg_ref, xi_ref, p0_ref, p1_ref, cw_ref, y_ref, n0_ref, n1_ref, *, seq):
    u = cg_ref[...].astype(F32) * xi_ref[...].astype(F32)
    rows, tn = u.shape
    n_seq = rows // seq

    def per_seq(p_ref):
        return jnp.broadcast_to(p_ref[...].reshape(n_seq, 1, tn), (n_seq, seq, tn)).reshape(rows, tn)

    h0, h1 = per_seq(p0_ref), per_seq(p1_ref)
    pos = lax.broadcasted_iota(jnp.int32, u.shape, 0) % seq
    u1 = jnp.where(pos == 0, h1, pltpu.roll(u, 1, axis=0))
    u2 = jnp.where(pos == 0, h0, jnp.where(pos == 1, h1, pltpu.roll(u, 2, axis=0)))
    _conv_taps(u, u1, u2, bg_ref, cw_ref, y_ref)
    u3 = u.reshape(n_seq, seq, tn)
    n0_ref[...] = u3[:, seq - 2, :]
    n1_ref[...] = u3[:, seq - 1, :]


def conv_short(p2, hist0, hist1, conv_w, *, row_block0, batch, seq, width, rows, tn):
    nb = width // tn
    n_seq = rows // seq
    n_r = batch * seq // rows

    def tok(sec):
        return pl.BlockSpec((rows, tn), lambda i, j: (row_block0 + i, sec * nb + j))

    per_seq = pl.BlockSpec((n_seq, tn), lambda i, j: (i, j))
    return pl.pallas_call(
        functools.partial(_conv_short_kernel, seq=seq),
        grid=(n_r, nb),
        in_specs=[tok(0), tok(1), tok(2), per_seq, per_seq, pl.BlockSpec((3, tn), lambda i, j: (0, j))],
        out_specs=[pl.BlockSpec((rows, tn), lambda i, j: (i, j)), per_seq, per_seq],
        out_shape=[jax.ShapeDtypeStruct((batch * seq, width), BF16),
                   jax.ShapeDtypeStruct((batch, width), F32), jax.ShapeDtypeStruct((batch, width), F32)],
        compiler_params=_params("parallel", "parallel"),
        name="conv_short",
    )(p2, p2, p2, hist0, hist1, conv_w)


def _softmax_rows(s):
    m = jnp.max(s, axis=-1, keepdims=True)
    e = jnp.exp(s - m)
    return e / jnp.sum(e, axis=-1, keepdims=True)


def _attn_seq_kernel(q_ref, k_ref, v_ref, o_ref, *, scale):
    s = _dot_nt(q_ref[...], _b16(k_ref[...])) * scale
    p = _softmax_rows(s)
    o_ref[...] = _dot(_b16(p), _b16(v_ref[...])).astype(o_ref.dtype)


def attention_sequences(p2, q_col0, mem_k, mem_v, *, batch, seq, n_mem, heads, head_dim, tq):
    n_t = seq // tq
    kv = pl.BlockSpec((n_mem, head_dim), lambda b, h, i: (b, h))
    return pl.pallas_call(
        functools.partial(_attn_seq_kernel, scale=head_dim ** -0.5),
        grid=(batch, heads, n_t),
        in_specs=[pl.BlockSpec((tq, head_dim), lambda b, h, i: (b * n_t + i, q_col0 + h)), kv, kv],
        out_specs=pl.BlockSpec((tq, head_dim), lambda b, h, i: (b * n_t + i, h)),
        out_shape=jax.ShapeDtypeStruct((batch * seq, heads * head_dim), BF16),
        compiler_params=_params("parallel", "parallel", "arbitrary"),
        name="attn_seq",
    )(p2, mem_k, mem_v)


def _attn_cache_kernel(q_ref, k_ref, v_ref, o_ref, *, scale, bb, seq, heads, head_dim):
    tiles = head_dim // LANES
    slabs = tiles * heads
    n_mem = k_ref.shape[1] // slabs
    q = q_ref[...].astype(F32)

    def head_slab(ref, b, h):
        return jnp.concatenate([ref[b, pl.ds(t * heads + h, n_mem, stride=slabs), :] for t in range(tiles)], axis=1)

    rows = []
    for b in range(bb):
        cols = []
        for h in range(heads):
            qb = _b16(q[b * seq:(b + 1) * seq, h * head_dim:(h + 1) * head_dim])
            s = _dot_nt(qb, _b16(head_slab(k_ref, b, h))) * scale
            p = _softmax_rows(s)
            cols.append(_dot(_b16(p), _b16(head_slab(v_ref, b, h))))
        rows.append(jnp.concatenate(cols, axis=1))
    o_ref[...] = jnp.concatenate(rows, axis=0).astype(o_ref.dtype)


def attention_cache(p2, q_blk0, row0, cache_k, cache_v, *, batch, seq, heads, head_dim, bb):
    d = heads * head_dim
    kv = pl.BlockSpec((bb,) + cache_k.shape[1:], lambda g: (g, 0, 0))
    return pl.pallas_call(
        functools.partial(_attn_cache_kernel, scale=head_dim ** -0.5, bb=bb, seq=seq, heads=heads, head_dim=head_dim),
        grid=(batch // bb,),
        in_specs=[pl.BlockSpec((bb * seq, d), lambda g: (row0 + g, q_blk0)), kv, kv],
        out_specs=pl.BlockSpec((bb * seq, d), lambda g: (g, 0)),
        out_shape=jax.ShapeDtypeStruct((batch * seq, d), BF16),
        compiler_params=_params("parallel"),
        name="attn_cache",
    )(p2, cache_k, cache_v)


def _merge_kernel(ya1_ref, yb1_ref, ym1_ref, ya2_ref, yb2_ref, ym2_ref, g0_ref, g1_ref, g2_ref,
                  wa_ref, wb_ref, wm_ref, o_ref, *, n_first):
    def merge(ya_ref, yb_ref, ym_ref):
        acc = jax.nn.sigmoid(g0_ref[...].astype(F32)) * _dot(_b16(ya_ref[...]), wa_ref[...])
        acc += jax.nn.sigmoid(g1_ref[...].astype(F32)) * _dot(_b16(yb_ref[...]), wb_ref[...])
        acc += jax.nn.sigmoid(g2_ref[...].astype(F32)) * _dot(_b16(ym_ref[...]), wm_ref[...])
        o_ref[...] = acc.astype(o_ref.dtype)

    i = pl.program_id(1)

    @pl.when(i < n_first)
    def _():
        merge(ya1_ref, yb1_ref, ym1_ref)

    @pl.when(i >= n_first)
    def _():
        merge(ya2_ref, yb2_ref, ym2_ref)


def merge_branches(first, second, p2, gate_col0, wa, wb, wm, *, tm, tn):
    m1, d = first[0].shape
    m2 = second[0].shape[0]
    n1 = m1 // tm
    nb = d // tn
    act1 = pl.BlockSpec((tm, d), lambda j, i: (jnp.minimum(i, n1 - 1), 0))
    act2 = pl.BlockSpec((tm, d), lambda j, i: (jnp.maximum(i - n1, 0), 0))
    wsp = pl.BlockSpec((d, tn), lambda j, i: (0, j))

    def gate(k):
        return pl.BlockSpec((tm, tn), lambda j, i: (i, gate_col0 + k * nb + j))

    return pl.pallas_call(
        functools.partial(_merge_kernel, n_first=n1),
        grid=(nb, (m1 + m2) // tm),
        in_specs=[act1, act1, act1, act2, act2, act2, gate(0), gate(1), gate(2), wsp, wsp, wsp],
        out_specs=pl.BlockSpec((tm, tn), lambda j, i: (i, j)),
        out_shape=jax.ShapeDtypeStruct((m1 + m2, d), BF16),
        compiler_params=_params("parallel", "parallel"),
        name="merge",
    )(*first, *second, p2, p2, p2, wa, wb, wm)


def _layer_norm(y, g, b):
    mu = jnp.mean(y, axis=-1, keepdims=True)
    d = y - mu
    var = jnp.mean(d * d, axis=-1, keepdims=True)
    return d * lax.rsqrt(var + LN_EPS) * g + b


def _to_slabs(ref, row0, x):
    rows = x.shape[0]
    for j in range(ROW_SLABS):
        ref[pl.ds(row0 * ROW_PITCH + j, rows, stride=ROW_PITCH), :] = x[:, j * LANES:(j + 1) * LANES]
    for j in range(ROW_SLABS, ROW_PITCH):
        ref[pl.ds(row0 * ROW_PITCH + j, rows, stride=ROW_PITCH), :] = jnp.zeros((rows, LANES), x.dtype)


def _from_slabs(ref, row0, rows):
    return jnp.concatenate(
        [ref[pl.ds(row0 * ROW_PITCH + j, rows, stride=ROW_PITCH), :] for j in range(ROW_SLABS)], axis=1)


def _proj_ln_kernel(m_ref, xa_ref, xb_ref, wo_ref, g_ref, b_ref, rw_ref, rb_ref, x1_ref, lg_ref, *, alpha, n_first):
    h = _dot(m_ref[...], wo_ref[...])
    x = jnp.where(pl.program_id(0) < n_first, xa_ref[...], xb_ref[...])
    x1 = _layer_norm(alpha * x + h, g_ref[...], b_ref[...])
    _to_slabs(x1_ref, 0, x1)
    hi, mid, lo = _split3(x1)
    w_hi, w_mid, w_lo = _split3(rw_ref[...])
    lg = _dot(hi, w_hi) + (_dot(hi, w_mid) + _dot(mid, w_hi)) + (_dot(hi, w_lo) + _dot(mid, w_mid) + _dot(lo, w_hi))
    lg_ref[...] = lg + rb_ref[...]


def project_norm_route(merged, xa, xb, wo, ln_g, ln_b, router_w, router_b, *, alpha, tm):
    m, d = merged.shape
    n_r = router_w.shape[1]
    n1 = xa.shape[0] // tm
    row = pl.BlockSpec((tm, d), lambda i: (i, 0))
    vec = pl.BlockSpec((1, d), lambda i: (0, 0))
    return pl.pallas_call(
        functools.partial(_proj_ln_kernel, alpha=alpha, n_first=n1),
        grid=(m // tm,),
        in_specs=[row, pl.BlockSpec((tm, d), lambda i: (jnp.minimum(i, n1 - 1), 0)),
                  pl.BlockSpec((tm, d), lambda i: (jnp.maximum(i - n1, 0), 0)),
                  pl.BlockSpec((d, d), lambda i: (0, 0)), vec, vec,
                  pl.BlockSpec((d, n_r), lambda i: (0, 0)), pl.BlockSpec((1, n_r), lambda i: (0, 0))],
        out_specs=[pl.BlockSpec((tm * ROW_PITCH, LANES), lambda i: (i, 0)),
                   pl.BlockSpec((tm, n_r), lambda i: (i, 0))],
        out_shape=[jax.ShapeDtypeStruct((m * ROW_PITCH, LANES), F32), jax.ShapeDtypeStruct((m, n_r), F32)],
        compiler_params=_params("parallel"),
        name="proj_ln_route",
    )(merged, xa, xb, wo, ln_g, ln_b, router_w, router_b)


MOE_GROUP = 1280
MOE_ROW_TILE = 640
MOE_F_TILE = 256
MOE_N_TILE = 256
MOE_K_CHUNK = 256
COMBINE_TOKENS = 128


def _row_copy(src_hbm, src_row, dst_vmem, dst_row, sem):
    return pltpu.make_async_copy(
        src_hbm.at[pl.ds(pl.multiple_of(src_row * ROW_PITCH, 8), ROW_SLABS)],
        dst_vmem.at[pl.ds(pl.multiple_of(dst_row * ROW_PITCH, 8), ROW_SLABS)], sem)


def _start_row_gather(src_row, n_rows, src_hbm, buf, sem):
    def issue(r, carry):
        _row_copy(src_hbm, src_row(r), buf, r, sem).start()
        return carry

    lax.fori_loop(0, n_rows, issue, 0, unroll=8)


def _wait_row_gather(n_rows, src_hbm, buf, sem):
    n = n_rows * ROW_SLABS
    pltpu.make_async_copy(src_hbm.at[pl.ds(0, n)], buf.at[pl.ds(0, n)], sem).wait()


def _dispatch_kernel(tv_ref, tj_ref, tok_ref, x_hbm, o_ref, buf, sem):
    t = pl.program_id(0)
    n = pl.num_programs(0)
    slot = t % 2
    rt = MOE_ROW_TILE

    def start(tile, dst):
        j0 = tj_ref[tile]
        _start_row_gather(lambda r: tok_ref[j0 + r], rt, x_hbm, buf.at[dst], sem.at[dst])

    @pl.when((t == 0) & (tv_ref[0] > 0))
    def _first():
        start(0, 0)

    @pl.when((t + 1 < n) & (tv_ref[jnp.minimum(t + 1, n - 1)] > 0))
    def _next():
        start(jnp.minimum(t + 1, n - 1), 1 - slot)

    @pl.when(tv_ref[t] > 0)
    def _():
        _wait_row_gather(rt, x_hbm, buf.at[slot], sem.at[slot])
        o_ref[...] = _b16(_from_slabs(buf.at[slot], 0, rt))

    @pl.when(tv_ref[t] == 0)
    def _():
        o_ref[...] = jnp.zeros(o_ref.shape, o_ref.dtype)


def moe_dispatch(x_slabs, order, tile_valid, tile_first, d):
    n_tiles = tile_valid.shape[0]
    rt = MOE_ROW_TILE
    sorted_tok = jnp.concatenate([order // TOP_K, jnp.zeros((rt,), jnp.int32)])
    grid_spec = pltpu.PrefetchScalarGridSpec(
        num_scalar_prefetch=3,
        grid=(n_tiles,),
        in_specs=[pl.BlockSpec(memory_space=pl.ANY)],
        out_specs=pl.BlockSpec((rt, d), lambda t, tv, tj, od: (t, 0)),
        scratch_shapes=[pltpu.VMEM((2, rt * ROW_PITCH, LANES), F32), pltpu.SemaphoreType.DMA((2,))],
    )
    return pl.pallas_call(
        _dispatch_kernel,
        grid_spec=grid_spec,
        out_shape=jax.ShapeDtypeStruct((n_tiles * rt, d), BF16),
        compiler_params=_params("arbitrary"),
        name="moe_dispatch",
    )(tile_valid, tile_first, sorted_tok, x_slabs)


def _moe_kernel(ie_ref, ib_ref, nr_ref, j0_ref, x_ref, wg_ref, wu_ref, wd_ref, bg_ref, bu_ref, bd_ref,
                y_hbm, h_sc, wgu_sc, wd_sc, o_sc, pend_sc, sem, *, n_f, n_n, n_assign):
    i = pl.program_id(0)
    s = pl.program_id(1)
    rows = nr_ref[i]
    rt = MOE_ROW_TILE
    tile_slabs = rt * ROW_PITCH
    n_tiles = (rows + rt - 1) // rt
    tf = MOE_F_TILE
    last_step = n_f + n_n - 1

    def tile_copy(r, row0):
        return pltpu.make_async_copy(
            o_sc.at[pl.ds(pl.multiple_of(r * tile_slabs, tile_slabs), tile_slabs)],
            y_hbm.at[pl.ds(pl.multiple_of((row0 + r * rt) * ROW_PITCH, 8), tile_slabs)], sem)

    def wait_pending():
        def one(k, carry):
            tile_copy(0, 0).wait()
            return carry

        lax.fori_loop(0, pend_sc[0], one, 0)
        pend_sc[0] = 0

    @pl.when((i == 0) & (s == 0))
    def _reset():
        pend_sc[0] = 0
        o_sc[...] = jnp.zeros(o_sc.shape, F32)

    @pl.when((rows > 0) & (s < n_f))
    def _hidden():
        def project(r):
            return _dot(x_ref[pl.ds(pl.multiple_of(r * rt, rt), rt), :], wgu_sc[...])

        def activate(h, r):
            hg = jnp.minimum(h[:, 0:tf] + bg_ref[0], SWIGLU_LIMIT)
            hu = jnp.clip(h[:, tf:2 * tf] + bu_ref[0], -SWIGLU_LIMIT, SWIGLU_LIMIT)
            h_sc[s, pl.ds(pl.multiple_of(r * rt, rt), rt), :] = _b16(hg * jax.nn.sigmoid(SWIGLU_ALPHA * hg) * (hu + 1.0))

        kc = MOE_K_CHUNK
        h0 = None
        for c in range(wg_ref.shape[1] // kc):
            ks = slice(c * kc, (c + 1) * kc)
            w_c = jnp.concatenate([_b16(wg_ref[0, ks, :]), _b16(wu_ref[0, ks, :])], axis=1)
            wgu_sc[ks, :] = w_c
            part = _dot(x_ref[0:rt, ks], w_c)
            h0 = part if h0 is None else h0 + part
        activate(h0, 0)

        def tile(r, carry):
            activate(project(r), r)
            return carry

        lax.fori_loop(1, n_tiles, tile, 0)

    @pl.when((rows > 0) & (s >= n_f))
    def _down():
        n = s - n_f

        @pl.when(s == n_f)
        def _():
            wait_pending()

        def project(r):
            sl = pl.ds(pl.multiple_of(r * rt, rt), rt)
            acc = _dot(h_sc[0, sl, :], wd_sc[0:tf, :])
            for f in range(1, n_f):
                acc += _dot(h_sc[f, sl, :], wd_sc[f * tf:(f + 1) * tf, :])
            return acc

        def emit(acc, r):
            val = acc + bd_ref[0]
            for jj in range(MOE_N_TILE // LANES):
                start = r * tile_slabs + n * (MOE_N_TILE // LANES) + jj
                o_sc[pl.ds(start, rt, stride=ROW_PITCH), :] = val[:, jj * LANES:(jj + 1) * LANES]

        acc0 = None
        for f in range(n_f):
            fs = slice(f * tf, (f + 1) * tf)
            w_f = _b16(wd_ref[0, fs, :])
            wd_sc[fs, :] = w_f
            part = _dot(h_sc[f, 0:rt, :], w_f)
            acc0 = part if acc0 is None else acc0 + part
        emit(acc0, 0)

        def tile(r, carry):
            emit(project(r), r)
            return carry

        lax.fori_loop(1, n_tiles, tile, 0)

        @pl.when(s == last_step)
        def _():
            row0 = j0_ref[i]

            def send(r, carry):
                tile_copy(r, row0).start()
                return carry

            lax.fori_loop(0, n_tiles, send, 0)
            pend_sc[0] = n_tiles

    @pl.when((i == pl.num_programs(0) - 1) & (s == last_step))
    def _finish():
        wait_pending()
        o_sc[0:tile_slabs, :] = jnp.zeros((tile_slabs, LANES), F32)
        tail = tile_copy(0, n_assign)
        tail.start()
        tail.wait()


def moe_experts(x_sorted, item_expert, item_block, item_rows, item_first, n_assign,
                w_gate_up, b_gate_up, w_down, b_down):
    n_items = item_expert.shape[0]
    d = x_sorted.shape[1]
    n_exp, d_ff, d_out = w_down.shape
    n_f = d_ff // MOE_F_TILE
    n_n = d_out // MOE_N_TILE

    def f_idx(i, s, nr):
        return jnp.minimum(jnp.where(nr[i] > 0, s, n_f - 1), n_f - 1)

    def n_idx(i, s, nr):
        return jnp.where(nr[i] > 0, jnp.maximum(s - n_f, 0), n_n - 1)

    grid_spec = pltpu.PrefetchScalarGridSpec(
        num_scalar_prefetch=4,
        grid=(n_items, n_f + n_n),
        in_specs=[
            pl.BlockSpec((MOE_GROUP, d), lambda i, s, ie, ib, nr, j0: (ib[i], 0)),
            pl.BlockSpec((1, d, MOE_F_TILE), lambda i, s, ie, ib, nr, j0: (ie[i], 0, f_idx(i, s, nr))),
            pl.BlockSpec((1, d, MOE_F_TILE), lambda i, s, ie, ib, nr, j0: (ie[i], 0, n_f + f_idx(i, s, nr))),
            pl.BlockSpec((1, d_ff, MOE_N_TILE), lambda i, s, ie, ib, nr, j0: (ie[i], 0, n_idx(i, s, nr))),
            pl.BlockSpec((1, 1, MOE_F_TILE), lambda i, s, ie, ib, nr, j0: (ie[i], 0, f_idx(i, s, nr))),
            pl.BlockSpec((1, 1, MOE_F_TILE), lambda i, s, ie, ib, nr, j0: (ie[i], 0, n_f + f_idx(i, s, nr))),
            pl.BlockSpec((1, 1, MOE_N_TILE), lambda i, s, ie, ib, nr, j0: (ie[i], 0, n_idx(i, s, nr))),
        ],
        out_specs=pl.BlockSpec(memory_space=pl.ANY),
        scratch_shapes=[pltpu.VMEM((n_f, MOE_GROUP, MOE_F_TILE), BF16),
                        pltpu.VMEM((d, 2 * MOE_F_TILE), BF16),
                        pltpu.VMEM((d_ff, MOE_N_TILE), BF16),
                        pltpu.VMEM((MOE_GROUP * ROW_PITCH, LANES), F32),
                        pltpu.SMEM((1,), jnp.int32),
                        pltpu.SemaphoreType.DMA],
    )
    return pl.pallas_call(
        functools.partial(_moe_kernel, n_f=n_f, n_n=n_n, n_assign=n_assign),
        grid_spec=grid_spec,
        out_shape=jax.ShapeDtypeStruct(((n_assign + MOE_ROW_TILE) * ROW_PITCH, LANES), F32),
        compiler_params=_params("arbitrary", "arbitrary"),
        name="moe_experts",
    )(item_expert, item_block, item_rows, item_first, x_sorted, w_gate_up, w_gate_up, w_down,
      b_gate_up.reshape(n_exp, 1, 2 * d_ff), b_gate_up.reshape(n_exp, 1, 2 * d_ff), b_down.reshape(n_exp, 1, d_out))


def _combine_kernel(pos_ref, nxt_ref, y_hbm, x1_ref, gate_ref, g_ref, b_ref, oa_ref, ob_ref, buf, sem, *, alpha, n_a):
    tm = COMBINE_TOKENS
    t = pl.program_id(0)
    slot = t % 2

    @pl.when(t == 0)
    def _first():
        _start_row_gather(lambda r: pos_ref[0, 0, r], tm * TOP_K, y_hbm, buf.at[0], sem.at[0])

    @pl.when(t + 1 < pl.num_programs(0))
    def _next():
        _start_row_gather(lambda r: nxt_ref[0, 0, r], tm * TOP_K, y_hbm, buf.at[1 - slot], sem.at[1 - slot])

    cur = buf.at[slot]
    _wait_row_gather(tm * TOP_K, y_hbm, cur, sem.at[slot])

    def finish(o_ref):
        d_model = o_ref.shape[1]
        slabs = [slice(j * LANES, (j + 1) * LANES) for j in range(ROW_SLABS)]
        gate = gate_ref[...]
        gk = [jnp.broadcast_to(gate[:, k:k + 1], (tm, LANES)) for k in range(TOP_K)]
        total = jnp.zeros((tm, 1), F32)
        for j, sl in enumerate(slabs):
            y = alpha * x1_ref[pl.ds(j, tm, stride=ROW_PITCH), :]
            for k in range(TOP_K):
                y += gk[k] * cur[pl.ds(k * tm * ROW_PITCH + j, tm, stride=ROW_PITCH), :]
            o_ref[:, sl] = y
            total += jnp.sum(y, axis=1, keepdims=True)
        mean = total * (1.0 / d_model)
        sq = jnp.zeros((tm, 1), F32)
        for sl in slabs:
            d = o_ref[:, sl] - mean
            sq += jnp.sum(d * d, axis=1, keepdims=True)
        rstd = lax.rsqrt(sq * (1.0 / d_model) + LN_EPS)
        for sl in slabs:
            o_ref[:, sl] = (o_ref[:, sl] - mean) * rstd * g_ref[:, sl] + b_ref[:, sl]

    @pl.when(t < n_a)
    def _():
        finish(oa_ref)

    @pl.when(t >= n_a)
    def _():
        finish(ob_ref)


def moe_combine_norm(y_slabs, pos, gate, x1_slabs, ln_g, ln_b, *, alpha, n_first):
    n_tok = pos.shape[0]
    tm = COMBINE_TOKENS
    d = ln_g.shape[1]
    n_t = n_tok // tm
    n_a = n_first // tm
    vec = pl.BlockSpec((1, d), lambda t: (0, 0))
    idx = pos.reshape(n_t, tm, TOP_K).transpose(0, 2, 1).reshape(n_t, 1, tm * TOP_K)
    return pl.pallas_call(
        functools.partial(_combine_kernel, alpha=alpha, n_a=n_a),
        grid=(n_t,),
        in_specs=[pl.BlockSpec((1, 1, tm * TOP_K), lambda t: (t, 0, 0), memory_space=pltpu.SMEM),
                  pl.BlockSpec((1, 1, tm * TOP_K), lambda t: (jnp.minimum(t + 1, n_t - 1), 0, 0),
                               memory_space=pltpu.SMEM),
                  pl.BlockSpec(memory_space=pl.ANY),
                  pl.BlockSpec((tm * ROW_PITCH, LANES), lambda t: (t, 0)),
                  pl.BlockSpec((tm, TOP_K), lambda t: (t, 0)), vec, vec],
        out_specs=[pl.BlockSpec((tm, d), lambda t: (jnp.minimum(t, n_a - 1), 0)),
                   pl.BlockSpec((tm, d), lambda t: (jnp.maximum(t - n_a, 0), 0))],
        out_shape=[jax.ShapeDtypeStruct((n_first, d), F32), jax.ShapeDtypeStruct((n_tok - n_first, d), F32)],
        scratch_shapes=[pltpu.VMEM((2, tm * TOP_K * ROW_PITCH, LANES), F32), pltpu.SemaphoreType.DMA((2,))],
        compiler_params=_params("arbitrary"),
        name="moe_combine",
    )(idx, idx, y_slabs, x1_slabs, gate, ln_g, ln_b)


def _route(logits, n_items):
    n_tok = logits.shape[0]
    n_assign = n_tok * TOP_K
    tiles_per_item = MOE_GROUP // MOE_ROW_TILE
    top_logit, top_idx = lax.top_k(logits, TOP_K)
    gate = jax.nn.softmax(top_logit, axis=-1)
    flat_e = top_idx.reshape(-1).astype(jnp.int32)
    order = jnp.argsort(flat_e).astype(jnp.int32)
    rank = jnp.argsort(order).astype(jnp.int32)
    experts = jnp.arange(N_EXPERTS, dtype=jnp.int32)
    onehot = flat_e[:, None] == experts[None, :]
    counts = jnp.sum(onehot, axis=0, dtype=jnp.int32)
    padded = (counts + MOE_GROUP - 1) // MOE_GROUP * MOE_GROUP
    start = jnp.cumsum(counts) - counts
    padded_end = jnp.cumsum(padded)
    padded_start = padded_end - padded

    def expert_of(row0):
        return jnp.minimum(jnp.sum(row0[:, None] >= padded_end[None, :], axis=1, dtype=jnp.int32), N_EXPERTS - 1)

    def table(values, e):
        return jnp.sum(jnp.where(e[:, None] == experts[None, :], values[None, :], 0), axis=1)

    n_real = (padded_end[-1] // MOE_GROUP).astype(jnp.int32)
    item = jnp.minimum(jnp.arange(n_items, dtype=jnp.int32), jnp.maximum(n_real - 1, 0))
    item_expert = expert_of(item * MOE_GROUP)
    filled = jnp.clip(table(padded_start + counts, item_expert) - item * MOE_GROUP, 0, MOE_GROUP)
    item_rows = jnp.where(jnp.arange(n_items) < n_real, filled, 0).astype(jnp.int32)
    item_first = jnp.clip(table(start - padded_start, item_expert) + item * MOE_GROUP, 0, n_assign - 1)
    tile_row0 = (jnp.arange(tiles_per_item, dtype=jnp.int32) * MOE_ROW_TILE)[None, :]
    tile_valid = jnp.clip(item_rows[:, None] - tile_row0, 0, MOE_ROW_TILE).reshape(-1).astype(jnp.int32)
    tile_first = jnp.clip(item_first[:, None] + tile_row0, 0, n_assign - 1).reshape(-1).astype(jnp.int32)
    return (gate, order, rank.reshape(n_tok, TOP_K), item_expert, item.astype(jnp.int32), item_rows,
            item_first.astype(jnp.int32), tile_valid, tile_first)


def _pad_rw_cols(a, rw_cols, pad):
    lead = a.shape[:-1]
    return jnp.concatenate([a[..., :rw_cols], jnp.zeros(lead + (pad,), a.dtype), a[..., rw_cols:]], axis=-1)


def _cache_rows(c, heads, head_dim):
    b, n_mem = c.shape[:2]
    tiles = head_dim // LANES
    c = c.reshape(b, n_mem, heads, tiles, LANES).transpose(0, 1, 3, 2, 4)
    return c.reshape(b, n_mem * tiles * heads, LANES)


def _layer(xp, xs, mem_prompt, cache_k, cache_v, st_rwkv, st_shift, st_conv, wts, depth):
    (w_in, w_w_up, w0, w_a_up, a0, w_g_up, tshift_mu, k_k, k_a, r_k, gn_g, gn_b, conv_w,
     w_mem_k, w_mem_v, w_proj_a, w_proj_b, w_proj_m, w_o, ln1_g, ln1_b,
     router_w, router_b, w_gate_up, b_gate_up, w_down, b_down, ln2_g, ln2_b) = wts
    bp, sp, d = xp.shape
    bs, ss, _ = xs.shape
    n_mem = mem_prompt.shape[1]
    mem_heads, mem_head = cache_k.shape[-2:]
    n_p, n_s = bp * sp, bs * ss
    n_tok = n_p + n_s
    alpha = (2.0 * depth) ** 0.25
    rw_cols = 3 * d + DECAY_LORA + AICL_LORA + GATE_LORA
    pad = LORA_PAD - (rw_cols - 3 * d)
    rw_pad = rw_cols + pad

    xp2, xs2 = xp.reshape(n_p, d), xs.reshape(n_s, d)
    x_bf = _b16(jnp.concatenate([xp2, xs2], axis=0))
    mu_pad = _pad_rw_cols(tshift_mu, rw_cols, pad)[None, :rw_pad]
    rest_cols = w_in.shape[1] - rw_cols
    w_in_t = jnp.swapaxes(w_in, 0, 1)

    p_rw = matmul_f32_weights(x_bf, w_in_t, rw_pad, BF16, 1024, 512, "in_proj_rw")
    p2 = matmul_shifted_weights(x_bf, w_in_t, rw_cols, rest_cols, BF16, 1024, 1024, "in_proj_rest")
    prev_s = matmul_f32_weights(st_shift, w_in_t, rw_pad, F32, bs, 512, "prev_proj").reshape(bs, 1, rw_pad)
    prev_p = jnp.zeros((bp, 1, rw_pad), F32)

    def lora_rows(w, row0):
        return _b16(jnp.zeros((LORA_PAD, d), F32).at[row0:row0 + w.shape[0]].set(w))

    ww = lora_rows(w_w_up, 0)
    wa = lora_rows(w_a_up, DECAY_LORA)
    wg = lora_rows(w_g_up, DECAY_LORA + AICL_LORA)
    r2 = lambda v: v.reshape(1, d)
    rw_vecs = (mu_pad, r2(w0), r2(a0), r2(k_k), r2(k_a), r2(r_k), r2(gn_g), r2(gn_b), ww, wa, wg)
    heads = d // RW_HEAD
    ya_p, rw_p = rwkv_time_mix(p_rw, prev_p, jnp.zeros((bp, heads, RW_HEAD, RW_HEAD), F32), *rw_vecs,
                               batch=bp, seq=sp, row_block0=0, pairs=8, has_state=False)
    ya_s, rw_s = rwkv_time_mix(p_rw, prev_s, st_rwkv, *rw_vecs,
                               batch=bs, seq=ss, row_block0=n_p // RW_ROWS, pairs=4, has_state=True)

    yb_p, cv_p = conv_sequences(p2, conv_w, batch=bp, seq=sp, width=d, rows=256, tn=512)
    yb_s, cv_s0, cv_s1 = conv_short(p2, st_conv[:, 0, :], st_conv[:, 1, :], conv_w, row_block0=n_p // 128,
                                    batch=bs, seq=ss, width=d, rows=128, tn=512)
    cv_s = jnp.stack([cv_s0, cv_s1], axis=1)

    mem_in = _b16(mem_prompt.reshape(bp * n_mem, d))
    mk = matmul(mem_in, _b16(w_mem_k), F32, bp * n_mem, 512, "mem_k")
    mv = matmul(mem_in, _b16(w_mem_v), F32, bp * n_mem, 512, "mem_v")
    q_col0 = 3 * d // mem_head
    ym_p = attention_sequences(p2, q_col0, mk, mv, batch=bp, seq=sp, n_mem=n_mem, heads=mem_heads,
                               head_dim=mem_head, tq=512)
    bb = 4
    ym_s = attention_cache(p2, 3 * d // d, n_p // (bb * ss), _cache_rows(cache_k, mem_heads, mem_head),
                           _cache_rows(cache_v, mem_heads, mem_head),
                           batch=bs, seq=ss, heads=mem_heads, head_dim=mem_head, bb=bb)

    merged = merge_branches((ya_p, yb_p, ym_p), (ya_s, yb_s, ym_s), p2, 4 * d // 512, _b16(w_proj_a),
                            _b16(w_proj_b), _b16(w_proj_m), tm=512, tn=512)
    rw_pad_r = jnp.zeros((d, LANES), F32).at[:, :N_EXPERTS].set(router_w)
    rb_pad_r = jnp.zeros((1, LANES), F32).at[0, :N_EXPERTS].set(router_b)
    x1_slabs, logits = project_norm_route(merged, xp2, xs2, _b16(w_o), r2(ln1_g), r2(ln1_b),
                                          rw_pad_r, rb_pad_r, alpha=alpha, tm=256)

    n_items = n_tok * TOP_K // MOE_GROUP + N_EXPERTS
    (gate, order, rank, item_expert, item_block, item_rows, item_first, tile_valid,
     tile_first) = _route(logits[:, :N_EXPERTS], n_items)
    x_sorted = moe_dispatch(x1_slabs, order, tile_valid, tile_first, d)
    y_slabs = moe_experts(x_sorted, item_expert, item_block, item_rows, item_first, n_tok * TOP_K,
                          w_gate_up, b_gate_up, w_down, b_down)
    y_p, y_s = moe_combine_norm(y_slabs, rank, gate, x1_slabs, r2(ln2_g), r2(ln2_b), alpha=alpha, n_first=n_p)
    y_p = y_p.reshape(bp, sp, d)
    y_s = y_s.reshape(bs, ss, d)
    mk5 = mk.reshape(bp, n_mem, mem_heads, mem_head)
    mv5 = mv.reshape(bp, n_mem, mem_heads, mem_head)
    return y_p, y_s, mk5, mv5, rw_p, xp[:, -1], cv_p, rw_s, xs[:, -1], cv_s


def kernel(x_prompt, x_sample, mem_prompt, cache_mem_k, cache_mem_v, state_rwkv, state_shift, state_conv, w_in, w_w_up, w0, w_a_up, a0, w_g_up, tshift_mu, k_k, k_a, r_k, gn_g, gn_b, conv_w, w_mem_k, w_mem_v, w_proj_a, w_proj_b, w_proj_m, w_o, ln1_g, ln1_b, router_w, router_b, w_gate_up, b_gate_up, w_down, b_down, ln2_g, ln2_b):
    weights = (w_in, w_w_up, w0, w_a_up, a0, w_g_up, tshift_mu, k_k, k_a, r_k, gn_g, gn_b, conv_w,
               w_mem_k, w_mem_v, w_proj_a, w_proj_b, w_proj_m, w_o, ln1_g, ln1_b,
               router_w, router_b, w_gate_up, b_gate_up, w_down, b_down, ln2_g, ln2_b)
    depth = w_in.shape[0]
    yp, ys = x_prompt, x_sample
    outs = [[] for _ in range(8)]
    for l in range(depth):
        res = _layer(yp, ys, mem_prompt, cache_mem_k[l], cache_mem_v[l], state_rwkv[l], state_shift[l],
                     state_conv[l], tuple(w[l] for w in weights), depth)
        yp, ys = res[0], res[1]
        for acc, r in zip(outs, res[2:]):
            acc.append(r)
    return (yp, ys) + tuple(jnp.stack(o) for o in outs)
```
